```python
import jax, jax.numpy as jnp
from jax import lax
import numpy as np

D_MODEL = 1024
BATCH = 4
SEQ = 4096
DEPTH = 2
DEC_BATCH = 32
DEC_SEQ = 16
PAST_LEN = 1024

CHUNK = 64
D_FF = 2816
N_HEADS = 16
N_KV_HEADS = 4
HEAD_DIM = 64
GROUP = N_HEADS // N_KV_HEADS
WINDOW = 128
WIN_CHUNKS = WINDOW // CHUNK
BAND = (WIN_CHUNKS + 1) * CHUNK
ATTN_Q = N_HEADS * HEAD_DIM
ATTN_KV = N_KV_HEADS * HEAD_DIM
SCALE = HEAD_DIM ** -0.5
HG_HEADS = 8
HG_DK = 128
HG_DV = D_MODEL // HG_HEADS
HG_FDIM = HG_HEADS * HG_DK
HG_IDIM = HG_HEADS * HG_DV
HG_BLOCK = 16
LOG_F_FLOOR = -60.0
LRU_WIDTH = D_MODEL
LRU_BLOCKS = 16
LRU_BW = LRU_WIDTH // LRU_BLOCKS
CONV_WIDTH = 4
LRU_C = 8.0
SPLIT_SIZES = (ATTN_Q, ATTN_KV, ATTN_KV, HG_FDIM, HG_FDIM, HG_IDIM, HG_IDIM, LRU_WIDTH, LRU_WIDTH, D_MODEL, D_MODEL, D_MODEL)
IN_COLS = ATTN_Q + 2 * ATTN_KV + 2 * HG_FDIM + 2 * HG_IDIM + 2 * LRU_WIDTH + 3 * D_MODEL
EPS = 1e-6
NEG = -1e30

kernel_name = "hybrid_streaming_swa_hgrn2_rglru_step"


def rms_norm(x, g):
    xf = x.astype(jnp.float32)
    y = xf * lax.rsqrt(jnp.mean(xf * xf, axis=-1, keepdims=True) + EPS)
    return y.astype(x.dtype) * g


def swiglu(h, w_up, w_down):
    gate, val = jnp.split(h @ w_up, 2, axis=-1)
    return (jax.nn.silu(gate) * val) @ w_down


def sink_softmax(s, sinks):
    m = jnp.maximum(jnp.max(s, axis=-1, keepdims=True), sinks)
    p = jnp.exp(s - m)
    return p / (jnp.sum(p, axis=-1, keepdims=True) + jnp.exp(sinks - m))


def swa_prompt(q, k, v, sinks):
    b, t = q.shape[:2]
    nc = t // CHUNK
    qc = q.reshape(b, nc, CHUNK, N_KV_HEADS, GROUP, HEAD_DIM)

    def band(a):
        ac = a.reshape(b, nc, CHUNK, N_KV_HEADS, HEAD_DIM)
        ap = jnp.pad(ac, ((0, 0), (WIN_CHUNKS, 0), (0, 0), (0, 0), (0, 0)))
        return jnp.concatenate([ap[:, j:j + nc] for j in range(WIN_CHUNKS + 1)], axis=2)

    kb, vb = band(k), band(v)
    valid = (np.arange(nc)[:, None] - WIN_CHUNKS + (np.arange(BAND) // CHUNK)[None, :]) >= 0
    s = jnp.einsum('bnqhgd,bnkhd->bnhgqk', qc, kb).astype(jnp.float32) * SCALE
    s = jnp.where(valid[None, :, None, None, None, :], s, NEG)
    sk = sinks.astype(jnp.float32).reshape(N_KV_HEADS, GROUP)[None, None, :, :, None, None]
    p = sink_softmax(s, sk)
    o = jnp.einsum('bnhgqk,bnkhd->bnqhgd', p.astype(v.dtype), vb)
    return o.reshape(b, t, ATTN_Q)


def swa_sample(q, k, v, ck, cv, sinks):
    b, t = q.shape[:2]
    keys = jnp.concatenate([ck.astype(k.dtype), k], axis=1)
    vals = jnp.concatenate([cv.astype(v.dtype), v], axis=1)
    qg = q.reshape(b, t, N_KV_HEADS, GROUP, HEAD_DIM)
    s = jnp.einsum('bqhgd,bkhd->bhgqk', qg, keys).astype(jnp.float32) * SCALE
    sk = sinks.astype(jnp.float32).reshape(N_KV_HEADS, GROUP)[None, :, :, None, None]
    p = sink_softmax(s, sk)
    o = jnp.einsum('bhgqk,bkhd->bqhgd', p.astype(v.dtype), vals)
    return o.reshape(b, t, ATTN_Q)


def hgrn2_blocks(q, logf, k, v, s0):
    bsz, t, h, dk = q.shape
    dv = v.shape[-1]
    pad = (-t) % HG_BLOCK
    if pad:
        pw = ((0, 0), (0, pad), (0, 0), (0, 0))
        q, logf, k, v = (jnp.pad(a, pw) for a in (q, logf, k, v))
    n = (t + pad) // HG_BLOCK

    def blocks(a):
        return jnp.moveaxis(a.reshape(bsz, n, HG_BLOCK, h, a.shape[-1]), 1, 0)

    causal = jnp.tril(jnp.ones((HG_BLOCK, HG_BLOCK), dtype=bool))[None, :, :, None, None]

    def step(s, blk):
        qb, gb, kb, vb = blk
        bc = jnp.cumsum(gb, axis=1)
        diff = jnp.where(causal, bc[:, :, None] - bc[:, None, :], NEG)
        att = jnp.einsum('blhk,blmhk,bmhk->bhlm', qb, jnp.exp(diff), kb)
        o = jnp.einsum('blhk,bhkv->blhv', qb * jnp.exp(bc), s) + jnp.einsum('bhlm,bmhv->blhv', att, vb)
        b_last = bc[:, -1]
        s_new = jnp.exp(b_last)[..., None] * s + jnp.einsum('blhk,blhv->bhkv', kb * jnp.exp(b_last[:, None] - bc), vb)
        return s_new, o

    s_fin, o = lax.scan(step, s0, (blocks(q), blocks(logf), blocks(k), blocks(v)))
    o = jnp.moveaxis(o, 0, 1).reshape(bsz, n * HG_BLOCK, h, dv)[:, :t]
    return o, s_fin


def rglru_scan(log_a, bx, h0):
    a = jnp.exp(log_a)
    b = jnp.sqrt(-jnp.expm1(2.0 * log_a)) * bx

    def combine(c1, c2):
        return c1[0] * c2[0], c2[0] * c1[1] + c2[1]

    a_cum, b_cum = lax.associative_scan(combine, (a, b), axis=1)
    return a_cum * h0[:, None] + b_cum


def mixer(h, p, lb, cache, win_rows):
    bsz, t, _ = h.shape
    dt = h.dtype
    idx = np.cumsum(SPLIT_SIZES)[:-1].tolist()
    (aq, ak, av, hq, hf, hi, hg, lx, lg, ga, gb, gc) = jnp.split(h @ p['w_in'], idx, axis=-1)

    q = rms_norm(aq.reshape(bsz, t, N_HEADS, HEAD_DIM), p['q_norm'])
    k = rms_norm(ak.reshape(bsz, t, N_KV_HEADS, HEAD_DIM), p['k_norm'])
    v = av.reshape(bsz, t, N_KV_HEADS, HEAD_DIM)
    if cache is None:
        ya = swa_prompt(q, k, v, p['attn_sinks'])
        k_rows, v_rows = k[:, t - win_rows:], v[:, t - win_rows:]
        s0 = jnp.zeros((bsz, HG_HEADS, HG_DK, HG_DV), jnp.float32)
        conv_buf = jnp.zeros((bsz, CONV_WIDTH - 1, LRU_WIDTH), dt)
        h0 = jnp.zeros((bsz, LRU_WIDTH), jnp.float32)
    else:
        ck, cv, s0, conv_buf, h0 = cache
        ya = swa_sample(q, k, v, ck, cv, p['attn_sinks'])
        k_rows, v_rows = k, v
        s0 = s0.astype(jnp.float32)
        conv_buf = conv_buf.astype(dt)
        h0 = h0.astype(jnp.float32)

    lbh = lb.reshape(HG_HEADS, HG_DK)
    hf32 = hf.astype(jnp.float32).reshape(bsz, t, HG_HEADS, HG_DK)
    f = lbh + (1.0 - lbh) * jax.nn.sigmoid(hf32)
    logf = jnp.maximum(jnp.log(jnp.maximum(f, 1e-26)), LOG_F_FLOOR)
    kk = (1.0 - lbh) * jax.nn.sigmoid(-hf32)
    qq = jax.nn.silu(hq).astype(jnp.float32).reshape(bsz, t, HG_HEADS, HG_DK)
    vv = hi.astype(jnp.float32).reshape(bsz, t, HG_HEADS, HG_DV)
    o, s_new = hgrn2_blocks(qq, logf, kk, vv, s0)
    o = rms_norm(o.astype(dt), p['hgrn_o_norm']) * jax.nn.silu(hg.reshape(bsz, t, HG_HEADS, HG_DV))
    yb = o.reshape(bsz, t, HG_IDIM)

    xpad = jnp.concatenate([conv_buf, lx], axis=1)
    xc = sum(xpad[:, j:j + t] * p['conv_w'][j] for j in range(CONV_WIDTH)) + p['conv_b']
    new_buf = xpad[:, t:]
    xr = xc.reshape(bsz, t, LRU_BLOCKS, LRU_BW)
    r = jax.nn.sigmoid(jnp.einsum('btnc,ncd->btnd', xr, p['lru_w_a']).reshape(bsz, t, LRU_WIDTH) + p['lru_b_a'])
    ig = jax.nn.sigmoid(jnp.einsum('btnc,ncd->btnd', xr, p['lru_w_x']).reshape(bsz, t, LRU_WIDTH) + p['lru_b_x'])
    log_a = -LRU_C * r.astype(jnp.float32) * jax.nn.softplus(-p['lru_lambda'].astype(jnp.float32))
    hs = rglru_scan(log_a, (ig * xc).astype(jnp.float32), h0)
    yc = hs.astype(dt) * jax.nn.gelu(lg)

    merged = (jax.nn.sigmoid(ga) * (ya @ p['w_attn_o'])
              + jax.nn.sigmoid(gb) * (yb @ p['w_hgrn_o'])
              + jax.nn.sigmoid(gc) * (yc @ p['w_lru_o']))
    out = merged @ p['w_out']
    return out, (k_rows, v_rows, s_new, new_buf, hs[:, -1])


def layer_forward(x, p, lb, cache, win_rows):
    x = x + 0.5 * swiglu(rms_norm(x, p['norm_ffn1']), p['w_ffn1_up'], p['w_ffn1_down'])
    m, st = mixer(rms_norm(x, p['norm_mix']), p, lb, cache, win_rows)
    x = x + m
    x = x + 0.5 * swiglu(rms_norm(x, p['norm_ffn2']), p['w_ffn2_up'], p['w_ffn2_down'])
    return x, st


def setup_inputs(seed: int = 0) -> dict:
    key = jax.random.key(seed)
    ks = iter(jax.random.split(key, 48))

    def nrm(shape, scale):
        return scale * jax.random.normal(next(ks), shape, jnp.float32)

    def gain(shape):
        return 1.0 + 0.01 * jax.random.normal(next(ks), shape, jnp.float32)

    win_rows = min(WINDOW, PAST_LEN)
    u = jax.random.uniform(next(ks), (DEPTH, LRU_WIDTH), jnp.float32, 0.9, 0.999)
    a_base = u ** (1.0 / LRU_C)
    lru_lambda = jnp.log(a_base) - jnp.log1p(-a_base)
    return {
        'x_prompt': nrm((BATCH, SEQ, D_MODEL), 1.0),
        'x_sample': nrm((DEC_BATCH, DEC_SEQ, D_MODEL), 1.0),
        'cache_attn_k': nrm((DEPTH, DEC_BATCH, win_rows, N_KV_HEADS, HEAD_DIM), 1.0),
        'cache_attn_v': nrm((DEPTH, DEC_BATCH, win_rows, N_KV_HEADS, HEAD_DIM), 1.0),
        'state_hgrn': nrm((DEPTH, DEC_BATCH, HG_HEADS, HG_DK, HG_DV), 0.3),
        'state_conv': nrm((DEPTH, DEC_BATCH, CONV_WIDTH - 1, LRU_WIDTH), 1.0),
        'state_lru': nrm((DEPTH, DEC_BATCH, LRU_WIDTH), 0.5),
        'norm_ffn1': gain((DEPTH, D_MODEL)),
        'w_ffn1_up': nrm((DEPTH, D_MODEL, 2 * D_FF), D_MODEL ** -0.5),
        'w_ffn1_down': nrm((DEPTH, D_FF, D_MODEL), D_FF ** -0.5),
        'norm_mix': gain((DEPTH, D_MODEL)),
        'w_in': nrm((DEPTH, D_MODEL, IN_COLS), D_MODEL ** -0.5),
        'q_norm': gain((DEPTH, HEAD_DIM)),
        'k_norm': gain((DEPTH, HEAD_DIM)),
        'attn_sinks': nrm((DEPTH, N_HEADS), 0.5),
        'w_attn_o': nrm((DEPTH, ATTN_Q, D_MODEL), ATTN_Q ** -0.5),
        'hgrn_lb_logits': nrm((DEPTH, HG_FDIM), 0.5),
        'hgrn_o_norm': gain((DEPTH, HG_DV)),
        'w_hgrn_o': nrm((DEPTH, HG_IDIM, D_MODEL), HG_IDIM ** -0.5),
        'conv_w': nrm((DEPTH, CONV_WIDTH, LRU_WIDTH), CONV_WIDTH ** -0.5),
        'conv_b': nrm((DEPTH, LRU_WIDTH), 0.01),
        'lru_w_a': nrm((DEPTH, LRU_BLOCKS, LRU_BW, LRU_BW), LRU_BW ** -0.5),
        'lru_b_a': nrm((DEPTH, LRU_WIDTH), 0.01),
        'lru_w_x': nrm((DEPTH, LRU_BLOCKS, LRU_BW, LRU_BW), LRU_BW ** -0.5),
        'lru_b_x': nrm((DEPTH, LRU_WIDTH), 0.01),
        'lru_lambda': lru_lambda,
        'w_lru_o': nrm((DEPTH, LRU_WIDTH, D_MODEL), LRU_WIDTH ** -0.5),
        'w_out': nrm((DEPTH, D_MODEL, D_MODEL), D_MODEL ** -0.5),
        'norm_ffn2': gain((DEPTH, D_MODEL)),
        'w_ffn2_up': nrm((DEPTH, D_MODEL, 2 * D_FF), D_MODEL ** -0.5),
        'w_ffn2_down': nrm((DEPTH, D_FF, D_MODEL), D_FF ** -0.5),
    }


def reference(x_prompt, x_sample, cache_attn_k, cache_attn_v, state_hgrn, state_conv, state_lru,
              norm_ffn1, w_ffn1_up, w_ffn1_down, norm_mix, w_in, q_norm, k_norm, attn_sinks, w_attn_o,
              hgrn_lb_logits, hgrn_o_norm, w_hgrn_o, conv_w, conv_b, lru_w_a, lru_b_a, lru_w_x, lru_b_x,
              lru_lambda, w_lru_o, w_out, norm_ffn2, w_ffn2_up, w_ffn2_down):
    win_rows = cache_attn_k.shape[2]
    lb_prob = jax.nn.softmax(hgrn_lb_logits.astype(jnp.float32), axis=0)
    lb_all = jnp.cumsum(lb_prob, axis=0) - lb_prob[0:1]

    xp, xs = x_prompt, x_sample
    st_p_all, st_s_all = [], []
    for l in range(DEPTH):
        p = {
            'norm_ffn1': norm_ffn1[l], 'w_ffn1_up': w_ffn1_up[l], 'w_ffn1_down': w_ffn1_down[l],
            'norm_mix': norm_mix[l], 'w_in': w_in[l], 'q_norm': q_norm[l], 'k_norm': k_norm[l],
            'attn_sinks': attn_sinks[l], 'w_attn_o': w_attn_o[l], 'hgrn_o_norm': hgrn_o_norm[l],
            'w_hgrn_o': w_hgrn_o[l], 'conv_w': conv_w[l], 'conv_b': conv_b[l], 'lru_w_a': lru_w_a[l],
            'lru_b_a': lru_b_a[l], 'lru_w_x': lru_w_x[l], 'lru_b_x': lru_b_x[l], 'lru_lambda': lru_lambda[l],
            'w_lru_o': w_lru_o[l], 'w_out': w_out[l], 'norm_ffn2': norm_ffn2[l],
            'w_ffn2_up': w_ffn2_up[l], 'w_ffn2_down': w_ffn2_down[l],
        }
        xp, st_p = layer_forward(xp, p, lb_all[l], None, win_rows)
        xs, st_s = layer_forward(xs, p, lb_all[l],
                                 (cache_attn_k[l], cache_attn_v[l], state_hgrn[l], state_conv[l], state_lru[l]),
                                 win_rows)
        st_p_all.append(st_p)
        st_s_all.append(st_s)

    def stack(sts, i):
        return jnp.stack([s[i] for s in sts], axis=0)

    return (xp, xs,
            stack(st_p_all, 0), stack(st_p_all, 1), stack(st_p_all, 2), stack(st_p_all, 3), stack(st_p_all, 4),
            stack(st_s_all, 0), stack(st_s_all, 1), stack(st_s_all, 2), stack(st_s_all, 3), stack(st_s_all, 4))
```

```python
import functools

import numpy as np
import jax
import jax.numpy as jnp
from jax import lax
from jax.experimental import pallas as pl
from jax.experimental.pallas import tpu as pltpu

F32 = jnp.float32
BF16 = jnp.bfloat16

CHUNK = 64
N_HEADS = 16
N_KV_HEADS = 4
HEAD_DIM = 64
GROUP = N_HEADS // N_KV_HEADS
WINDOW = 128
SCALE = HEAD_DIM ** -0.5
HG_HEADS = 8
HG_DK = 128
HG_DV = 128
LOG_F_FLOOR = -60.0
CONV_WIDTH = 4
LRU_C = 8.0
LRU_BW = 64
EPS = 1e-6
NEG = -1e30

MXU_DIM = 256
VMEM_LIMIT_BYTES = 56 * 1024 * 1024

_SLAB = 1024
(_C_AQ, _C_HQ, _C_HF, _C_HI, _C_HG, _C_LX, _C_LG, _C_GA, _C_GB, _C_GC) = range(10)
_KV_W = N_KV_HEADS * HEAD_DIM
_C_AK = 10 * _SLAB // _KV_W
_C_AV = _C_AK + 1


def _cparams(sem):
    return pltpu.CompilerParams(dimension_semantics=sem, vmem_limit_bytes=VMEM_LIMIT_BYTES)


def _rms(x, g):
    return x * lax.rsqrt(jnp.mean(x * x, axis=-1, keepdims=True) + EPS) * g


def _sigmoid(x):
    return 1.0 / (1.0 + jnp.exp(-x))


def _silu(x):
    return x * _sigmoid(x)


def _dot(a, b):
    return jnp.dot(a, b, preferred_element_type=F32)


def _dot_nt(a, b):
    return lax.dot_general(a, b, (((1,), (1,)), ((), ())), preferred_element_type=F32)


def _dot_tn(a, b):
    return lax.dot_general(a, b, (((0,), (0,)), ((), ())), preferred_element_type=F32)


def _ffn_kernel(x_ref, g_ref, wg_ref, wv_ref, wd_ref, o_ref, h_ref, acc_ref):
    j = pl.program_id(1)

    @pl.when(j == 0)
    def _():
        h_ref[...] = _rms(x_ref[...], g_ref[...]).astype(BF16)
        acc_ref[...] = jnp.zeros_like(acc_ref)

    h = h_ref[...]
    gate = _dot(h, wg_ref[...])
    val = _dot(h, wv_ref[...])
    act = (_silu(gate) * val).astype(BF16)
    acc_ref[...] += _dot(act, wd_ref[...])

    @pl.when(j == pl.num_programs(1) - 1)
    def _():
        o_ref[...] = x_ref[...] + 0.5 * acc_ref[...]


def _ffn(x, g, w_up, w_down, tm):
    n, d = x.shape
    dff = w_down.shape[0]
    nf = 2
    tf = dff // nf
    return pl.pallas_call(
        _ffn_kernel,
        grid=(n // tm, nf),
        in_specs=[
            pl.BlockSpec((tm, d), lambda i, j: (i, 0)),
            pl.BlockSpec((1, d), lambda i, j: (0, 0)),
            pl.BlockSpec((d, tf), lambda i, j: (0, j)),
            pl.BlockSpec((d, tf), lambda i, j: (0, j + nf)),
            pl.BlockSpec((tf, d), lambda i, j: (j, 0)),
        ],
        out_specs=pl.BlockSpec((tm, d), lambda i, j: (i, 0)),
        out_shape=jax.ShapeDtypeStruct((n, d), F32),
        scratch_shapes=[pltpu.VMEM((tm, d), BF16), pltpu.VMEM((tm, d), F32)],
        compiler_params=_cparams(("parallel", "arbitrary")),
        name="ffn",
    )(x, g, w_up, w_up, w_down)


def _proj_kernel(x_ref, g_ref, w_ref, o_ref, h_ref):
    @pl.when(pl.program_id(1) == 0)
    def _():
        h_ref[...] = _rms(x_ref[...], g_ref[...]).astype(BF16)

    o_ref[...] = _dot(h_ref[...], w_ref[...])


def _proj(x, g, w, tm, tn):
    n, d = x.shape
    cols = w.shape[1]
    return pl.pallas_call(
        _proj_kernel,
        grid=(n // tm, cols // tn),
        in_specs=[
            pl.BlockSpec((tm, d), lambda i, j: (i, 0)),
            pl.BlockSpec((1, d), lambda i, j: (0, 0)),
            pl.BlockSpec((d, tn), lambda i, j: (0, j)),
        ],
        out_specs=pl.BlockSpec((tm, tn), lambda i, j: (i, j)),
        out_shape=jax.ShapeDtypeStruct((n, cols), F32),
        scratch_shapes=[pltpu.VMEM((tm, d), BF16)],
        compiler_params=_cparams(("parallel", "arbitrary")),
        name="proj",
    )(x, g, w)


def _attn_kernel(q_ref, kc_ref, vc_ref, kp_ref, vp_ref, qg_ref, kg_ref, sinks_ref, ya_ref, kn_ref,
                 *, ch, prompt):
    tq = q_ref.shape[0]
    nprev = kp_ref.shape[0]
    nch = tq // ch
    band = nprev + ch
    t = pl.program_id(1)
    qg = qg_ref[...] * SCALE
    kg = kg_ref[...]
    kc, vc, kp, vp = kc_ref[...], vc_ref[...], kp_ref[...], vp_ref[...]
    kn_parts = []
    for j in range(N_KV_HEADS):
        sl = slice(j * HEAD_DIM, (j + 1) * HEAD_DIM)
        kcj = _rms(kc[:, sl], kg)
        kn_parts.append(kcj)
        kpj = kp[:, sl]
        if prompt:
            kpj = _rms(kpj, kg)
        kall = jnp.concatenate([kpj, kcj], axis=0).astype(BF16)
        vall = jnp.concatenate([vp[:, sl], vc[:, sl]], axis=0).astype(BF16)
        sk = jnp.concatenate(
            [jnp.full((ch, 1), sinks_ref[j * GROUP + g], F32) for g in range(GROUP)], axis=0)
        for c in range(nch):
            rows = slice(c * ch, (c + 1) * ch)
            qt = q_ref[rows, j * GROUP * HEAD_DIM:(j + 1) * GROUP * HEAD_DIM]
            qcat = jnp.concatenate(
                [_rms(qt[:, g * HEAD_DIM:(g + 1) * HEAD_DIM], qg) for g in range(GROUP)],
                axis=0).astype(BF16)
            kb = kall[c * ch:c * ch + band]
            vb = vall[c * ch:c * ch + band]
            s = _dot_nt(qcat, kb)
            if prompt:
                col_chunk = lax.broadcasted_iota(jnp.int32, (1, band), 1) // ch
                valid = (t * nch + c - nprev // ch + col_chunk) >= 0
                s = jnp.where(valid, s, NEG)
            m = jnp.maximum(jnp.max(s, axis=-1, keepdims=True), sk)
            p = jnp.exp(s - m)
            den = jnp.sum(p, axis=-1, keepdims=True) + jnp.exp(sk - m)
            p = p / den
            o = _dot(p.astype(BF16), vb)
            ya_ref[rows, j * GROUP * HEAD_DIM:(j + 1) * GROUP * HEAD_DIM] = jnp.concatenate(
                [o[g * ch:(g + 1) * ch] for g in range(GROUP)], axis=1)
    kn_ref[...] = jnp.concatenate(kn_parts, axis=1)


def _attn(p_all, prev_k, prev_v, qg, kg, sinks, *, nseq, t, tq, ch, prompt):
    n = nseq * t
    nt = t // tq
    if prompt:
        def prev_map(col):
            return lambda b, i: (jnp.maximum(b * (t // WINDOW) + i * (tq // WINDOW) - 1, 0), col)
        kp_spec = pl.BlockSpec((WINDOW, _KV_W), prev_map(_C_AK))
        vp_spec = pl.BlockSpec((WINDOW, _KV_W), prev_map(_C_AV))
    else:
        kp_spec = pl.BlockSpec((WINDOW, _KV_W), lambda b, i: (b, 0))
        vp_spec = pl.BlockSpec((WINDOW, _KV_W), lambda b, i: (b, 0))
    return pl.pallas_call(
        functools.partial(_attn_kernel, ch=ch, prompt=prompt),
        grid=(nseq, nt),
        in_specs=[
            pl.BlockSpec((tq, _SLAB), lambda b, i: (b * nt + i, _C_AQ)),
            pl.BlockSpec((tq, _KV_W), lambda b, i: (b * nt + i, _C_AK)),
            pl.BlockSpec((tq, _KV_W), lambda b, i: (b * nt + i, _C_AV)),
            kp_spec, vp_spec,
            pl.BlockSpec((1, HEAD_DIM), lambda b, i: (0, 0)),
            pl.BlockSpec((1, HEAD_DIM), lambda b, i: (0, 0)),
            pl.BlockSpec(memory_space=pltpu.SMEM),
        ],
        out_specs=[
            pl.BlockSpec((tq, _SLAB), lambda b, i: (b * nt + i, 0)),
            pl.BlockSpec((tq, _KV_W), lambda b, i: (b * nt + i, 0)),
        ],
        out_shape=[jax.ShapeDtypeStruct((n, _SLAB), F32), jax.ShapeDtypeStruct((n, _KV_W), F32)],
        compiler_params=_cparams(("parallel", "arbitrary")),
        name="attn_prompt" if prompt else "attn_sample",
    )(p_all, p_all, p_all, prev_k, prev_v, qg, kg, sinks)


def _hgrn_levels(c):
    return [c >> (i + 1) for i in range(int(np.log2(c)))]


def _hgrn_kernel(hq_ref, hf_ref, hi_ref, hg_ref, lbl_ref, on_ref, s0_ref, yb_ref, sout_ref,
                 a_ref, m_ref, st_ref, *, layer):
    c = pl.program_id(1)
    cs = hq_ref.shape[0]
    levels = _hgrn_levels(cs)

    @pl.when(c == 0)
    def _():
        row = lax.broadcasted_iota(jnp.int32, (cs, cs), 0)
        col = lax.broadcasted_iota(jnp.int32, (cs, cs), 1)
        a_ref[0] = (col <= row).astype(F32).astype(BF16)
        for i, s in enumerate(levels):
            mid = (row & ~(2 * s - 1)) + s
            upper = (row & s) != 0
            up = ((col >= mid) & (col <= row)).astype(F32)
            lo = ((col > row) & (col < mid)).astype(F32)
            a_ref[i + 1] = jnp.where(upper, up, lo).astype(BF16)
            same = (row & ~(2 * s - 1)) == (col & ~(2 * s - 1))
            m_ref[i] = same.astype(F32)
        for h in range(HG_HEADS):
            st_ref[h] = s0_ref[0, h].T

    lg = lbl_ref[...]
    e = jnp.exp(lg - jnp.max(lg, axis=0, keepdims=True))
    lb = jnp.zeros((1, e.shape[1]), F32)
    for i in range(1, layer + 1):
        lb = lb + e[i:i + 1]
    lb = lb / jnp.sum(e, axis=0, keepdims=True)

    hf = hf_ref[...]
    f = lb + (1.0 - lb) * _sigmoid(hf)
    logf = jnp.maximum(jnp.log(jnp.maximum(f, 1e-26)), LOG_F_FLOOR)
    kk = (1.0 - lb) * _sigmoid(-hf)
    l1 = logf.astype(BF16)
    r1 = logf - l1.astype(F32)
    l2 = r1.astype(BF16)
    l3 = (r1 - l2.astype(F32)).astype(BF16)

    def lin(i):
        a = a_ref[i]
        return _dot(a, l1) + _dot(a, l2) + _dot(a, l3)

    bc = lin(0)
    ex = [lin(i + 1) for i in range(len(levels))]
    rowc = lax.broadcasted_iota(jnp.int32, (cs, 1), 0)
    on = on_ref[...]

    for h in range(HG_HEADS):
        sl = slice(h * HG_DK, (h + 1) * HG_DK)
        q = _silu(hq_ref[:, sl])
        v = hi_ref[:, sl]
        vb = v.astype(BF16)
        k = kk[:, sl]
        b = bc[:, sl]
        st = st_ref[h]
        o = _dot_nt((q * jnp.exp(b)).astype(BF16), st.astype(BF16))
        att = jnp.zeros((cs, cs), F32)
        for i, s in enumerate(levels):
            g = jnp.exp(ex[i][:, sl])
            upper = (rowc & s) != 0
            qs = jnp.where(upper, q * g, 0.0).astype(BF16)
            ks = jnp.where(upper, 0.0, k * g).astype(BF16)
            att = att + _dot_nt(qs, ks) * m_ref[i]
        o = o + _dot(att.astype(BF16), vb)
        o = o + jnp.sum(q * k, axis=-1, keepdims=True) * v
        blast = b[cs - 1:cs, :]
        kd = (k * jnp.exp(blast - b)).astype(BF16)
        st_new = st * jnp.exp(blast) + _dot_tn(vb, kd)
        st_ref[h] = st_new
        yb_ref[:, sl] = _rms(o, on) * _silu(hg_ref[:, sl])

    @pl.when(c == pl.num_programs(1) - 1)
    def _():
        for h in range(HG_HEADS):
            sout_ref[0, h] = st_ref[h].T


def _hgrn(p_all, lb_logits, o_norm, s0, *, nseq, t, cs, layer):
    n = nseq * t
    nc = t // cs
    nlev = len(_hgrn_levels(cs))

    def slab(col):
        return pl.BlockSpec((cs, _SLAB), lambda b, i: (b * nc + i, col))

    depth, fdim = lb_logits.shape
    st_spec = pl.BlockSpec((1, HG_HEADS, HG_DK, HG_DV), lambda b, i: (b, 0, 0, 0))
    return pl.pallas_call(
        functools.partial(_hgrn_kernel, layer=layer),
        grid=(nseq, nc),
        in_specs=[
            slab(_C_HQ), slab(_C_HF), slab(_C_HI), slab(_C_HG),
            pl.BlockSpec((depth, fdim), lambda b, i: (0, 0)),
            pl.BlockSpec((1, HG_DV), lambda b, i: (0, 0)),
            st_spec,
        ],
        out_specs=[pl.BlockSpec((cs, _SLAB), lambda b, i: (b * nc + i, 0)), st_spec],
        out_shape=[jax.ShapeDtypeStruct((n, _SLAB), F32),
                   jax.ShapeDtypeStruct((nseq, HG_HEADS, HG_DK, HG_DV), F32)],
        scratch_shapes=[
            pltpu.VMEM((nlev + 1, cs, cs), BF16),
            pltpu.VMEM((nlev, cs, cs), F32),
            pltpu.VMEM((HG_HEADS, HG_DV, HG_DK), F32),
        ],
        compiler_params=_cparams(("parallel", "arbitrary")),
        name="hgrn_t%d" % cs,
    )(p_all, p_all, p_all, p_all, lb_logits, o_norm, s0)


_PAD = 8


def _lru_kernel(lx_ref, lg_ref, cw_ref, cb_ref, wa_ref, ba_ref, wx_ref, bx_ref, lam_ref,
                cbuf_ref, h0_ref, yc_ref, nbuf_ref, hlast_ref, xe_ref, a_ref, b_ref, hc_ref):
    t = pl.program_id(1)
    tt = lx_ref.shape[0]
    w = lx_ref.shape[1]
    hist = CONV_WIDTH - 1

    @pl.when(t == 0)
    def _():
        xe_ref[_PAD - hist:_PAD, :] = cbuf_ref[0]
        hc_ref[...] = h0_ref[0]

    @pl.when(t > 0)
    def _():
        xe_ref[_PAD - hist:_PAD, :] = xe_ref[_PAD + tt - hist:_PAD + tt, :]

    xe_ref[_PAD:_PAD + tt, :] = lx_ref[...]
    xc = xe_ref[_PAD - hist:_PAD - hist + tt, :] * cw_ref[0:1, :]
    for j in range(1, CONV_WIDTH):
        xc = xc + xe_ref[_PAD - hist + j:_PAD - hist + j + tt, :] * cw_ref[j:j + 1, :]
    xc = xc + cb_ref[...]

    ra, rx = [], []
    for g in range(w // MXU_DIM):
        xg = xc[:, g * MXU_DIM:(g + 1) * MXU_DIM].astype(BF16)
        ra.append(_dot(xg, wa_ref[g]))
        rx.append(_dot(xg, wx_ref[g]))
    r = _sigmoid(jnp.concatenate(ra, axis=1) + ba_ref[...])
    ig = _sigmoid(jnp.concatenate(rx, axis=1) + bx_ref[...])
    nl = -lam_ref[...]
    softplus = jnp.maximum(nl, 0.0) + jnp.log1p(jnp.exp(-jnp.abs(nl)))
    log_a = -LRU_C * r * softplus
    a = jnp.exp(log_a)
    a_ref[...] = a
    b_ref[...] = jnp.sqrt(-jnp.tanh(log_a) * (a * a + 1.0)) * (ig * xc)

    def step(i, h):
        h = a_ref[pl.ds(i, 1), :] * h + b_ref[pl.ds(i, 1), :]
        b_ref[pl.ds(i, 1), :] = h
        return h

    h = lax.fori_loop(0, tt, step, hc_ref[...], unroll=8)
    hc_ref[...] = h
    lgv = lg_ref[...]
    gelu = 0.5 * lgv * (1.0 + jnp.tanh(np.sqrt(2.0 / np.pi) * (lgv + 0.044715 * (lgv * lgv * lgv))))
    yc_ref[...] = b_ref[...] * gelu

    @pl.when(t == pl.num_programs(1) - 1)
    def _():
        nbuf_ref[0] = xe_ref[_PAD + tt - hist:_PAD + tt, :]
        hlast_ref[0] = h


def _lru(p_all, cw, cb, wa, ba, wx, bx, lam, cbuf, h0, *, nseq, t, tt):
    n = nseq * t
    nt = t // tt
    w = _SLAB
    hist = CONV_WIDTH - 1

    def vec(rows):
        return pl.BlockSpec((rows, w), lambda b, i: (0, 0))

    wspec = pl.BlockSpec(wa.shape, lambda b, i: (0, 0, 0))
    return pl.pallas_call(
        _lru_kernel,
        grid=(nseq, nt),
        in_specs=[
            pl.BlockSpec((tt, w), lambda b, i: (b * nt + i, _C_LX)),
            pl.BlockSpec((tt, w), lambda b, i: (b * nt + i, _C_LG)),
            vec(CONV_WIDTH), vec(1), wspec, vec(1), wspec, vec(1), vec(1),
            pl.BlockSpec((1, hist, w), lambda b, i: (b, 0, 0)),
            pl.BlockSpec((1, 1, w), lambda b, i: (b, 0, 0)),
        ],
        out_specs=[
            pl.BlockSpec((tt, w), lambda b, i: (b * nt + i, 0)),
            pl.BlockSpec((1, hist, w), lambda b, i: (b, 0, 0)),
            pl.BlockSpec((1, 1, w), lambda b, i: (b, 0, 0)),
        ],
        out_shape=[jax.ShapeDtypeStruct((n, w), F32),
                   jax.ShapeDtypeStruct((nseq, hist, w), F32),
                   jax.ShapeDtypeStruct((nseq, 1, w), F32)],
        scratch_shapes=[
            pltpu.VMEM((_PAD + tt, w), F32),
            pltpu.VMEM((tt, w), F32),
            pltpu.VMEM((tt, w), F32),
            pltpu.VMEM((1, w), F32),
        ],
        compiler_params=_cparams(("parallel", "arbitrary")),
        name="lru_t%d" % tt,
    )(p_all, p_all, cw, cb, wa, ba, wx, bx, lam, cbuf, h0)


def _merge_kernel(x_ref, ya_ref, yb_ref, yc_ref, ga_ref, gb_ref, gc_ref,
                  wa_ref, wb_ref, wc_ref, wo_ref, o_ref):
    m = _sigmoid(ga_ref[...]) * _dot(ya_ref[...].astype(BF16), wa_ref[...])
    m = m + _sigmoid(gb_ref[...]) * _dot(yb_ref[...].astype(BF16), wb_ref[...])
    m = m + _sigmoid(gc_ref[...]) * _dot(yc_ref[...].astype(BF16), wc_ref[...])
    o_ref[...] = x_ref[...] + _dot(m.astype(BF16), wo_ref[...])


def _merge(x, ya, yb, yc, p_all, wa, wb, wc, wo, tm):
    n, d = x.shape
    row = pl.BlockSpec((tm, d), lambda i: (i, 0))

    def slab(col):
        return pl.BlockSpec((tm, _SLAB), lambda i: (i, col))

    wspec = pl.BlockSpec((d, d), lambda i: (0, 0))
    return pl.pallas_call(
        _merge_kernel,
        grid=(n // tm,),
        in_specs=[row, row, row, row, slab(_C_GA), slab(_C_GB), slab(_C_GC),
                  wspec, wspec, wspec, wspec],
        out_specs=row,
        out_shape=jax.ShapeDtypeStruct((n, d), F32),
        compiler_params=_cparams(("parallel",)),
        name="merge",
    )(x, ya, yb, yc, p_all, p_all, p_all, wa, wb, wc, wo)


def _permute_w_in(w_in):
    aq, ak, av, rest = jnp.split(w_in, [N_HEADS * HEAD_DIM, N_HEADS * HEAD_DIM + _KV_W,
                                        N_HEADS * HEAD_DIM + 2 * _KV_W], axis=-1)
    return jnp.concatenate([aq, rest, ak, av], axis=-1).astype(BF16)


def _block_diag(w):
    nb, bw, _ = w.shape
    per = MXU_DIM // bw
    wg = w.reshape(nb // per, per, bw, bw)
    eye = jnp.eye(per, dtype=w.dtype)
    out = jnp.einsum('gpij,pq->gpiqj', wg, eye)
    return out.reshape(nb // per, MXU_DIM, MXU_DIM).astype(BF16)


def _tiles(n, prefer):
    for tm in prefer:
        if n % tm == 0:
            return tm
    return n


def _layer(x, lp, cache, *, nseq, t, layer, prompt):
    n, d = x.shape
    tm = _tiles(n, (512, 256, 128))
    x = _ffn(x, lp['norm_ffn1'], lp['w_ffn1_up'], lp['w_ffn1_down'], tm)
    p_all = _proj(x, lp['norm_mix'], lp['w_in'], _tiles(n, (1024, 512, 256, 128)), 1536)
    ck, cv, s0, cbuf, h0 = cache
    if prompt:
        ya, kn = _attn(p_all, p_all, p_all, lp['q_norm'], lp['k_norm'], lp['attn_sinks'],
                       nseq=nseq, t=t, tq=_tiles(t, (256, 128)), ch=CHUNK, prompt=True)
        cs = _tiles(t, (128, 64, 32, 16))
        tt = _tiles(t, (512, 256, 128, 64, 32, 16))
    else:
        ya, kn = _attn(p_all, ck, cv, lp['q_norm'], lp['k_norm'], lp['attn_sinks'],
                       nseq=nseq, t=t, tq=t, ch=t, prompt=False)
        cs = t
        tt = t
    yb, s_new = _hgrn(p_all, lp['hgrn_lb_logits'], lp['hgrn_o_norm'], s0,
                      nseq=nseq, t=t, cs=cs, layer=layer)
    yc, nbuf, hlast = _lru(p_all, lp['conv_w'], lp['conv_b'], lp['lru_w_a'], lp['lru_b_a'],
                           lp['lru_w_x'], lp['lru_b_x'], lp['lru_lambda'], cbuf, h0,
                           nseq=nseq, t=t, tt=tt)
    x = _merge(x, ya, yb, yc, p_all, lp['w_attn_o'], lp['w_hgrn_o'], lp['w_lru_o'], lp['w_out'],
               _tiles(n, (256, 128)))
    x = _ffn(x, lp['norm_ffn2'], lp['w_ffn2_up'], lp['w_ffn2_down'], tm)
    v_rows = p_all[:, _C_AV * _KV_W:(_C_AV + 1) * _KV_W]
    return x, (kn, v_rows, s_new, nbuf, hlast.reshape(nseq, d))


def _forward(x_prompt, x_sample, cache_attn_k, cache_attn_v, state_hgrn, state_conv, state_lru, w):
    bsz, seq, d = x_prompt.shape
    dbsz, dseq, _ = x_sample.shape
    depth = w['w_in'].shape[0]
    win_rows = cache_attn_k.shape[2]
    xp = x_prompt.reshape(bsz * seq, d)
    xs = x_sample.reshape(dbsz * dseq, d)
    zero_cache = (None, None,
                  jnp.zeros((bsz, HG_HEADS, HG_DK, HG_DV), F32),
                  jnp.zeros((bsz, CONV_WIDTH - 1, d), F32),
                  jnp.zeros((bsz, 1, d), F32))
    st_p, st_s = [], []
    for l in range(depth):
        lp = {
            'norm_ffn1': w['norm_ffn1'][l][None], 'norm_mix': w['norm_mix'][l][None],
            'norm_ffn2': w['norm_ffn2'][l][None],
            'w_ffn1_up': w['w_ffn1_up'][l].astype(BF16), 'w_ffn1_down': w['w_ffn1_down'][l].astype(BF16),
            'w_ffn2_up': w['w_ffn2_up'][l].astype(BF16), 'w_ffn2_down': w['w_ffn2_down'][l].astype(BF16),
            'w_in': _permute_w_in(w['w_in'][l]),
            'q_norm': w['q_norm'][l][None], 'k_norm': w['k_norm'][l][None],
            'attn_sinks': w['attn_sinks'][l],
            'w_attn_o': w['w_attn_o'][l].astype(BF16), 'w_hgrn_o': w['w_hgrn_o'][l].astype(BF16),
            'w_lru_o': w['w_lru_o'][l].astype(BF16), 'w_out': w['w_out'][l].astype(BF16),
            'hgrn_lb_logits': w['hgrn_lb_logits'], 'hgrn_o_norm': w['hgrn_o_norm'][l][None],
            'conv_w': w['conv_w'][l], 'conv_b': w['conv_b'][l][None],
            'lru_w_a': _block_diag(w['lru_w_a'][l]), 'lru_b_a': w['lru_b_a'][l][None],
            'lru_w_x': _block_diag(w['lru_w_x'][l]), 'lru_b_x': w['lru_b_x'][l][None],
            'lru_lambda': w['lru_lambda'][l][None],
        }
        xp, sp = _layer(xp, lp, zero_cache, nseq=bsz, t=seq, layer=l, prompt=True)
        cache = (cache_attn_k[l].reshape(dbsz * win_rows, _KV_W),
                 cache_attn_v[l].reshape(dbsz * win_rows, _KV_W),
                 state_hgrn[l], state_conv[l], state_lru[l][:, None, :])
        xs, ss = _layer(xs, lp, cache, nseq=dbsz, t=dseq, layer=l, prompt=False)
        kn, vr, s_new, nbuf, hl = sp
        kn = kn.reshape(bsz, seq, N_KV_HEADS, HEAD_DIM)[:, seq - win_rows:]
        vr = vr.reshape(bsz, seq, N_KV_HEADS, HEAD_DIM)[:, seq - win_rows:]
        st_p.append((kn, vr, s_new, nbuf, hl))
        kn, vr, s_new, nbuf, hl = ss
        st_s.append((kn.reshape(dbsz, dseq, N_KV_HEADS, HEAD_DIM),
                     vr.reshape(dbsz, dseq, N_KV_HEADS, HEAD_DIM), s_new, nbuf, hl))

    def stack(sts, i):
        return jnp.stack([s[i] for s in sts], axis=0)

    return (xp.reshape(bsz, seq, d), xs.reshape(dbsz, dseq, d),
            stack(st_p, 0), stack(st_p, 1), stack(st_p, 2), stack(st_p, 3), stack(st_p, 4),
            stack(st_s, 0), stack(st_s, 1), stack(st_s, 2), stack(st_s, 3), stack(st_s, 4))


def kernel(x_prompt, x_sample, cache_attn_k, cache_attn_v, state_hgrn, state_conv, state_lru,
           norm_ffn1, w_ffn1_up, w_ffn1_down, norm_mix, w_in, q_norm, k_norm, attn_sinks, w_attn_o,
           hgrn_lb_logits, hgrn_o_norm, w_hgrn_o, conv_w, conv_b, lru_w_a, lru_b_a, lru_w_x, lru_b_x,
           lru_lambda, w_lru_o, w_out, norm_ffn2, w_ffn2_up, w_ffn2_down):
    w = dict(norm_ffn1=norm_ffn1, w_ffn1_up=w_ffn1_up, w_ffn1_down=w_ffn1_down, norm_mix=norm_mix,
             w_in=w_in, q_norm=q_norm, k_norm=k_norm, attn_sinks=attn_sinks, w_attn_o=w_attn_o,
             hgrn_lb_logits=hgrn_lb_logits, hgrn_o_norm=hgrn_o_norm, w_hgrn_o=w_hgrn_o,
             conv_w=conv_w, conv_b=conv_b, lru_w_a=lru_w_a, lru_b_a=lru_b_a, lru_w_x=lru_w_x,
             lru_b_x=lru_b_x, lru_lambda=lru_lambda, w_lru_o=w_lru_o, w_out=w_out,
             norm_ffn2=norm_ffn2, w_ffn2_up=w_ffn2_up, w_ffn2_down=w_ffn2_down)
    return _forward(x_prompt, x_sample, cache_attn_k, cache_attn_v, state_hgrn, state_conv,
                    state_lru, w)
```

```python
import functools

import numpy as np
import jax
import jax.numpy as jnp
from jax import lax
from jax.experimental import pallas as pl
from jax.experimental.pallas import tpu as pltpu

F32 = jnp.float32
BF16 = jnp.bfloat16

CHUNK = 64
N_HEADS = 16
N_KV_HEADS = 4
HEAD_DIM = 64
GROUP = N_HEADS // N_KV_HEADS
WINDOW = 128
SCALE = HEAD_DIM ** -0.5
HG_HEADS = 8
HG_DK = 128
HG_DV = 128
LOG_F_FLOOR = -60.0
CONV_WIDTH = 4
LRU_C = 8.0
LRU_BW = 64
EPS = 1e-6
NEG = -1e30

MXU_DIM = 256
VMEM_LIMIT_BYTES = 56 * 1024 * 1024

_SLAB = 1024
(_C_HQ, _C_HF, _C_HI, _C_HG, _C_LX, _C_LG, _C_GA, _C_GB, _C_GC, _C_AQ) = range(10)
_N_MIX_COLS = 9 * _SLAB
_KV_W = N_KV_HEADS * HEAD_DIM
_C_AK = 10 * _SLAB // _KV_W
_C_AV = _C_AK + 1
_QKV_ROWS = N_HEADS * HEAD_DIM + 2 * _KV_W


def _cparams(sem):
    return pltpu.CompilerParams(dimension_semantics=sem, vmem_limit_bytes=VMEM_LIMIT_BYTES)


def _rms(x, g):
    return x * lax.rsqrt(jnp.mean(x * x, axis=-1, keepdims=True) + EPS) * g


def _sigmoid(x):
    return 1.0 / (1.0 + jnp.exp(-x))


def _silu(x):
    return x * _sigmoid(x)


def _dot(a, b):
    return jnp.dot(a, b, preferred_element_type=F32)


def _dot_nt(a, b):
    return lax.dot_general(a, b, (((1,), (1,)), ((), ())), preferred_element_type=F32)


def _dot_tn(a, b):
    return lax.dot_general(a, b, (((0,), (0,)), ((), ())), preferred_element_type=F32)


def _ffn_kernel(x_ref, g_ref, wg_ref, wv_ref, wd_ref, o_ref, h_ref, acc_ref):
    j = pl.program_id(1)

    @pl.when(j == 0)
    def _():
        h_ref[...] = _rms(x_ref[...], g_ref[...]).astype(BF16)
        acc_ref[...] = jnp.zeros_like(acc_ref)

    h = h_ref[...]
    gate = _dot(h, wg_ref[...])
    val = _dot(h, wv_ref[...])
    act = (_silu(gate) * val).astype(BF16)
    acc_ref[...] += _dot(act, wd_ref[...])

    @pl.when(j == pl.num_programs(1) - 1)
    def _():
        o_ref[...] = x_ref[...] + 0.5 * acc_ref[...]


def _ffn(x, g, w_up, w_down, tm):
    n, d = x.shape
    dff = w_down.shape[0]
    nf = 2
    tf = dff // nf
    return pl.pallas_call(
        _ffn_kernel,
        grid=(n // tm, nf),
        in_specs=[
            pl.BlockSpec((tm, d), lambda i, j: (i, 0)),
            pl.BlockSpec((1, d), lambda i, j: (0, 0)),
            pl.BlockSpec((d, tf), lambda i, j: (0, j)),
            pl.BlockSpec((d, tf), lambda i, j: (0, j + nf)),
            pl.BlockSpec((tf, d), lambda i, j: (j, 0)),
        ],
        out_specs=pl.BlockSpec((tm, d), lambda i, j: (i, 0)),
        out_shape=jax.ShapeDtypeStruct((n, d), F32),
        scratch_shapes=[pltpu.VMEM((tm, d), BF16), pltpu.VMEM((tm, d), F32)],
        compiler_params=_cparams(("parallel", "arbitrary")),
        name="ffn",
    )(x, g, w_up, w_up, w_down)


def _proj_kernel(x_ref, g_ref, w_ref, o_ref, h_ref):
    @pl.when(pl.program_id(1) == 0)
    def _():
        h_ref[...] = _rms(x_ref[...], g_ref[...]).astype(BF16)

    o_ref[...] = _dot(h_ref[...], w_ref[...])


def _proj(x, g, w, tm, tn, cols):
    n, d = x.shape
    return pl.pallas_call(
        _proj_kernel,
        grid=(n // tm, cols // tn),
        in_specs=[
            pl.BlockSpec((tm, d), lambda i, j: (i, 0)),
            pl.BlockSpec((1, d), lambda i, j: (0, 0)),
            pl.BlockSpec((d, tn), lambda i, j: (0, j)),
        ],
        out_specs=pl.BlockSpec((tm, tn), lambda i, j: (i, j)),
        out_shape=jax.ShapeDtypeStruct((n, cols), F32),
        scratch_shapes=[pltpu.VMEM((tm, d), BF16)],
        compiler_params=_cparams(("parallel", "arbitrary")),
        name="proj",
    )(x, g, w)


def _attn_sample_kernel(q_ref, kc_ref, vc_ref, kp_ref, vp_ref, qg_ref, kg_ref, sinks_ref,
                        ya_ref, kn_ref):
    tq = q_ref.shape[0]
    qg = qg_ref[...] * SCALE
    kg = kg_ref[...]
    kc, vc, kp, vp = kc_ref[...], vc_ref[...], kp_ref[...], vp_ref[...]
    kn_parts = []
    for j in range(N_KV_HEADS):
        sl = slice(j * HEAD_DIM, (j + 1) * HEAD_DIM)
        kcj = _rms(kc[:, sl], kg)
        kn_parts.append(kcj)
        kb = jnp.concatenate([kp[:, sl], kcj], axis=0).astype(BF16)
        vb = jnp.concatenate([vp[:, sl], vc[:, sl]], axis=0).astype(BF16)
        sk = jnp.concatenate(
            [jnp.full((tq, 1), sinks_ref[j * GROUP + g], F32) for g in range(GROUP)], axis=0)
        qt = q_ref[:, j * GROUP * HEAD_DIM:(j + 1) * GROUP * HEAD_DIM]
        qcat = jnp.concatenate(
            [_rms(qt[:, g * HEAD_DIM:(g + 1) * HEAD_DIM], qg) for g in range(GROUP)],
            axis=0).astype(BF16)
        s = _dot_nt(qcat, kb)
        m = jnp.maximum(jnp.max(s, axis=-1, keepdims=True), sk)
        p = jnp.exp(s - m)
        den = jnp.sum(p, axis=-1, keepdims=True) + jnp.exp(sk - m)
        p = p / den
        o = _dot(p.astype(BF16), vb)
        ya_ref[:, j * GROUP * HEAD_DIM:(j + 1) * GROUP * HEAD_DIM] = jnp.concatenate(
            [o[g * tq:(g + 1) * tq] for g in range(GROUP)], axis=1)
    kn_ref[...] = jnp.concatenate(kn_parts, axis=1)


def _attn_sample(p_all, cache_k, cache_v, qg, kg, sinks, *, nseq, t):
    n = nseq * t
    prev = pl.BlockSpec((WINDOW, _KV_W), lambda b: (b, 0))
    return pl.pallas_call(
        _attn_sample_kernel,
        grid=(nseq,),
        in_specs=[
            pl.BlockSpec((t, _SLAB), lambda b: (b, _C_AQ)),
            pl.BlockSpec((t, _KV_W), lambda b: (b, _C_AK)),
            pl.BlockSpec((t, _KV_W), lambda b: (b, _C_AV)),
            prev, prev,
            pl.BlockSpec((1, HEAD_DIM), lambda b: (0, 0)),
            pl.BlockSpec((1, HEAD_DIM), lambda b: (0, 0)),
            pl.BlockSpec(memory_space=pltpu.SMEM),
        ],
        out_specs=[
            pl.BlockSpec((t, _SLAB), lambda b: (b, 0)),
            pl.BlockSpec((t, _KV_W), lambda b: (b, 0)),
        ],
        out_shape=[jax.ShapeDtypeStruct((n, _SLAB), F32), jax.ShapeDtypeStruct((n, _KV_W), F32)],
        compiler_params=_cparams(("parallel",)),
        name="attn_sample",
    )(p_all, p_all, p_all, cache_k, cache_v, qg, kg, sinks)


def _qkv_t_kernel(x_ref, g_ref, w_ref, gc_ref, o_ref):
    h = _rms(x_ref[...], g_ref[...]).astype(BF16)
    acc = _dot_nt(w_ref[...], h)
    nqk = gc_ref.shape[0]
    for r in range(0, nqk, HEAD_DIM):
        blk = acc[r:r + HEAD_DIM]
        ms = jnp.sum(blk * blk, axis=0, keepdims=True) * (1.0 / HEAD_DIM)
        o_ref[r:r + HEAD_DIM, :] = blk * lax.rsqrt(ms + EPS) * gc_ref[r:r + HEAD_DIM, :]
    o_ref[nqk:, :] = acc[nqk:]


def _qkv_t(x, g, w_t, gain_col, tm):
    n, d = x.shape
    rows = w_t.shape[0]
    return pl.pallas_call(
        _qkv_t_kernel,
        grid=(n // tm,),
        in_specs=[
            pl.BlockSpec((tm, d), lambda i: (i, 0)),
            pl.BlockSpec((1, d), lambda i: (0, 0)),
            pl.BlockSpec((rows, d), lambda i: (0, 0)),
            pl.BlockSpec(gain_col.shape, lambda i: (0, 0)),
        ],
        out_specs=pl.BlockSpec((rows, tm), lambda i: (0, i)),
        out_shape=jax.ShapeDtypeStruct((rows, n), F32),
        compiler_params=_cparams(("parallel",)),
        name="qkv_t",
    )(x, g, w_t, gain_col)


_PAIR = 2 * CHUNK
_KEYS = _PAIR + WINDOW


def _attn_prompt_kernel(q_ref, kc_ref, vc_ref, kp_ref, vp_ref, sinks_ref, ya_ref, yat_ref):
    tq = q_ref.shape[1]
    t = pl.program_id(1)
    lane = lax.broadcasted_iota(jnp.int32, (1, GROUP * _PAIR), 1)
    first_chunk = (lane % _PAIR) < CHUNK
    k_cur = kc_ref[...].T.astype(BF16)
    k_prev = kp_ref[...].T.astype(BF16)
    for p in range(tq // _PAIR):
        lo = p * _PAIR
        if p == 0:
            kb = jnp.concatenate([k_prev, k_cur[:_PAIR]], axis=0)
        else:
            kb = k_cur[lo - WINDOW:lo + _PAIR]
        first_key_chunk = t * (tq // CHUNK) + (lo - WINDOW) // CHUNK
        for j in range(N_KV_HEADS):
            rows = slice(j * HEAD_DIM, (j + 1) * HEAD_DIM)
            if p == 0:
                vb = jnp.concatenate([vp_ref[rows, :], vc_ref[rows, :_PAIR]], axis=1)
            else:
                vb = vc_ref[rows, lo - WINDOW:lo + _PAIR]
            qcat = jnp.concatenate(
                [q_ref[(j * GROUP + g) * HEAD_DIM:(j * GROUP + g + 1) * HEAD_DIM, lo:lo + _PAIR]
                 for g in range(GROUP)], axis=1).astype(BF16)
            zero = jnp.zeros_like(qcat)
            qz = jnp.concatenate([qcat if i == j else zero for i in range(N_KV_HEADS)], axis=0)
            s = _dot(kb, qz)
            s = jnp.concatenate([
                jnp.where(first_chunk & (first_key_chunk >= 0), s[:CHUNK], NEG),
                jnp.where(first_key_chunk + 1 >= 0, s[CHUNK:2 * CHUNK], NEG),
                s[2 * CHUNK:3 * CHUNK],
                jnp.where(first_chunk, NEG, s[3 * CHUNK:]),
            ], axis=0)
            sk = jnp.concatenate(
                [jnp.full((1, _PAIR), sinks_ref[j * GROUP + g], F32) for g in range(GROUP)], axis=1)
            m = jnp.maximum(jnp.max(s, axis=0, keepdims=True), sk)
            e = jnp.exp(s - m)
            den = jnp.sum(e, axis=0, keepdims=True) + jnp.exp(sk - m)
            pn = (e * (1.0 / den)).astype(BF16)
            o = _dot(vb.astype(BF16), pn)
            for g in range(GROUP):
                yat_ref[(j * GROUP + g) * HEAD_DIM:(j * GROUP + g + 1) * HEAD_DIM, lo:lo + _PAIR] = (
                    o[:, g * _PAIR:(g + 1) * _PAIR])
    ya_ref[...] = yat_ref[...].T


def _attn_prompt(qkv_t, sinks, *, nseq, t, tq):
    n = nseq * t
    nt = t // tq
    nq = N_HEADS * HEAD_DIM
    kblk, vblk = nq // _KV_W, nq // _KV_W + 1

    def prev_map(blk):
        return lambda b, i: (blk, jnp.maximum(b * (t // WINDOW) + i * (tq // WINDOW) - 1, 0))

    return pl.pallas_call(
        _attn_prompt_kernel,
        grid=(nseq, nt),
        in_specs=[
            pl.BlockSpec((nq, tq), lambda b, i: (0, b * nt + i)),
            pl.BlockSpec((_KV_W, tq), lambda b, i: (kblk, b * nt + i)),
            pl.BlockSpec((_KV_W, tq), lambda b, i: (vblk, b * nt + i)),
            pl.BlockSpec((_KV_W, WINDOW), prev_map(kblk)),
            pl.BlockSpec((_KV_W, WINDOW), prev_map(vblk)),
            pl.BlockSpec(memory_space=pltpu.SMEM),
        ],
        out_specs=pl.BlockSpec((tq, nq), lambda b, i: (b * nt + i, 0)),
        out_shape=jax.ShapeDtypeStruct((n, nq), F32),
        scratch_shapes=[pltpu.VMEM((nq, tq), F32)],
        compiler_params=_cparams(("parallel", "arbitrary")),
        name="attn_prompt",
    )(qkv_t, qkv_t, qkv_t, qkv_t, qkv_t, sinks)


def _hgrn_levels(c):
    return [c >> (i + 1) for i in range(int(np.log2(c)))]


def _hgrn_kernel(hq_ref, hf_ref, hi_ref, hg_ref, lbl_ref, on_ref, s0_ref, yb_ref, sout_ref,
                 a_ref, m_ref, st_ref, *, layer):
    c = pl.program_id(1)
    cs = hq_ref.shape[0]
    levels = _hgrn_levels(cs)

    @pl.when(c == 0)
    def _():
        row = lax.broadcasted_iota(jnp.int32, (cs, cs), 0)
        col = lax.broadcasted_iota(jnp.int32, (cs, cs), 1)
        a_ref[0] = (col <= row).astype(F32).astype(BF16)
        for i, s in enumerate(levels):
            mid = (row & ~(2 * s - 1)) + s
            upper = (row & s) != 0
            up = ((col >= mid) & (col <= row)).astype(F32)
            lo = ((col > row) & (col < mid)).astype(F32)
            a_ref[i + 1] = jnp.where(upper, up, lo).astype(BF16)
            same = (row & ~(2 * s - 1)) == (col & ~(2 * s - 1))
            m_ref[i] = same.astype(F32)
        for h in range(HG_HEADS):
            st_ref[h] = s0_ref[0, h].T

    lg = lbl_ref[...]
    e = jnp.exp(lg - jnp.max(lg, axis=0, keepdims=True))
    lb = jnp.zeros((1, e.shape[1]), F32)
    for i in range(1, layer + 1):
        lb = lb + e[i:i + 1]
    lb = lb / jnp.sum(e, axis=0, keepdims=True)

    hf = hf_ref[...]
    f = lb + (1.0 - lb) * _sigmoid(hf)
    logf = jnp.maximum(jnp.log(jnp.maximum(f, 1e-26)), LOG_F_FLOOR)
    kk = (1.0 - lb) * _sigmoid(-hf)
    l1 = logf.astype(BF16)
    r1 = logf - l1.astype(F32)
    l2 = r1.astype(BF16)
    l3 = (r1 - l2.astype(F32)).astype(BF16)

    def lin(i):
        a = a_ref[i]
        return _dot(a, l1) + _dot(a, l2) + _dot(a, l3)

    bc = lin(0)
    ex = [lin(i + 1) for i in range(len(levels))]
    rowc = lax.broadcasted_iota(jnp.int32, (cs, 1), 0)
    on = on_ref[...]

    for h in range(HG_HEADS):
        sl = slice(h * HG_DK, (h + 1) * HG_DK)
        q = _silu(hq_ref[:, sl])
        v = hi_ref[:, sl]
        vb = v.astype(BF16)
        k = kk[:, sl]
        b = bc[:, sl]
        st = st_ref[h]
        o = _dot_nt((q * jnp.exp(b)).astype(BF16), st.astype(BF16))
        att = jnp.zeros((cs, cs), F32)
        for i, s in enumerate(levels):
            g = jnp.exp(ex[i][:, sl])
            upper = (rowc & s) != 0
            qs = jnp.where(upper, q * g, 0.0).astype(BF16)
            ks = jnp.where(upper, 0.0, k * g).astype(BF16)
            att = att + _dot_nt(qs, ks) * m_ref[i]
        o = o + _dot(att.astype(BF16), vb)
        o = o + jnp.sum(q * k, axis=-1, keepdims=True) * v
        blast = b[cs - 1:cs, :]
        kd = (k * jnp.exp(blast - b)).astype(BF16)
        st_new = st * jnp.exp(blast) + _dot_tn(vb, kd)
        st_ref[h] = st_new
        yb_ref[:, sl] = _rms(o, on) * _silu(hg_ref[:, sl])

    @pl.when(c == pl.num_programs(1) - 1)
    def _():
        for h in range(HG_HEADS):
            sout_ref[0, h] = st_ref[h].T


def _hgrn(p_all, lb_logits, o_norm, s0, *, nseq, t, cs, layer):
    n = nseq * t
    nc = t // cs
    nlev = len(_hgrn_levels(cs))

    def slab(col):
        return pl.BlockSpec((cs, _SLAB), lambda b, i: (b * nc + i, col))

    depth, fdim = lb_logits.shape
    st_spec = pl.BlockSpec((1, HG_HEADS, HG_DK, HG_DV), lambda b, i: (b, 0, 0, 0))
    return pl.pallas_call(
        functools.partial(_hgrn_kernel, layer=layer),
        grid=(nseq, nc),
        in_specs=[
            slab(_C_HQ), slab(_C_HF), slab(_C_HI), slab(_C_HG),
            pl.BlockSpec((depth, fdim), lambda b, i: (0, 0)),
            pl.BlockSpec((1, HG_DV), lambda b, i: (0, 0)),
            st_spec,
        ],
        out_specs=[pl.BlockSpec((cs, _SLAB), lambda b, i: (b * nc + i, 0)), st_spec],
        out_shape=[jax.ShapeDtypeStruct((n, _SLAB), F32),
                   jax.ShapeDtypeStruct((nseq, HG_HEADS, HG_DK, HG_DV), F32)],
        scratch_shapes=[
            pltpu.VMEM((nlev + 1, cs, cs), BF16),
            pltpu.VMEM((nlev, cs, cs), F32),
            pltpu.VMEM((HG_HEADS, HG_DV, HG_DK), F32),
        ],
        compiler_params=_cparams(("parallel", "arbitrary")),
        name="hgrn_t%d" % cs,
    )(p_all, p_all, p_all, p_all, lb_logits, o_norm, s0)


_PAD = 8


def _lru_kernel(lx_ref, lg_ref, cw_ref, cb_ref, wa_ref, ba_ref, wx_ref, bx_ref, lam_ref,
                cbuf_ref, h0_ref, yc_ref, nbuf_ref, hlast_ref, xe_ref, a_ref, b_ref, hc_ref):
    t = pl.program_id(1)
    tt = lx_ref.shape[0]
    w = lx_ref.shape[1]
    hist = CONV_WIDTH - 1

    @pl.when(t == 0)
    def _():
        xe_ref[_PAD - hist:_PAD, :] = cbuf_ref[0]
        hc_ref[...] = h0_ref[0]

    @pl.when(t > 0)
    def _():
        xe_ref[_PAD - hist:_PAD, :] = xe_ref[_PAD + tt - hist:_PAD + tt, :]

    xe_ref[_PAD:_PAD + tt, :] = lx_ref[...]
    xc = xe_ref[_PAD - hist:_PAD - hist + tt, :] * cw_ref[0:1, :]
    for j in range(1, CONV_WIDTH):
        xc = xc + xe_ref[_PAD - hist + j:_PAD - hist + j + tt, :] * cw_ref[j:j + 1, :]
    xc = xc + cb_ref[...]

    ra, rx = [], []
    for g in range(w // MXU_DIM):
        xg = xc[:, g * MXU_DIM:(g + 1) * MXU_DIM].astype(BF16)
        ra.append(_dot(xg, wa_ref[g]))
        rx.append(_dot(xg, wx_ref[g]))
    r = _sigmoid(jnp.concatenate(ra, axis=1) + ba_ref[...])
    ig = _sigmoid(jnp.concatenate(rx, axis=1) + bx_ref[...])
    nl = -lam_ref[...]
    softplus = jnp.maximum(nl, 0.0) + jnp.log1p(jnp.exp(-jnp.abs(nl)))
    log_a = -LRU_C * r * softplus
    a = jnp.exp(log_a)
    a_ref[...] = a
    b_ref[...] = jnp.sqrt(-jnp.tanh(log_a) * (a * a + 1.0)) * (ig * xc)

    def step(i, h):
        h = a_ref[pl.ds(i, 1), :] * h + b_ref[pl.ds(i, 1), :]
        b_ref[pl.ds(i, 1), :] = h
        return h

    h = lax.fori_loop(0, tt, step, hc_ref[...], unroll=8)
    hc_ref[...] = h
    lgv = lg_ref[...]
    gelu = 0.5 * lgv * (1.0 + jnp.tanh(np.sqrt(2.0 / np.pi) * (lgv + 0.044715 * (lgv * lgv * lgv))))
    yc_ref[...] = b_ref[...] * gelu

    @pl.when(t == pl.num_programs(1) - 1)
    def _():
        nbuf_ref[0] = xe_ref[_PAD + tt - hist:_PAD + tt, :]
        hlast_ref[0] = h


def _lru(p_all, cw, cb, wa, ba, wx, bx, lam, cbuf, h0, *, nseq, t, tt):
    n = nseq * t
    nt = t // tt
    w = _SLAB
    hist = CONV_WIDTH - 1

    def vec(rows):
        return pl.BlockSpec((rows, w), lambda b, i: (0, 0))

    wspec = pl.BlockSpec(wa.shape, lambda b, i: (0, 0, 0))
    return pl.pallas_call(
        _lru_kernel,
        grid=(nseq, nt),
        in_specs=[
            pl.BlockSpec((tt, w), lambda b, i: (b * nt + i, _C_LX)),
            pl.BlockSpec((tt, w), lambda b, i: (b * nt + i, _C_LG)),
            vec(CONV_WIDTH), vec(1), wspec, vec(1), wspec, vec(1), vec(1),
            pl.BlockSpec((1, hist, w), lambda b, i: (b, 0, 0)),
            pl.BlockSpec((1, 1, w), lambda b, i: (b, 0, 0)),
        ],
        out_specs=[
            pl.BlockSpec((tt, w), lambda b, i: (b * nt + i, 0)),
            pl.BlockSpec((1, hist, w), lambda b, i: (b, 0, 0)),
            pl.BlockSpec((1, 1, w), lambda b, i: (b, 0, 0)),
        ],
        out_shape=[jax.ShapeDtypeStruct((n, w), F32),
                   jax.ShapeDtypeStruct((nseq, hist, w), F32),
                   jax.ShapeDtypeStruct((nseq, 1, w), F32)],
        scratch_shapes=[
            pltpu.VMEM((_PAD + tt, w), F32),
            pltpu.VMEM((tt, w), F32),
            pltpu.VMEM((tt, w), F32),
            pltpu.VMEM((1, w), F32),
        ],
        compiler_params=_cparams(("parallel", "arbitrary")),
        name="lru_t%d" % tt,
    )(p_all, p_all, cw, cb, wa, ba, wx, bx, lam, cbuf, h0)


def _merge_kernel(x_ref, ya_ref, yb_ref, yc_ref, ga_ref, gb_ref, gc_ref,
                  wa_ref, wb_ref, wc_ref, wo_ref, o_ref):
    m = _sigmoid(ga_ref[...]) * _dot(ya_ref[...].astype(BF16), wa_ref[...])
    m = m + _sigmoid(gb_ref[...]) * _dot(yb_ref[...].astype(BF16), wb_ref[...])
    m = m + _sigmoid(gc_ref[...]) * _dot(yc_ref[...].astype(BF16), wc_ref[...])
    o_ref[...] = x_ref[...] + _dot(m.astype(BF16), wo_ref[...])


def _merge(x, ya, yb, yc, p_all, wa, wb, wc, wo, tm):
    n, d = x.shape
    row = pl.BlockSpec((tm, d), lambda i: (i, 0))

    def slab(col):
        return pl.BlockSpec((tm, _SLAB), lambda i: (i, col))

    wspec = pl.BlockSpec((d, d), lambda i: (0, 0))
    return pl.pallas_call(
        _merge_kernel,
        grid=(n // tm,),
        in_specs=[row, row, row, row, slab(_C_GA), slab(_C_GB), slab(_C_GC),
                  wspec, wspec, wspec, wspec],
        out_specs=row,
        out_shape=jax.ShapeDtypeStruct((n, d), F32),
        compiler_params=_cparams(("parallel",)),
        name="merge",
    )(x, ya, yb, yc, p_all, p_all, p_all, wa, wb, wc, wo)


def _permute_w_in(w_in):
    aq, ak, av, rest = jnp.split(w_in, [N_HEADS * HEAD_DIM, N_HEADS * HEAD_DIM + _KV_W,
                                        N_HEADS * HEAD_DIM + 2 * _KV_W], axis=-1)
    return jnp.concatenate([rest, aq, ak, av], axis=-1).astype(BF16)


def _qkv_weight_t(w_in):
    return w_in[:, :_QKV_ROWS].T.astype(BF16)


def _qk_gain_col(q_norm, k_norm):
    return jnp.concatenate([jnp.tile(q_norm * SCALE, N_HEADS), jnp.tile(k_norm, N_KV_HEADS)])[:, None]


def _block_diag(w):
    nb, bw, _ = w.shape
    per = MXU_DIM // bw
    wg = w.reshape(nb // per, per, bw, bw)
    eye = jnp.eye(per, dtype=w.dtype)
    out = jnp.einsum('gpij,pq->gpiqj', wg, eye)
    return out.reshape(nb // per, MXU_DIM, MXU_DIM).astype(BF16)


def _tiles(n, prefer):
    for tm in prefer:
        if n % tm == 0:
            return tm
    return n


def _layer(x, lp, cache, *, nseq, t, layer, prompt):
    n, d = x.shape
    tm = _tiles(n, (512, 256, 128))
    x = _ffn(x, lp['norm_ffn1'], lp['w_ffn1_up'], lp['w_ffn1_down'], tm)
    ck, cv, s0, cbuf, h0 = cache
    tm_proj = _tiles(n, (1024, 512, 256, 128))
    if prompt:
        p_all = _proj(x, lp['norm_mix'], lp['w_in'], tm_proj, 1536, _N_MIX_COLS)
        qkv_t = _qkv_t(x, lp['norm_mix'], lp['w_qkv_t'], lp['qk_gain_col'], tm)
        ya = _attn_prompt(qkv_t, lp['attn_sinks'], nseq=nseq, t=t, tq=_tiles(t, (256, 128)))
        nq = N_HEADS * HEAD_DIM
        last = qkv_t[nq:, :].reshape(2, N_KV_HEADS, HEAD_DIM, nseq, t)[..., t - WINDOW:]
        last = jnp.transpose(last, (0, 3, 4, 1, 2)).reshape(2, nseq * WINDOW, _KV_W)
        kn, v_rows = last[0], last[1]
        cs = _tiles(t, (128, 64, 32, 16))
        tt = _tiles(t, (512, 256, 128, 64, 32, 16))
    else:
        p_all = _proj(x, lp['norm_mix'], lp['w_in'], tm_proj, 1536, lp['w_in'].shape[1])
        ya, kn = _attn_sample(p_all, ck, cv, lp['q_norm'], lp['k_norm'], lp['attn_sinks'],
                              nseq=nseq, t=t)
        v_rows = p_all[:, _C_AV * _KV_W:(_C_AV + 1) * _KV_W]
        cs = t
        tt = t
    yb, s_new = _hgrn(p_all, lp['hgrn_lb_logits'], lp['hgrn_o_norm'], s0,
                      nseq=nseq, t=t, cs=cs, layer=layer)
    yc, nbuf, hlast = _lru(p_all, lp['conv_w'], lp['conv_b'], lp['lru_w_a'], lp['lru_b_a'],
                           lp['lru_w_x'], lp['lru_b_x'], lp['lru_lambda'], cbuf, h0,
                           nseq=nseq, t=t, tt=tt)
    x = _merge(x, ya, yb, yc, p_all, lp['w_attn_o'], lp['w_hgrn_o'], lp['w_lru_o'], lp['w_out'],
               _tiles(n, (256, 128)))
    x = _ffn(x, lp['norm_ffn2'], lp['w_ffn2_up'], lp['w_ffn2_down'], tm)
    return x, (kn, v_rows, s_new, nbuf, hlast.reshape(nseq, d))


def _forward(x_prompt, x_sample, cache_attn_k, cache_attn_v, state_hgrn, state_conv, state_lru, w):
    bsz, seq, d = x_prompt.shape
    dbsz, dseq, _ = x_sample.shape
    depth = w['w_in'].shape[0]
    win_rows = cache_attn_k.shape[2]
    assert win_rows == WINDOW, "the attention kernels assume a full cached window"
    xp = x_prompt.reshape(bsz * seq, d)
    xs = x_sample.reshape(dbsz * dseq, d)
    zero_cache = (None, None,
                  jnp.zeros((bsz, HG_HEADS, HG_DK, HG_DV), F32),
                  jnp.zeros((bsz, CONV_WIDTH - 1, d), F32),
                  jnp.zeros((bsz, 1, d), F32))
    st_p, st_s = [], []
    for l in range(depth):
        lp = {
            'norm_ffn1': w['norm_ffn1'][l][None], 'norm_mix': w['norm_mix'][l][None],
            'norm_ffn2': w['norm_ffn2'][l][None],
            'w_ffn1_up': w['w_ffn1_up'][l].astype(BF16), 'w_ffn1_down': w['w_ffn1_down'][l].astype(BF16),
            'w_ffn2_up': w['w_ffn2_up'][l].astype(BF16), 'w_ffn2_down': w['w_ffn2_down'][l].astype(BF16),
            'w_in': _permute_w_in(w['w_in'][l]), 'w_qkv_t': _qkv_weight_t(w['w_in'][l]),
            'qk_gain_col': _qk_gain_col(w['q_norm'][l], w['k_norm'][l]),
            'q_norm': w['q_norm'][l][None], 'k_norm': w['k_norm'][l][None],
            'attn_sinks': w['attn_sinks'][l],
            'w_attn_o': w['w_attn_o'][l].astype(BF16), 'w_hgrn_o': w['w_hgrn_o'][l].astype(BF16),
            'w_lru_o': w['w_lru_o'][l].astype(BF16), 'w_out': w['w_out'][l].astype(BF16),
            'hgrn_lb_logits': w['hgrn_lb_logits'], 'hgrn_o_norm': w['hgrn_o_norm'][l][None],
            'conv_w': w['conv_w'][l], 'conv_b': w['conv_b'][l][None],
            'lru_w_a': _block_diag(w['lru_w_a'][l]), 'lru_b_a': w['lru_b_a'][l][None],
            'lru_w_x': _block_diag(w['lru_w_x'][l]), 'lru_b_x': w['lru_b_x'][l][None],
            'lru_lambda': w['lru_lambda'][l][None],
        }
        xp, sp = _layer(xp, lp, zero_cache, nseq=bsz, t=seq, layer=l, prompt=True)
        cache = (cache_attn_k[l].reshape(dbsz * win_rows, _KV_W),
                 cache_attn_v[l].reshape(dbsz * win_rows, _KV_W),
                 state_hgrn[l], state_conv[l], state_lru[l][:, None, :])
        xs, ss = _layer(xs, lp, cache, nseq=dbsz, t=dseq, layer=l, prompt=False)
        kn, vr, s_new, nbuf, hl = sp
        st_p.append((kn.reshape(bsz, win_rows, N_KV_HEADS, HEAD_DIM),
                     vr.reshape(bsz, win_rows, N_KV_HEADS, HEAD_DIM), s_new, nbuf, hl))
        kn, vr, s_new, nbuf, hl = ss
        st_s.append((kn.reshape(dbsz, dseq, N_KV_HEADS, HEAD_DIM),
                     vr.reshape(dbsz, dseq, N_KV_HEADS, HEAD_DIM), s_new, nbuf, hl))

    def stack(sts, i):
        return jnp.stack([s[i] for s in sts], axis=0)

    return (xp.reshape(bsz, seq, d), xs.reshape(dbsz, dseq, d),
            stack(st_p, 0), stack(st_p, 1), stack(st_p, 2), stack(st_p, 3), stack(st_p, 4),
            stack(st_s, 0), stack(st_s, 1), stack(st_s, 2), stack(st_s, 3), stack(st_s, 4))


def kernel(x_prompt, x_sample, cache_attn_k, cache_attn_v, state_hgrn, state_conv, state_lru,
           norm_ffn1, w_ffn1_up, w_ffn1_down, norm_mix, w_in, q_norm, k_norm, attn_sinks, w_attn_o,
           hgrn_lb_logits, hgrn_o_norm, w_hgrn_o, conv_w, conv_b, lru_w_a, lru_b_a, lru_w_x, lru_b_x,
           lru_lambda, w_lru_o, w_out, norm_ffn2, w_ffn2_up, w_ffn2_down):
    w = dict(norm_ffn1=norm_ffn1, w_ffn1_up=w_ffn1_up, w_ffn1_down=w_ffn1_down, norm_mix=norm_mix,
             w_in=w_in, q_norm=q_norm, k_norm=k_norm, attn_sinks=attn_sinks, w_attn_o=w_attn_o,
             hgrn_lb_logits=hgrn_lb_logits, hgrn_o_norm=hgrn_o_norm, w_hgrn_o=w_hgrn_o,
             conv_w=conv_w, conv_b=conv_b, lru_w_a=lru_w_a, lru_b_a=lru_b_a, lru_w_x=lru_w_x,
             lru_b_x=lru_b_x, lru_lambda=lru_lambda, w_lru_o=w_lru_o, w_out=w_out,
             norm_ffn2=norm_ffn2, w_ffn2_up=w_ffn2_up, w_ffn2_down=w_ffn2_down)
    return _forward(x_prompt, x_sample, cache_attn_k, cache_attn_v, state_hgrn, state_conv,
                    state_lru, w)
```

```python
import functools

import numpy as np
import jax
import jax.numpy as jnp
from jax import lax
from jax.experimental import pallas as pl
from jax.experimental.pallas import tpu as pltpu

F32 = jnp.float32
BF16 = jnp.bfloat16

CHUNK = 64
N_HEADS = 16
N_KV_HEADS = 4
HEAD_DIM = 64
GROUP = N_HEADS // N_KV_HEADS
WINDOW = 128
SCALE = HEAD_DIM ** -0.5
HG_HEADS = 8
HG_DK = 128
HG_DV = 128
LOG_F_FLOOR = -60.0
CONV_WIDTH = 4
LRU_C = 8.0
LRU_BW = 64
EPS = 1e-6
NEG = -1e30

MXU_DIM = 256
VMEM_LIMIT_BYTES = 56 * 1024 * 1024

_SLAB = 1024
(_C_HQ, _C_HF, _C_HI, _C_HG, _C_LX, _C_LG, _C_GA, _C_GB, _C_GC, _C_AQ) = range(10)
_N_MIX_COLS = 9 * _SLAB
_KV_W = N_KV_HEADS * HEAD_DIM
_C_AK = 10 * _SLAB // _KV_W
_C_AV = _C_AK + 1
_QKV_ROWS = N_HEADS * HEAD_DIM + 2 * _KV_W


def _cparams(sem):
    return pltpu.CompilerParams(dimension_semantics=sem, vmem_limit_bytes=VMEM_LIMIT_BYTES)


def _rms(x, g):
    return x * lax.rsqrt(jnp.mean(x * x, axis=-1, keepdims=True) + EPS) * g


def _sigmoid(x):
    return 1.0 / (1.0 + jnp.exp(-x))


def _silu(x):
    return x * _sigmoid(x)


def _dot(a, b):
    return jnp.dot(a, b, preferred_element_type=F32)


def _dot_nt(a, b):
    return lax.dot_general(a, b, (((1,), (1,)), ((), ())), preferred_element_type=F32)


def _dot_tn(a, b):
    return lax.dot_general(a, b, (((0,), (0,)), ((), ())), preferred_element_type=F32)


def _ff_chunks(dff, width):
    assert dff % MXU_DIM == 0
    edges = list(range(0, dff, width)) + [dff]
    return list(zip(edges[:-1], edges[1:]))


def _ffn_kernel(x_ref, g_ref, wu_ref, wd_ref, o_ref, *, chunks):
    x = x_ref[...]
    dff = wd_ref.shape[0]
    h = _rms(x, g_ref[...]).astype(BF16)
    acc = None
    for lo, hi in chunks:
        gate = _dot(h, wu_ref[:, lo:hi])
        val = _dot(h, wu_ref[:, dff + lo:dff + hi])
        part = _dot((_silu(gate) * val).astype(BF16), wd_ref[lo:hi, :])
        acc = part if acc is None else acc + part
    o_ref[...] = x + 0.5 * acc


def _resident(shape):
    return pl.BlockSpec(shape, lambda *_: (0,) * len(shape), pipeline_mode=pl.Buffered(1))


def _ffn(x, g, w_up, w_down, tm):
    n, d = x.shape
    dff = w_down.shape[0]
    return pl.pallas_call(
        functools.partial(_ffn_kernel, chunks=_ff_chunks(dff, 6 * MXU_DIM)),
        grid=(n // tm,),
        in_specs=[
            pl.BlockSpec((tm, d), lambda i: (i, 0)),
            _resident((1, d)),
            _resident(w_up.shape),
            _resident(w_down.shape),
        ],
        out_specs=pl.BlockSpec((tm, d), lambda i: (i, 0)),
        out_shape=jax.ShapeDtypeStruct((n, d), F32),
        compiler_params=_cparams(("parallel",)),
        name="ffn",
    )(x, g, w_up, w_down)


def _proj_kernel(x_ref, g_ref, w_ref, o_ref, h_ref):
    @pl.when(pl.program_id(1) == 0)
    def _():
        h_ref[...] = _rms(x_ref[...], g_ref[...]).astype(BF16)

    tn = o_ref.shape[1]
    col = pl.multiple_of(pl.program_id(1) * tn, tn)
    o_ref[...] = _dot(h_ref[...], w_ref[:, pl.ds(col, tn)])


def _proj(x, g, w, tm, tn, cols):
    n, d = x.shape
    return pl.pallas_call(
        _proj_kernel,
        grid=(n // tm, cols // tn),
        in_specs=[
            pl.BlockSpec((tm, d), lambda i, j: (i, 0)),
            _resident((1, d)),
            _resident(w.shape),
        ],
        out_specs=pl.BlockSpec((tm, tn), lambda i, j: (i, j)),
        out_shape=jax.ShapeDtypeStruct((n, cols), F32),
        scratch_shapes=[pltpu.VMEM((tm, d), BF16)],
        compiler_params=_cparams(("parallel", "arbitrary")),
        name="proj",
    )(x, g, w)


def _attn_sample_kernel(q_ref, kc_ref, vc_ref, kp_ref, vp_ref, qg_ref, kg_ref, sinks_ref,
                        ya_ref, kn_ref, *, t):
    qg = qg_ref[...] * SCALE
    kg = kg_ref[...]
    knew = jnp.concatenate(
        [_rms(kc_ref[:, j * HEAD_DIM:(j + 1) * HEAD_DIM], kg) for j in range(N_KV_HEADS)], axis=1)
    kn_ref[...] = knew
    for sq in range(q_ref.shape[0] // t):
        new = slice(sq * t, (sq + 1) * t)
        old = slice(sq * WINDOW, (sq + 1) * WINDOW)
        for j in range(N_KV_HEADS):
            sl = slice(j * HEAD_DIM, (j + 1) * HEAD_DIM)
            kb = jnp.concatenate([kp_ref[old, sl], knew[new, sl]], axis=0).astype(BF16)
            vb = jnp.concatenate([vp_ref[old, sl], vc_ref[new, sl]], axis=0).astype(BF16)
            sk = jnp.concatenate(
                [jnp.full((t, 1), sinks_ref[j * GROUP + g], F32) for g in range(GROUP)], axis=0)
            qt = q_ref[new, j * GROUP * HEAD_DIM:(j + 1) * GROUP * HEAD_DIM]
            qcat = jnp.concatenate(
                [_rms(qt[:, g * HEAD_DIM:(g + 1) * HEAD_DIM], qg) for g in range(GROUP)],
                axis=0).astype(BF16)
            s = _dot_nt(qcat, kb)
            m = jnp.maximum(jnp.max(s, axis=-1, keepdims=True), sk)
            p = jnp.exp(s - m)
            den = jnp.sum(p, axis=-1, keepdims=True) + jnp.exp(sk - m)
            p = p / den
            o = _dot(p.astype(BF16), vb)
            ya_ref[new, j * GROUP * HEAD_DIM:(j + 1) * GROUP * HEAD_DIM] = jnp.concatenate(
                [o[g * t:(g + 1) * t] for g in range(GROUP)], axis=1)


def _attn_sample(p_all, cache_k, cache_v, qg, kg, sinks, *, nseq, t, layer, sb):
    n = nseq * t
    prev = pl.BlockSpec((None, sb * WINDOW, _KV_W), lambda b: (layer, b, 0))
    return pl.pallas_call(
        functools.partial(_attn_sample_kernel, t=t),
        grid=(nseq // sb,),
        in_specs=[
            pl.BlockSpec((sb * t, _SLAB), lambda b: (b, _C_AQ)),
            pl.BlockSpec((sb * t, _KV_W), lambda b: (b, _C_AK)),
            pl.BlockSpec((sb * t, _KV_W), lambda b: (b, _C_AV)),
            prev, prev,
            _resident((1, HEAD_DIM)),
            _resident((1, HEAD_DIM)),
            pl.BlockSpec(memory_space=pltpu.SMEM),
        ],
        out_specs=[
            pl.BlockSpec((sb * t, _SLAB), lambda b: (b, 0)),
            pl.BlockSpec((sb * t, _KV_W), lambda b: (b, 0)),
        ],
        out_shape=[jax.ShapeDtypeStruct((n, _SLAB), F32), jax.ShapeDtypeStruct((n, _KV_W), F32)],
        compiler_params=_cparams(("parallel",)),
        name="attn_sample",
    )(p_all, p_all, p_all, cache_k, cache_v, qg, kg, sinks)


def _qkv_t_kernel(x_ref, g_ref, w_ref, gc_ref, o_ref):
    h = _rms(x_ref[...], g_ref[...]).astype(BF16)
    acc = _dot_nt(w_ref[...], h)
    nqk = gc_ref.shape[0]
    for r in range(0, nqk, HEAD_DIM):
        blk = acc[r:r + HEAD_DIM]
        ms = jnp.sum(blk * blk, axis=0, keepdims=True) * (1.0 / HEAD_DIM)
        o_ref[r:r + HEAD_DIM, :] = blk * lax.rsqrt(ms + EPS) * gc_ref[r:r + HEAD_DIM, :]
    o_ref[nqk:, :] = acc[nqk:]


def _qkv_t(x, g, w_t, gain_col, tm):
    n, d = x.shape
    rows = w_t.shape[0]
    return pl.pallas_call(
        _qkv_t_kernel,
        grid=(n // tm,),
        in_specs=[
            pl.BlockSpec((tm, d), lambda i: (i, 0)),
            _resident((1, d)),
            _resident((rows, d)),
            _resident(gain_col.shape),
        ],
        out_specs=pl.BlockSpec((rows, tm), lambda i: (0, i)),
        out_shape=jax.ShapeDtypeStruct((rows, n), F32),
        compiler_params=_cparams(("parallel",)),
        name="qkv_t",
    )(x, g, w_t, gain_col)


_PAIR = 2 * CHUNK
_KEYS = _PAIR + WINDOW


def _attn_prompt_kernel(q_ref, kc_ref, vc_ref, kp_ref, vp_ref, sinks_ref, ya_ref, yat_ref):
    tq = q_ref.shape[1]
    t = pl.program_id(1)
    lane = lax.broadcasted_iota(jnp.int32, (1, GROUP * _PAIR), 1)
    first_chunk = (lane % _PAIR) < CHUNK
    k_cur = kc_ref[...].T.astype(BF16)
    k_prev = kp_ref[...].T.astype(BF16)
    for p in range(tq // _PAIR):
        lo = p * _PAIR
        if p == 0:
            kb = jnp.concatenate([k_prev, k_cur[:_PAIR]], axis=0)
        else:
            kb = k_cur[lo - WINDOW:lo + _PAIR]
        first_key_chunk = t * (tq // CHUNK) + (lo - WINDOW) // CHUNK
        for j in range(N_KV_HEADS):
            rows = slice(j * HEAD_DIM, (j + 1) * HEAD_DIM)
            if p == 0:
                vb = jnp.concatenate([vp_ref[rows, :], vc_ref[rows, :_PAIR]], axis=1)
            else:
                vb = vc_ref[rows, lo - WINDOW:lo + _PAIR]
            qcat = jnp.concatenate(
                [q_ref[(j * GROUP + g) * HEAD_DIM:(j * GROUP + g + 1) * HEAD_DIM, lo:lo + _PAIR]
                 for g in range(GROUP)], axis=1).astype(BF16)
            zero = jnp.zeros_like(qcat)
            qz = jnp.concatenate([qcat if i == j else zero for i in range(N_KV_HEADS)], axis=0)
            s = _dot(kb, qz)
            s = jnp.concatenate([
                jnp.where(first_chunk & (first_key_chunk >= 0), s[:CHUNK], NEG),
                jnp.where(first_key_chunk + 1 >= 0, s[CHUNK:2 * CHUNK], NEG),
                s[2 * CHUNK:3 * CHUNK],
                jnp.where(first_chunk, NEG, s[3 * CHUNK:]),
            ], axis=0)
            sk = jnp.concatenate(
                [jnp.full((1, _PAIR), sinks_ref[j * GROUP + g], F32) for g in range(GROUP)], axis=1)
            m = jnp.maximum(jnp.max(s, axis=0, keepdims=True), sk)
            e = jnp.exp(s - m)
            den = jnp.sum(e, axis=0, keepdims=True) + jnp.exp(sk - m)
            pn = (e * (1.0 / den)).astype(BF16)
            o = _dot(vb.astype(BF16), pn)
            for g in range(GROUP):
                yat_ref[(j * GROUP + g) * HEAD_DIM:(j * GROUP + g + 1) * HEAD_DIM, lo:lo + _PAIR] = (
                    o[:, g * _PAIR:(g + 1) * _PAIR])
    ya_ref[...] = yat_ref[...].T


def _attn_prompt(qkv_t, sinks, *, nseq, t, tq):
    n = nseq * t
    nt = t // tq
    nq = N_HEADS * HEAD_DIM
    kblk, vblk = nq // _KV_W, nq // _KV_W + 1

    def prev_map(blk):
        return lambda b, i: (blk, jnp.maximum(b * (t // WINDOW) + i * (tq // WINDOW) - 1, 0))

    return pl.pallas_call(
        _attn_prompt_kernel,
        grid=(nseq, nt),
        in_specs=[
            pl.BlockSpec((nq, tq), lambda b, i: (0, b * nt + i)),
            pl.BlockSpec((_KV_W, tq), lambda b, i: (kblk, b * nt + i)),
            pl.BlockSpec((_KV_W, tq), lambda b, i: (vblk, b * nt + i)),
            pl.BlockSpec((_KV_W, WINDOW), prev_map(kblk)),
            pl.BlockSpec((_KV_W, WINDOW), prev_map(vblk)),
            pl.BlockSpec(memory_space=pltpu.SMEM),
        ],
        out_specs=pl.BlockSpec((tq, nq), lambda b, i: (b * nt + i, 0)),
        out_shape=jax.ShapeDtypeStruct((n, nq), F32),
        scratch_shapes=[pltpu.VMEM((nq, tq), F32)],
        compiler_params=_cparams(("parallel", "arbitrary")),
        name="attn_prompt",
    )(qkv_t, qkv_t, qkv_t, qkv_t, qkv_t, sinks)


def _hgrn_levels(c):
    return [c >> (i + 1) for i in range(int(np.log2(c)))]


_SUBLANES = 8


def _level_operands(q, k, g, s):
    rows = q.shape[0]
    if s < _SUBLANES:
        upper = (lax.broadcasted_iota(jnp.int32, (rows, 1), 0) & s) != 0
        return jnp.where(upper, q * g, 0.0), jnp.where(upper, 0.0, k * g)
    zero = jnp.zeros((s, q.shape[1]), F32)
    qparts, kparts = [], []
    for g0 in range(0, rows, 2 * s):
        lo, up = slice(g0, g0 + s), slice(g0 + s, g0 + 2 * s)
        kparts += [k[lo] * g[lo], zero]
        qparts += [zero, q[up] * g[up]]
    return jnp.concatenate(qparts, axis=0), jnp.concatenate(kparts, axis=0)


def _hgrn_kernel(hq_ref, hf_ref, hi_ref, hg_ref, lbl_ref, on_ref, s0_ref, yb_ref, sout_ref,
                 a_ref, a2_ref, m_ref, st_ref, *, layer, cs):
    c = pl.program_id(1)
    rows = hq_ref.shape[0]
    nsq = rows // cs
    levels = _hgrn_levels(cs)

    @pl.when(c == 0)
    def _():
        row = lax.broadcasted_iota(jnp.int32, (rows, rows), 0)
        col = lax.broadcasted_iota(jnp.int32, (rows, rows), 1)
        mats = [((col <= row) & ((row & ~(cs - 1)) == (col & ~(cs - 1)))).astype(F32)]
        for i, s in enumerate(levels):
            mid = (row & ~(2 * s - 1)) + s
            upper = (row & s) != 0
            up = ((col >= mid) & (col <= row)).astype(F32)
            lo = ((col > row) & (col < mid)).astype(F32)
            mats.append(jnp.where(upper, up, lo))
            same = (row & ~(2 * s - 1)) == (col & ~(2 * s - 1))
            m_ref[i] = same.astype(F32)
        for i, a in enumerate(mats):
            a = a.astype(BF16)
            a_ref[i] = a
            a2_ref[i] = jnp.concatenate([a, a], axis=1)
        for sq in range(nsq):
            for h in range(HG_HEADS):
                st_ref[sq * HG_HEADS + h] = s0_ref[sq, h].T

    lg = lbl_ref[...]
    e = jnp.exp(lg - jnp.max(lg, axis=0, keepdims=True))
    lb = jnp.zeros((1, e.shape[1]), F32)
    for i in range(1, layer + 1):
        lb = lb + e[i:i + 1]
    lb = lb / jnp.sum(e, axis=0, keepdims=True)

    sg = _sigmoid(hf_ref[...])
    f = lb + (1.0 - lb) * sg
    logf = jnp.maximum(jnp.log(jnp.maximum(f, 1e-26)), LOG_F_FLOOR)
    kk = (1.0 - lb) * (1.0 - sg)
    l1 = logf.astype(BF16)
    r1 = logf - l1.astype(F32)
    l2 = r1.astype(BF16)
    l3 = (r1 - l2.astype(F32)).astype(BF16)
    l12 = jnp.concatenate([l1, l2], axis=0)

    def lin(i):
        return _dot(a2_ref[i], l12) + _dot(a_ref[i], l3)

    bc = lin(0)
    ex = [lin(i + 1) for i in range(len(levels))]
    on = on_ref[...]

    for h in range(HG_HEADS):
        sl = slice(h * HG_DK, (h + 1) * HG_DK)
        q = _silu(hq_ref[:, sl])
        v = hi_ref[:, sl]
        vb = v.astype(BF16)
        k = kk[:, sl]
        b = bc[:, sl]
        att = None
        for i, s in enumerate(levels):
            qs, ks = _level_operands(q, k, jnp.exp(ex[i][:, sl]), s)
            part = _dot_nt(qs.astype(BF16), ks.astype(BF16))
            if 2 * s < rows:
                part = part * m_ref[i]
            att = part if att is None else att + part
        o = _dot(att.astype(BF16), vb) + jnp.sum(q * k, axis=-1, keepdims=True) * v
        qd = (q * jnp.exp(b)).astype(BF16)
        inter = []
        for sq in range(nsq):
            r = slice(sq * cs, (sq + 1) * cs)
            st = st_ref[sq * HG_HEADS + h]
            inter.append(_dot_nt(qd[r], st.astype(BF16)))
            blast = b[(sq + 1) * cs - 1:(sq + 1) * cs, :]
            kd = (k[r] * jnp.exp(blast - b[r])).astype(BF16)
            st_ref[sq * HG_HEADS + h] = st * jnp.exp(blast) + _dot_tn(vb[r], kd)
        o = o + (inter[0] if nsq == 1 else jnp.concatenate(inter, axis=0))
        yb_ref[:, sl] = _rms(o, on) * _silu(hg_ref[:, sl])

    @pl.when(c == pl.num_programs(1) - 1)
    def _():
        for sq in range(nsq):
            for h in range(HG_HEADS):
                sout_ref[sq, h] = st_ref[sq * HG_HEADS + h].T


def _hgrn(p_all, lb_logits, o_norm, s0, *, nseq, t, cs, sb, layer, s0_layer):
    assert sb == 1 or cs == t
    n = nseq * t
    nc = t // cs
    rows = sb * cs
    nlev = len(_hgrn_levels(cs))

    def slab(col):
        return pl.BlockSpec((rows, _SLAB), lambda b, i: (b * nc + i, col))

    st_block = (sb, HG_HEADS, HG_DK, HG_DV)
    return pl.pallas_call(
        functools.partial(_hgrn_kernel, layer=layer, cs=cs),
        grid=(nseq // sb, nc),
        in_specs=[
            slab(_C_HQ), slab(_C_HF), slab(_C_HI), slab(_C_HG),
            _resident(lb_logits.shape),
            _resident((1, HG_DV)),
            pl.BlockSpec((None,) + st_block, lambda b, i: (s0_layer, b, 0, 0, 0)),
        ],
        out_specs=[pl.BlockSpec((rows, _SLAB), lambda b, i: (b * nc + i, 0)),
                   pl.BlockSpec(st_block, lambda b, i: (b, 0, 0, 0))],
        out_shape=[jax.ShapeDtypeStruct((n, _SLAB), F32),
                   jax.ShapeDtypeStruct((nseq, HG_HEADS, HG_DK, HG_DV), F32)],
        scratch_shapes=[
            pltpu.VMEM((nlev + 1, rows, rows), BF16),
            pltpu.VMEM((nlev + 1, rows, 2 * rows), BF16),
            pltpu.VMEM((nlev, rows, rows), F32),
            pltpu.VMEM((sb * HG_HEADS, HG_DV, HG_DK), F32),
        ],
        compiler_params=_cparams(("parallel", "arbitrary")),
        name="hgrn_t%d" % cs,
    )(p_all, p_all, p_all, p_all, lb_logits, o_norm, s0)


_PAD = 8


def _lru_kernel(lx_ref, lg_ref, cw_ref, cb_ref, wa_ref, ba_ref, wx_ref, bx_ref, lam_ref,
                cbuf_ref, h0_ref, yc_ref, nbuf_ref, hlast_ref, xe_ref, a_ref, b_ref, hc_ref):
    t = pl.program_id(1)
    tt = lx_ref.shape[0]
    w = lx_ref.shape[1]
    hist = CONV_WIDTH - 1

    @pl.when(t == 0)
    def _():
        xe_ref[_PAD - hist:_PAD, :] = cbuf_ref[0]
        hc_ref[...] = h0_ref[0]

    @pl.when(t > 0)
    def _():
        xe_ref[_PAD - hist:_PAD, :] = xe_ref[_PAD + tt - hist:_PAD + tt, :]

    xe_ref[_PAD:_PAD + tt, :] = lx_ref[...]
    xc = xe_ref[_PAD - hist:_PAD - hist + tt, :] * cw_ref[0:1, :]
    for j in range(1, CONV_WIDTH):
        xc = xc + xe_ref[_PAD - hist + j:_PAD - hist + j + tt, :] * cw_ref[j:j + 1, :]
    xc = xc + cb_ref[...]

    ra, rx = [], []
    for g in range(w // MXU_DIM):
        xg = xc[:, g * MXU_DIM:(g + 1) * MXU_DIM].astype(BF16)
        ra.append(_dot(xg, wa_ref[g]))
        rx.append(_dot(xg, wx_ref[g]))
    r = _sigmoid(jnp.concatenate(ra, axis=1) + ba_ref[...])
    ig = _sigmoid(jnp.concatenate(rx, axis=1) + bx_ref[...])
    nl = -lam_ref[...]
    softplus = jnp.maximum(nl, 0.0) + jnp.log1p(jnp.exp(-jnp.abs(nl)))
    log_a = -LRU_C * r * softplus
    a = jnp.exp(log_a)
    a_ref[...] = a
    b_ref[...] = jnp.sqrt(-jnp.tanh(log_a) * (a * a + 1.0)) * (ig * xc)

    def step(i, h):
        h = a_ref[pl.ds(i, 1), :] * h + b_ref[pl.ds(i, 1), :]
        b_ref[pl.ds(i, 1), :] = h
        return h

    h = lax.fori_loop(0, tt, step, hc_ref[...], unroll=8)
    hc_ref[...] = h
    lgv = lg_ref[...]
    gelu = 0.5 * lgv * (1.0 + jnp.tanh(np.sqrt(2.0 / np.pi) * (lgv + 0.044715 * (lgv * lgv * lgv))))
    yc_ref[...] = b_ref[...] * gelu

    @pl.when(t == pl.num_programs(1) - 1)
    def _():
        nbuf_ref[0] = xe_ref[_PAD + tt - hist:_PAD + tt, :]
        hlast_ref[0] = h


def _lru(p_all, cw, cb, wa, ba, wx, bx, lam, cbuf, h0, *, nseq, t, tt):
    n = nseq * t
    nt = t // tt
    w = _SLAB
    hist = CONV_WIDTH - 1

    def vec(rows):
        return pl.BlockSpec((rows, w), lambda b, i: (0, 0))

    wspec = pl.BlockSpec(wa.shape, lambda b, i: (0, 0, 0))
    return pl.pallas_call(
        _lru_kernel,
        grid=(nseq, nt),
        in_specs=[
            pl.BlockSpec((tt, w), lambda b, i: (b * nt + i, _C_LX)),
            pl.BlockSpec((tt, w), lambda b, i: (b * nt + i, _C_LG)),
            vec(CONV_WIDTH), vec(1), wspec, vec(1), wspec, vec(1), vec(1),
            pl.BlockSpec((1, hist, w), lambda b, i: (b, 0, 0)),
            pl.BlockSpec((1, 1, w), lambda b, i: (b, 0, 0)),
        ],
        out_specs=[
            pl.BlockSpec((tt, w), lambda b, i: (b * nt + i, 0)),
            pl.BlockSpec((1, hist, w), lambda b, i: (b, 0, 0)),
            pl.BlockSpec((1, 1, w), lambda b, i: (b, 0, 0)),
        ],
        out_shape=[jax.ShapeDtypeStruct((n, w), F32),
                   jax.ShapeDtypeStruct((nseq, hist, w), F32),
                   jax.ShapeDtypeStruct((nseq, 1, w), F32)],
        scratch_shapes=[
            pltpu.VMEM((_PAD + tt, w), F32),
            pltpu.VMEM((tt, w), F32),
            pltpu.VMEM((tt, w), F32),
            pltpu.VMEM((1, w), F32),
        ],
        compiler_params=_cparams(("parallel", "arbitrary")),
        name="lru_t%d" % tt,
    )(p_all, p_all, cw, cb, wa, ba, wx, bx, lam, cbuf, h0)


def _merge_kernel(x_ref, ya_ref, yb_ref, yc_ref, ga_ref, gb_ref, gc_ref,
                  wa_ref, wb_ref, wc_ref, wo_ref, o_ref):
    m = _sigmoid(ga_ref[...]) * _dot(ya_ref[...].astype(BF16), wa_ref[...])
    m = m + _sigmoid(gb_ref[...]) * _dot(yb_ref[...].astype(BF16), wb_ref[...])
    m = m + _sigmoid(gc_ref[...]) * _dot(yc_ref[...].astype(BF16), wc_ref[...])
    o_ref[...] = x_ref[...] + _dot(m.astype(BF16), wo_ref[...])


def _merge(x, ya, yb, yc, p_all, wa, wb, wc, wo, tm):
    n, d = x.shape
    row = pl.BlockSpec((tm, d), lambda i: (i, 0))

    def slab(col):
        return pl.BlockSpec((tm, _SLAB), lambda i: (i, col))

    wspec = _resident((d, d))
    return pl.pallas_call(
        _merge_kernel,
        grid=(n // tm,),
        in_specs=[row, row, row, row, slab(_C_GA), slab(_C_GB), slab(_C_GC),
                  wspec, wspec, wspec, wspec],
        out_specs=row,
        out_shape=jax.ShapeDtypeStruct((n, d), F32),
        compiler_params=_cparams(("parallel",)),
        name="merge",
    )(x, ya, yb, yc, p_all, p_all, p_all, wa, wb, wc, wo)


def _permute_w_in(w_in):
    aq, ak, av, rest = jnp.split(w_in, [N_HEADS * HEAD_DIM, N_HEADS * HEAD_DIM + _KV_W,
                                        N_HEADS * HEAD_DIM + 2 * _KV_W], axis=-1)
    return jnp.concatenate([rest, aq, ak, av], axis=-1).astype(BF16)


def _qkv_weight_t(w_in):
    return w_in[:, :_QKV_ROWS].T.astype(BF16)


def _qk_gain_col(q_norm, k_norm):
    return jnp.concatenate([jnp.tile(q_norm * SCALE, N_HEADS), jnp.tile(k_norm, N_KV_HEADS)])[:, None]


def _block_diag(w):
    nb, bw, _ = w.shape
    per = MXU_DIM // bw
    wg = w.reshape(nb // per, per, bw, bw)
    eye = jnp.eye(per, dtype=w.dtype)
    out = jnp.einsum('gpij,pq->gpiqj', wg, eye)
    return out.reshape(nb // per, MXU_DIM, MXU_DIM).astype(BF16)


def _tiles(n, prefer):
    for tm in prefer:
        if n % tm == 0:
            return tm
    return n


def _layer(x, lp, cache, *, nseq, t, layer, prompt):
    n, d = x.shape
    tm = _tiles(n, (512, 256, 128))
    x = _ffn(x, lp['norm_ffn1'], lp['w_ffn1_up'], lp['w_ffn1_down'], tm)
    ck, cv, s0, cbuf, h0 = cache
    tm_proj = _tiles(n, (1024, 512, 256, 128))
    if prompt:
        p_all = _proj(x, lp['norm_mix'], lp['w_in'], tm_proj, 1536, _N_MIX_COLS)
        qkv_t = _qkv_t(x, lp['norm_mix'], lp['w_qkv_t'], lp['qk_gain_col'], tm)
        ya = _attn_prompt(qkv_t, lp['attn_sinks'], nseq=nseq, t=t, tq=_tiles(t, (256, 128)))
        nq = N_HEADS * HEAD_DIM
        last = qkv_t[nq:, :].reshape(2, N_KV_HEADS, HEAD_DIM, nseq, t)[..., t - WINDOW:]
        last = jnp.transpose(last, (0, 3, 4, 1, 2)).reshape(2, nseq * WINDOW, _KV_W)
        kn, v_rows = last[0], last[1]
        cs = _tiles(t, (128, 64, 32, 16))
        tt = _tiles(t, (512, 256, 128, 64, 32, 16))
    else:
        p_all = _proj(x, lp['norm_mix'], lp['w_in'], tm_proj, 1536, lp['w_in'].shape[1])
        ya, kn = _attn_sample(p_all, ck, cv, lp['q_norm'], lp['k_norm'], lp['attn_sinks'],
                              nseq=nseq, t=t, layer=layer, sb=_tiles(nseq, (8, 4, 2, 1)))
        v_rows = p_all[:, _C_AV * _KV_W:(_C_AV + 1) * _KV_W]
        cs = t
        tt = t
    sb = 1 if cs < t else _tiles(nseq, (4, 2, 1))
    yb, s_new = _hgrn(p_all, lp['hgrn_lb_logits'], lp['hgrn_o_norm'], s0, nseq=nseq, t=t, cs=cs,
                      sb=sb, layer=layer, s0_layer=0 if prompt else layer)
    yc, nbuf, hlast = _lru(p_all, lp['conv_w'], lp['conv_b'], lp['lru_w_a'], lp['lru_b_a'],
                           lp['lru_w_x'], lp['lru_b_x'], lp['lru_lambda'], cbuf, h0,
                           nseq=nseq, t=t, tt=tt)
    x = _merge(x, ya, yb, yc, p_all, lp['w_attn_o'], lp['w_hgrn_o'], lp['w_lru_o'], lp['w_out'], tm)
    x = _ffn(x, lp['norm_ffn2'], lp['w_ffn2_up'], lp['w_ffn2_down'], tm)
    return x, (kn, v_rows, s_new, nbuf, hlast.reshape(nseq, d))


def _forward(x_prompt, x_sample, cache_attn_k, cache_attn_v, state_hgrn, state_conv, state_lru, w):
    bsz, seq, d = x_prompt.shape
    dbsz, dseq, _ = x_sample.shape
    depth = w['w_in'].shape[0]
    win_rows = cache_attn_k.shape[2]
    assert win_rows == WINDOW, "the attention kernels assume a full cached window"
    xp = x_prompt.reshape(bsz * seq, d)
    xs = x_sample.reshape(dbsz * dseq, d)
    zero_cache = (None, None,
                  jnp.zeros((1, bsz, HG_HEADS, HG_DK, HG_DV), F32),
                  jnp.zeros((bsz, CONV_WIDTH - 1, d), F32),
                  jnp.zeros((bsz, 1, d), F32))
    st_p, st_s = [], []
    for l in range(depth):
        lp = {
            'norm_ffn1': w['norm_ffn1'][l][None], 'norm_mix': w['norm_mix'][l][None],
            'norm_ffn2': w['norm_ffn2'][l][None],
            'w_ffn1_up': w['w_ffn1_up'][l].astype(BF16), 'w_ffn1_down': w['w_ffn1_down'][l].astype(BF16),
            'w_ffn2_up': w['w_ffn2_up'][l].astype(BF16), 'w_ffn2_down': w['w_ffn2_down'][l].astype(BF16),
            'w_in': _permute_w_in(w['w_in'][l]), 'w_qkv_t': _qkv_weight_t(w['w_in'][l]),
            'qk_gain_col': _qk_gain_col(w['q_norm'][l], w['k_norm'][l]),
            'q_norm': w['q_norm'][l][None], 'k_norm': w['k_norm'][l][None],
            'attn_sinks': w['attn_sinks'][l],
            'w_attn_o': w['w_attn_o'][l].astype(BF16), 'w_hgrn_o': w['w_hgrn_o'][l].astype(BF16),
            'w_lru_o': w['w_lru_o'][l].astype(BF16), 'w_out': w['w_out'][l].astype(BF16),
            'hgrn_lb_logits': w['hgrn_lb_logits'], 'hgrn_o_norm': w['hgrn_o_norm'][l][None],
            'conv_w': w['conv_w'][l], 'conv_b': w['conv_b'][l][None],
            'lru_w_a': _block_diag(w['lru_w_a'][l]), 'lru_b_a': w['lru_b_a'][l][None],
            'lru_w_x': _block_diag(w['lru_w_x'][l]), 'lru_b_x': w['lru_b_x'][l][None],
            'lru_lambda': w['lru_lambda'][l][None],
        }
        xp, sp = _layer(xp, lp, zero_cache, nseq=bsz, t=seq, layer=l, prompt=True)
        cache = (cache_attn_k.reshape(depth, dbsz * win_rows, _KV_W),
                 cache_attn_v.reshape(depth, dbsz * win_rows, _KV_W),
                 state_hgrn, state_conv[l], state_lru[l][:, None, :])
        xs, ss = _layer(xs, lp, cache, nseq=dbsz, t=dseq, layer=l, prompt=False)
        kn, vr, s_new, nbuf, hl = sp
        st_p.append((kn.reshape(bsz, win_rows, N_KV_HEADS, HEAD_DIM),
                     vr.reshape(bsz, win_rows, N_KV_HEADS, HEAD_DIM), s_new, nbuf, hl))
        kn, vr, s_new, nbuf, hl = ss
        st_s.append((kn.reshape(dbsz, dseq, N_KV_HEADS, HEAD_DIM),
                     vr.reshape(dbsz, dseq, N_KV_HEADS, HEAD_DIM), s_new, nbuf, hl))

    def stack(sts, i):
        return jnp.stack([s[i] for s in sts], axis=0)

    return (xp.reshape(bsz, seq, d), xs.reshape(dbsz, dseq, d),
            stack(st_p, 0), stack(st_p, 1), stack(st_p, 2), stack(st_p, 3), stack(st_p, 4),
            stack(st_s, 0), stack(st_s, 1), stack(st_s, 2), stack(st_s, 3), stack(st_s, 4))


def kernel(x_prompt, x_sample, cache_attn_k, cache_attn_v, state_hgrn, state_conv, state_lru,
           norm_ffn1, w_ffn1_up, w_ffn1_down, norm_mix, w_in, q_norm, k_norm, attn_sinks, w_attn_o,
           hgrn_lb_logits, hgrn_o_norm, w_hgrn_o, conv_w, conv_b, lru_w_a, lru_b_a, lru_w_x, lru_b_x,
           lru_lambda, w_lru_o, w_out, norm_ffn2, w_ffn2_up, w_ffn2_down):
    w = dict(norm_ffn1=norm_ffn1, w_ffn1_up=w_ffn1_up, w_ffn1_down=w_ffn1_down, norm_mix=norm_mix,
             w_in=w_in, q_norm=q_norm, k_norm=k_norm, attn_sinks=attn_sinks, w_attn_o=w_attn_o,
             hgrn_lb_logits=hgrn_lb_logits, hgrn_o_norm=hgrn_o_norm, w_hgrn_o=w_hgrn_o,
             conv_w=conv_w, conv_b=conv_b, lru_w_a=lru_w_a, lru_b_a=lru_b_a, lru_w_x=lru_w_x,
             lru_b_x=lru_b_x, lru_lambda=lru_lambda, w_lru_o=w_lru_o, w_out=w_out,
             norm_ffn2=norm_ffn2, w_ffn2_up=w_ffn2_up, w_ffn2_down=w_ffn2_down)
    return _forward(x_prompt, x_sample, cache_attn_k, cache_attn_v, state_hgrn, state_conv,
                    state_lru, w)
```

```python
import functools

import numpy as np
import jax
import jax.numpy as jnp
from jax import lax
from jax.experimental import pallas as pl
from jax.experimental.pallas import tpu as pltpu

F32 = jnp.float32
BF16 = jnp.bfloat16

CHUNK = 64
N_HEADS = 16
N_KV_HEADS = 4
HEAD_DIM = 64
GROUP = N_HEADS // N_KV_HEADS
WINDOW = 128
SCALE = HEAD_DIM ** -0.5
HG_HEADS = 8
HG_DK = 128
HG_DV = 128
LOG_F_FLOOR = -60.0
CONV_WIDTH = 4
LRU_C = 8.0
LRU_BW = 64
EPS = 1e-6
NEG = -1e30

MXU_DIM = 256
VMEM_LIMIT_BYTES = 56 * 1024 * 1024

_SLAB = 1024
(_C_HQ, _C_HF, _C_HI, _C_HG, _C_LX, _C_LG, _C_GA, _C_GB, _C_GC) = range(9)
_KV_W = N_KV_HEADS * HEAD_DIM
_QKV_ROWS = N_HEADS * HEAD_DIM + 2 * _KV_W


def _cparams(sem):
    return pltpu.CompilerParams(dimension_semantics=sem, vmem_limit_bytes=VMEM_LIMIT_BYTES)


def _rms(x, g):
    return x * lax.rsqrt(jnp.mean(x * x, axis=-1, keepdims=True) + EPS) * g


def _sigmoid(x):
    return 1.0 / (1.0 + jnp.exp(-x))


def _silu(x):
    return x * _sigmoid(x)


def _dot(a, b):
    return jnp.dot(a, b, preferred_element_type=F32)


def _dot_nt(a, b):
    return lax.dot_general(a, b, (((1,), (1,)), ((), ())), preferred_element_type=F32)


def _dot_tn(a, b):
    return lax.dot_general(a, b, (((0,), (0,)), ((), ())), preferred_element_type=F32)


def _ff_chunks(dff, width):
    assert dff % MXU_DIM == 0
    edges = list(range(0, dff, width)) + [dff]
    return list(zip(edges[:-1], edges[1:]))


def _ffn_kernel(x_ref, g_ref, wu_ref, wd_ref, o_ref, *, chunks):
    x = x_ref[...]
    dff = wd_ref.shape[0]
    h = _rms(x, g_ref[...]).astype(BF16)
    acc = None
    for lo, hi in chunks:
        gate = _dot(h, wu_ref[:, lo:hi])
        val = _dot(h, wu_ref[:, dff + lo:dff + hi])
        part = _dot((_silu(gate) * val).astype(BF16), wd_ref[lo:hi, :])
        acc = part if acc is None else acc + part
    o_ref[...] = x + 0.5 * acc


def _resident(shape):
    return pl.BlockSpec(shape, lambda *_: (0,) * len(shape), pipeline_mode=pl.Buffered(1))


def _layer_resident(stacked, layer):
    shape = stacked.shape[1:]
    return pl.BlockSpec((None,) + shape, lambda *_: (layer,) + (0,) * len(shape),
                        pipeline_mode=pl.Buffered(1))


def _ffn(x, g, w_up, w_down, tm, layer):
    n, d = x.shape
    dff = w_down.shape[1]
    return pl.pallas_call(
        functools.partial(_ffn_kernel, chunks=_ff_chunks(dff, 6 * MXU_DIM)),
        grid=(n // tm,),
        in_specs=[
            pl.BlockSpec((tm, d), lambda i: (i, 0)),
            _layer_resident(g, layer),
            _layer_resident(w_up, layer),
            _layer_resident(w_down, layer),
        ],
        out_specs=pl.BlockSpec((tm, d), lambda i: (i, 0)),
        out_shape=jax.ShapeDtypeStruct((n, d), F32),
        compiler_params=_cparams(("parallel",)),
        name="ffn",
    )(x, g, w_up, w_down)


def _proj_kernel(x_ref, g_ref, w_ref, o_ref, h_ref):
    @pl.when(pl.program_id(1) == 0)
    def _():
        h_ref[...] = _rms(x_ref[...], g_ref[...]).astype(BF16)

    tn = o_ref.shape[1]
    col = pl.multiple_of(pl.program_id(1) * tn, tn)
    o_ref[...] = _dot(h_ref[...], w_ref[:, pl.ds(col, tn)])


def _proj(x, g, w, tm, tn, layer):
    n, d = x.shape
    cols = w.shape[2]
    return pl.pallas_call(
        _proj_kernel,
        grid=(n // tm, cols // tn),
        in_specs=[
            pl.BlockSpec((tm, d), lambda i, j: (i, 0)),
            _layer_resident(g, layer),
            _layer_resident(w, layer),
        ],
        out_specs=pl.BlockSpec((tm, tn), lambda i, j: (i, j)),
        out_shape=jax.ShapeDtypeStruct((n, cols), F32),
        scratch_shapes=[pltpu.VMEM((tm, d), BF16)],
        compiler_params=_cparams(("parallel", "arbitrary")),
        name="proj",
    )(x, g, w)


_LANES = 128


def _softmax_pv(s, sk, vt):
    m = jnp.maximum(jnp.max(s, axis=0, keepdims=True), sk)
    e = jnp.exp(s - m)
    den = jnp.sum(e, axis=0, keepdims=True) + jnp.exp(sk - m)
    return _dot(vt, (e * (1.0 / den)).astype(BF16))


def _group_queries(q_ref, j, lanes):
    qcat = jnp.concatenate(
        [q_ref[(j * GROUP + g) * HEAD_DIM:(j * GROUP + g + 1) * HEAD_DIM, lanes]
         for g in range(GROUP)], axis=1).astype(BF16)
    zero = jnp.zeros_like(qcat)
    return jnp.concatenate([qcat if i == j else zero for i in range(N_KV_HEADS)], axis=0)


def _sink_row(sinks_ref, j, width):
    return jnp.concatenate(
        [jnp.full((1, width), sinks_ref[j * GROUP + g], F32) for g in range(GROUP)], axis=1)


def _attn_sample_kernel(qkv_ref, kp_ref, vp_ref, sinks_ref, ya_ref, yat_ref, *, t):
    nq = N_HEADS * HEAD_DIM
    ncache = kp_ref.shape[0]
    keys = ncache + _LANES
    kfull = jnp.concatenate([kp_ref[...], qkv_ref[nq:nq + _KV_W, :].T], axis=0).astype(BF16)
    vt = jnp.concatenate([vp_ref[...].T, qkv_ref[nq + _KV_W:, :]], axis=1).astype(BF16)
    row = lax.broadcasted_iota(jnp.int32, (keys, 1), 0)
    key_seq = jnp.where(row < ncache, row // WINDOW, (row - ncache) // t)
    lane = lax.broadcasted_iota(jnp.int32, (1, GROUP * _LANES), 1)
    valid = key_seq == (lane % _LANES) // t
    for j in range(N_KV_HEADS):
        s = jnp.where(valid, _dot(kfull, _group_queries(qkv_ref, j, slice(None))), NEG)
        o = _softmax_pv(s, _sink_row(sinks_ref, j, _LANES), vt[j * HEAD_DIM:(j + 1) * HEAD_DIM])
        for g in range(GROUP):
            yat_ref[(j * GROUP + g) * HEAD_DIM:(j * GROUP + g + 1) * HEAD_DIM, :] = (
                o[:, g * _LANES:(g + 1) * _LANES])
    ya_ref[...] = yat_ref[...].T.astype(ya_ref.dtype)


def _attn_sample(qkv_t, cache_k, cache_v, sinks, *, nseq, t, layer):
    n = nseq * t
    nq = N_HEADS * HEAD_DIM
    sb = _LANES // t
    prev = pl.BlockSpec((None, sb * WINDOW, _KV_W), lambda b: (layer, b, 0))
    return pl.pallas_call(
        functools.partial(_attn_sample_kernel, t=t),
        grid=(nseq // sb,),
        in_specs=[
            pl.BlockSpec((_QKV_ROWS, _LANES), lambda b: (0, b)),
            prev, prev,
            pl.BlockSpec(memory_space=pltpu.SMEM),
        ],
        out_specs=pl.BlockSpec((_LANES, nq), lambda b: (b, 0)),
        out_shape=jax.ShapeDtypeStruct((n, nq), BF16),
        scratch_shapes=[pltpu.VMEM((nq, _LANES), F32)],
        compiler_params=_cparams(("parallel",)),
        name="attn_sample",
    )(qkv_t, cache_k, cache_v, sinks)


def _qkv_t_kernel(x_ref, g_ref, w_ref, gc_ref, o_ref):
    h = _rms(x_ref[...], g_ref[...]).astype(BF16)
    acc = _dot_nt(w_ref[...], h)
    nqk = gc_ref.shape[0]
    for r in range(0, nqk, HEAD_DIM):
        blk = acc[r:r + HEAD_DIM]
        ms = jnp.sum(blk * blk, axis=0, keepdims=True) * (1.0 / HEAD_DIM)
        o_ref[r:r + HEAD_DIM, :] = blk * lax.rsqrt(ms + EPS) * gc_ref[r:r + HEAD_DIM, :]
    o_ref[nqk:, :] = acc[nqk:]


def _qkv_t(x, g, w_t, gain_col, tm, layer):
    n, d = x.shape
    rows = w_t.shape[1]
    return pl.pallas_call(
        _qkv_t_kernel,
        grid=(n // tm,),
        in_specs=[
            pl.BlockSpec((tm, d), lambda i: (i, 0)),
            _layer_resident(g, layer),
            _layer_resident(w_t, layer),
            _layer_resident(gain_col, layer),
        ],
        out_specs=pl.BlockSpec((rows, tm), lambda i: (0, i)),
        out_shape=jax.ShapeDtypeStruct((rows, n), F32),
        compiler_params=_cparams(("parallel",)),
        name="qkv_t",
    )(x, g, w_t, gain_col)


_PAIR = 2 * CHUNK
_KEYS = _PAIR + WINDOW


def _attn_prompt_kernel(q_ref, kc_ref, vc_ref, kp_ref, vp_ref, sinks_ref, ya_ref, yat_ref):
    tq = q_ref.shape[1]
    t = pl.program_id(1)
    lane = lax.broadcasted_iota(jnp.int32, (1, GROUP * _PAIR), 1)
    first_chunk = (lane % _PAIR) < CHUNK
    k_cur = kc_ref[...].T.astype(BF16)
    k_prev = kp_ref[...].T.astype(BF16)
    for p in range(tq // _PAIR):
        lo = p * _PAIR
        if p == 0:
            kb = jnp.concatenate([k_prev, k_cur[:_PAIR]], axis=0)
        else:
            kb = k_cur[lo - WINDOW:lo + _PAIR]
        first_key_chunk = t * (tq // CHUNK) + (lo - WINDOW) // CHUNK
        for j in range(N_KV_HEADS):
            rows = slice(j * HEAD_DIM, (j + 1) * HEAD_DIM)
            if p == 0:
                vb = jnp.concatenate([vp_ref[rows, :], vc_ref[rows, :_PAIR]], axis=1)
            else:
                vb = vc_ref[rows, lo - WINDOW:lo + _PAIR]
            s = _dot(kb, _group_queries(q_ref, j, slice(lo, lo + _PAIR)))
            s = jnp.concatenate([
                jnp.where(first_chunk & (first_key_chunk >= 0), s[:CHUNK], NEG),
                jnp.where(first_key_chunk + 1 >= 0, s[CHUNK:2 * CHUNK], NEG),
                s[2 * CHUNK:3 * CHUNK],
                jnp.where(first_chunk, NEG, s[3 * CHUNK:]),
            ], axis=0)
            o = _softmax_pv(s, _sink_row(sinks_ref, j, _PAIR), vb.astype(BF16))
            for g in range(GROUP):
                yat_ref[(j * GROUP + g) * HEAD_DIM:(j * GROUP + g + 1) * HEAD_DIM, lo:lo + _PAIR] = (
                    o[:, g * _PAIR:(g + 1) * _PAIR])
    ya_ref[...] = yat_ref[...].T.astype(ya_ref.dtype)


def _attn_prompt(qkv_t, sinks, *, nseq, t, tq):
    n = nseq * t
    nt = t // tq
    nq = N_HEADS * HEAD_DIM
    kblk, vblk = nq // _KV_W, nq // _KV_W + 1

    def prev_map(blk):
        return lambda b, i: (blk, jnp.maximum(b * (t // WINDOW) + i * (tq // WINDOW) - 1, 0))

    return pl.pallas_call(
        _attn_prompt_kernel,
        grid=(nseq, nt),
        in_specs=[
            pl.BlockSpec((nq, tq), lambda b, i: (0, b * nt + i)),
            pl.BlockSpec((_KV_W, tq), lambda b, i: (kblk, b * nt + i)),
            pl.BlockSpec((_KV_W, tq), lambda b, i: (vblk, b * nt + i)),
            pl.BlockSpec((_KV_W, WINDOW), prev_map(kblk)),
            pl.BlockSpec((_KV_W, WINDOW), prev_map(vblk)),
            pl.BlockSpec(memory_space=pltpu.SMEM),
        ],
        out_specs=pl.BlockSpec((tq, nq), lambda b, i: (b * nt + i, 0)),
        out_shape=jax.ShapeDtypeStruct((n, nq), BF16),
        scratch_shapes=[pltpu.VMEM((nq, tq), F32)],
        compiler_params=_cparams(("parallel", "arbitrary")),
        name="attn_prompt",
    )(qkv_t, qkv_t, qkv_t, qkv_t, qkv_t, sinks)


def _hgrn_levels(c):
    return [c >> (i + 1) for i in range(int(np.log2(c)))]


_SUBLANES = 8


def _level_operands(q, k, g, s):
    rows = q.shape[0]
    if s < _SUBLANES:
        upper = (lax.broadcasted_iota(jnp.int32, (rows, 1), 0) & s) != 0
        return jnp.where(upper, q * g, 0.0), jnp.where(upper, 0.0, k * g)
    zero = jnp.zeros((s, q.shape[1]), F32)
    qparts, kparts = [], []
    for g0 in range(0, rows, 2 * s):
        lo, up = slice(g0, g0 + s), slice(g0 + s, g0 + 2 * s)
        kparts += [k[lo] * g[lo], zero]
        qparts += [zero, q[up] * g[up]]
    return jnp.concatenate(qparts, axis=0), jnp.concatenate(kparts, axis=0)


def _hgrn_kernel(hq_ref, hf_ref, hi_ref, hg_ref, lbl_ref, on_ref, s0_ref, yb_ref, sout_ref,
                 a_ref, a2_ref, m_ref, st_ref, *, layer, cs):
    c = pl.program_id(1)
    rows = hq_ref.shape[0]
    nsq = rows // cs
    levels = _hgrn_levels(cs)

    @pl.when(c == 0)
    def _():
        row = lax.broadcasted_iota(jnp.int32, (rows, rows), 0)
        col = lax.broadcasted_iota(jnp.int32, (rows, rows), 1)
        mats = [((col <= row) & ((row & ~(cs - 1)) == (col & ~(cs - 1)))).astype(F32)]
        for i, s in enumerate(levels):
            mid = (row & ~(2 * s - 1)) + s
            upper = (row & s) != 0
            up = ((col >= mid) & (col <= row)).astype(F32)
            lo = ((col > row) & (col < mid)).astype(F32)
            mats.append(jnp.where(upper, up, lo))
            same = (row & ~(2 * s - 1)) == (col & ~(2 * s - 1))
            m_ref[i] = same.astype(F32)
        for i, a in enumerate(mats):
            a = a.astype(BF16)
            a_ref[i] = a
            a2_ref[i] = jnp.concatenate([a, a], axis=1)
        for sq in range(nsq):
            for h in range(HG_HEADS):
                st_ref[sq * HG_HEADS + h] = s0_ref[sq, h].T

    lg = lbl_ref[...]
    e = jnp.exp(lg - jnp.max(lg, axis=0, keepdims=True))
    lb = jnp.zeros((1, e.shape[1]), F32)
    for i in range(1, layer + 1):
        lb = lb + e[i:i + 1]
    lb = lb / jnp.sum(e, axis=0, keepdims=True)

    sg = _sigmoid(hf_ref[...])
    f = lb + (1.0 - lb) * sg
    logf = jnp.maximum(jnp.log(jnp.maximum(f, 1e-26)), LOG_F_FLOOR)
    kk = (1.0 - lb) * (1.0 - sg)
    l1 = logf.astype(BF16)
    r1 = logf - l1.astype(F32)
    l2 = r1.astype(BF16)
    l3 = (r1 - l2.astype(F32)).astype(BF16)
    l12 = jnp.concatenate([l1, l2], axis=0)

    def lin(i):
        return _dot(a2_ref[i], l12) + _dot(a_ref[i], l3)

    bc = lin(0)
    ex = [lin(i + 1) for i in range(len(levels))]
    on = on_ref[...]

    for h in range(HG_HEADS):
        sl = slice(h * HG_DK, (h + 1) * HG_DK)
        q = _silu(hq_ref[:, sl])
        v = hi_ref[:, sl]
        vb = v.astype(BF16)
        k = kk[:, sl]
        b = bc[:, sl]
        att = None
        for i, s in enumerate(levels):
            qs, ks = _level_operands(q, k, jnp.exp(ex[i][:, sl]), s)
            part = _dot_nt(qs.astype(BF16), ks.astype(BF16))
            if 2 * s < rows:
                part = part * m_ref[i]
            att = part if att is None else att + part
        o = _dot(att.astype(BF16), vb) + jnp.sum(q * k, axis=-1, keepdims=True) * v
        qd = (q * jnp.exp(b)).astype(BF16)
        inter = []
        for sq in range(nsq):
            r = slice(sq * cs, (sq + 1) * cs)
            st = st_ref[sq * HG_HEADS + h]
            inter.append(_dot_nt(qd[r], st.astype(BF16)))
            blast = b[(sq + 1) * cs - 1:(sq + 1) * cs, :]
            kd = (k[r] * jnp.exp(blast - b[r])).astype(BF16)
            st_ref[sq * HG_HEADS + h] = st * jnp.exp(blast) + _dot_tn(vb[r], kd)
        o = o + (inter[0] if nsq == 1 else jnp.concatenate(inter, axis=0))
        yb_ref[:, sl] = (_rms(o, on) * _silu(hg_ref[:, sl])).astype(yb_ref.dtype)

    @pl.when(c == pl.num_programs(1) - 1)
    def _():
        for sq in range(nsq):
            for h in range(HG_HEADS):
                sout_ref[sq, h] = st_ref[sq * HG_HEADS + h].T


def _hgrn(p_all, lb_logits, o_norm, s0, *, nseq, t, cs, sb, layer, s0_layer):
    assert sb == 1 or cs == t
    n = nseq * t
    nc = t // cs
    rows = sb * cs
    nlev = len(_hgrn_levels(cs))

    def slab(col):
        return pl.BlockSpec((rows, _SLAB), lambda b, i: (b * nc + i, col))

    st_block = (sb, HG_HEADS, HG_DK, HG_DV)
    return pl.pallas_call(
        functools.partial(_hgrn_kernel, layer=layer, cs=cs),
        grid=(nseq // sb, nc),
        in_specs=[
            slab(_C_HQ), slab(_C_HF), slab(_C_HI), slab(_C_HG),
            _resident(lb_logits.shape),
            _layer_resident(o_norm, layer),
            pl.BlockSpec((None,) + st_block, lambda b, i: (s0_layer, b, 0, 0, 0)),
        ],
        out_specs=[pl.BlockSpec((rows, _SLAB), lambda b, i: (b * nc + i, 0)),
                   pl.BlockSpec(st_block, lambda b, i: (b, 0, 0, 0))],
        out_shape=[jax.ShapeDtypeStruct((n, _SLAB), BF16),
                   jax.ShapeDtypeStruct((nseq, HG_HEADS, HG_DK, HG_DV), F32)],
        scratch_shapes=[
            pltpu.VMEM((nlev + 1, rows, rows), BF16),
            pltpu.VMEM((nlev + 1, rows, 2 * rows), BF16),
            pltpu.VMEM((nlev, rows, rows), F32),
            pltpu.VMEM((sb * HG_HEADS, HG_DV, HG_DK), F32),
        ],
        compiler_params=_cparams(("parallel", "arbitrary")),
        name="hgrn_t%d" % cs,
    )(p_all, p_all, p_all, p_all, lb_logits, o_norm, s0)


_PAD = 8


def _lru_kernel(lx_ref, lg_ref, cw_ref, cb_ref, wa_ref, ba_ref, wx_ref, bx_ref, lam_ref,
                cbuf_ref, h0_ref, yc_ref, nbuf_ref, hlast_ref, xe_ref, a_ref, b_ref, hc_ref):
    t = pl.program_id(1)
    tt = lx_ref.shape[0]
    w = lx_ref.shape[1]
    hist = CONV_WIDTH - 1

    @pl.when(t == 0)
    def _():
        xe_ref[_PAD - hist:_PAD, :] = cbuf_ref[0]
        hc_ref[...] = h0_ref[0]

    @pl.when(t > 0)
    def _():
        xe_ref[_PAD - hist:_PAD, :] = xe_ref[_PAD + tt - hist:_PAD + tt, :]

    xe_ref[_PAD:_PAD + tt, :] = lx_ref[...]
    xc = xe_ref[_PAD - hist:_PAD - hist + tt, :] * cw_ref[0:1, :]
    for j in range(1, CONV_WIDTH):
        xc = xc + xe_ref[_PAD - hist + j:_PAD - hist + j + tt, :] * cw_ref[j:j + 1, :]
    xc = xc + cb_ref[...]

    ra, rx = [], []
    for g in range(w // MXU_DIM):
        xg = xc[:, g * MXU_DIM:(g + 1) * MXU_DIM].astype(BF16)
        ra.append(_dot(xg, wa_ref[g]))
        rx.append(_dot(xg, wx_ref[g]))
    r = _sigmoid(jnp.concatenate(ra, axis=1) + ba_ref[...])
    ig = _sigmoid(jnp.concatenate(rx, axis=1) + bx_ref[...])
    nl = -lam_ref[...]
    softplus = jnp.maximum(nl, 0.0) + jnp.log1p(jnp.exp(-jnp.abs(nl)))
    log_a = -LRU_C * r * softplus
    a = jnp.exp(log_a)
    a_ref[...] = a
    b_ref[...] = jnp.sqrt(-jnp.tanh(log_a) * (a * a + 1.0)) * (ig * xc)

    def step(i, h):
        h = a_ref[pl.ds(i, 1), :] * h + b_ref[pl.ds(i, 1), :]
        b_ref[pl.ds(i, 1), :] = h
        return h

    h = lax.fori_loop(0, tt, step, hc_ref[...], unroll=8)
    hc_ref[...] = h
    lgv = lg_ref[...]
    gelu = 0.5 * lgv * (1.0 + jnp.tanh(np.sqrt(2.0 / np.pi) * (lgv + 0.044715 * (lgv * lgv * lgv))))
    yc_ref[...] = (b_ref[...] * gelu).astype(yc_ref.dtype)

    @pl.when(t == pl.num_programs(1) - 1)
    def _():
        nbuf_ref[0] = xe_ref[_PAD + tt - hist:_PAD + tt, :]
        hlast_ref[0] = h


def _lru(p_all, params, cbuf, h0, *, nseq, t, tt, layer, state_layer):
    n = nseq * t
    nt = t // tt
    w = _SLAB
    hist = CONV_WIDTH - 1
    return pl.pallas_call(
        _lru_kernel,
        grid=(nseq, nt),
        in_specs=[
            pl.BlockSpec((tt, w), lambda b, i: (b * nt + i, _C_LX)),
            pl.BlockSpec((tt, w), lambda b, i: (b * nt + i, _C_LG)),
            *[_layer_resident(p, layer) for p in params],
            pl.BlockSpec((None, 1, hist, w), lambda b, i: (state_layer, b, 0, 0)),
            pl.BlockSpec((None, 1, 1, w), lambda b, i: (state_layer, b, 0, 0)),
        ],
        out_specs=[
            pl.BlockSpec((tt, w), lambda b, i: (b * nt + i, 0)),
            pl.BlockSpec((1, hist, w), lambda b, i: (b, 0, 0)),
            pl.BlockSpec((1, 1, w), lambda b, i: (b, 0, 0)),
        ],
        out_shape=[jax.ShapeDtypeStruct((n, w), BF16),
                   jax.ShapeDtypeStruct((nseq, hist, w), F32),
                   jax.ShapeDtypeStruct((nseq, 1, w), F32)],
        scratch_shapes=[
            pltpu.VMEM((_PAD + tt, w), F32),
            pltpu.VMEM((tt, w), F32),
            pltpu.VMEM((tt, w), F32),
            pltpu.VMEM((1, w), F32),
        ],
        compiler_params=_cparams(("parallel", "arbitrary")),
        name="lru_t%d" % tt,
    )(p_all, p_all, *params, cbuf, h0)


def _merge_kernel(x_ref, ya_ref, yb_ref, yc_ref, ga_ref, gb_ref, gc_ref,
                  wa_ref, wb_ref, wc_ref, wo_ref, o_ref):
    m = _sigmoid(ga_ref[...]) * _dot(ya_ref[...], wa_ref[...])
    m = m + _sigmoid(gb_ref[...]) * _dot(yb_ref[...], wb_ref[...])
    m = m + _sigmoid(gc_ref[...]) * _dot(yc_ref[...], wc_ref[...])
    o_ref[...] = x_ref[...] + _dot(m.astype(BF16), wo_ref[...])


def _merge(x, ya, yb, yc, p_all, weights, tm, layer):
    n, d = x.shape
    row = pl.BlockSpec((tm, d), lambda i: (i, 0))

    def slab(col):
        return pl.BlockSpec((tm, _SLAB), lambda i: (i, col))

    return pl.pallas_call(
        _merge_kernel,
        grid=(n // tm,),
        in_specs=[row, row, row, row, slab(_C_GA), slab(_C_GB), slab(_C_GC),
                  *[_layer_resident(wt, layer) for wt in weights]],
        out_specs=row,
        out_shape=jax.ShapeDtypeStruct((n, d), F32),
        compiler_params=_cparams(("parallel",)),
        name="merge",
    )(x, ya, yb, yc, p_all, p_all, p_all, *weights)


def _block_diag(w):
    nl, nb, bw, _ = w.shape
    per = MXU_DIM // bw
    wg = w.reshape(nl, nb // per, per, bw, bw)
    eye = jnp.eye(per, dtype=w.dtype)
    out = jnp.einsum('lgpij,pq->lgpiqj', wg, eye)
    return out.reshape(nl, nb // per, MXU_DIM, MXU_DIM).astype(BF16)


def _prepare_params(w):
    def vec(name):
        return w[name][:, None, :]

    def bf16(name):
        return w[name].astype(BF16)

    w_in = w['w_in']
    qk_gain = jnp.concatenate([jnp.tile(w['q_norm'] * SCALE, (1, N_HEADS)),
                               jnp.tile(w['k_norm'], (1, N_KV_HEADS))], axis=1)
    return {
        'norm_ffn1': vec('norm_ffn1'), 'norm_mix': vec('norm_mix'), 'norm_ffn2': vec('norm_ffn2'),
        'ffn1': (bf16('w_ffn1_up'), bf16('w_ffn1_down')),
        'ffn2': (bf16('w_ffn2_up'), bf16('w_ffn2_down')),
        'w_mix': w_in[:, :, _QKV_ROWS:].astype(BF16),
        'w_qkv_t': jnp.swapaxes(w_in[:, :, :_QKV_ROWS], 1, 2).astype(BF16),
        'qk_gain_col': qk_gain[:, :, None],
        'attn_sinks': w['attn_sinks'],
        'merge': (bf16('w_attn_o'), bf16('w_hgrn_o'), bf16('w_lru_o'), bf16('w_out')),
        'hgrn_lb_logits': w['hgrn_lb_logits'], 'hgrn_o_norm': vec('hgrn_o_norm'),
        'lru': (w['conv_w'], vec('conv_b'), _block_diag(w['lru_w_a']), vec('lru_b_a'),
                _block_diag(w['lru_w_x']), vec('lru_b_x'), vec('lru_lambda')),
    }


def _tiles(n, prefer):
    for tm in prefer:
        if n % tm == 0:
            return tm
    return n


def _layer(x, pp, cache, *, nseq, t, layer, prompt):
    n, d = x.shape
    tm = _tiles(n, (512, 256, 128))
    x = _ffn(x, pp['norm_ffn1'], *pp['ffn1'], tm, layer)
    ck, cv, s0, cbuf, h0 = cache
    state_layer = 0 if prompt else layer
    p_all = _proj(x, pp['norm_mix'], pp['w_mix'], _tiles(n, (1024, 512, 256, 128)), 1536, layer)
    qkv_t = _qkv_t(x, pp['norm_mix'], pp['w_qkv_t'], pp['qk_gain_col'], tm, layer)
    sinks = pp['attn_sinks'][layer]
    nq = N_HEADS * HEAD_DIM
    if prompt:
        ya = _attn_prompt(qkv_t, sinks, nseq=nseq, t=t, tq=_tiles(t, (256, 128)))
        last = jnp.stack([qkv_t[nq:, (b + 1) * t - WINDOW:(b + 1) * t] for b in range(nseq)])
        cs = _tiles(t, (128, 64, 32, 16))
        tt = _tiles(t, (512, 256, 128, 64, 32, 16))
        sb = 1
    else:
        ya = _attn_sample(qkv_t, ck, cv, sinks, nseq=nseq, t=t, layer=layer)
        last = qkv_t[nq:, :].reshape(2 * _KV_W, nseq, t).transpose(1, 0, 2)
        cs = tt = t
        sb = _tiles(nseq, (4, 2, 1))
    last = last.reshape(nseq, 2, N_KV_HEADS, HEAD_DIM, -1).transpose(1, 0, 4, 2, 3)
    yb, s_new = _hgrn(p_all, pp['hgrn_lb_logits'], pp['hgrn_o_norm'], s0, nseq=nseq, t=t, cs=cs,
                      sb=sb, layer=layer, s0_layer=state_layer)
    yc, nbuf, hlast = _lru(p_all, pp['lru'], cbuf, h0, nseq=nseq, t=t, tt=tt, layer=layer,
                           state_layer=state_layer)
    x = _merge(x, ya, yb, yc, p_all, pp['merge'], tm, layer)
    x = _ffn(x, pp['norm_ffn2'], *pp['ffn2'], tm, layer)
    return x, (last[0], last[1], s_new, nbuf, hlast.reshape(nseq, d))


def _forward(x_prompt, x_sample, cache_attn_k, cache_attn_v, state_hgrn, state_conv, state_lru, w):
    bsz, seq, d = x_prompt.shape
    dbsz, dseq, _ = x_sample.shape
    depth = w['w_in'].shape[0]
    win_rows = cache_attn_k.shape[2]
    assert win_rows == WINDOW, "the attention kernels assume a full cached window"
    pp = _prepare_params(w)
    xp = x_prompt.reshape(bsz * seq, d)
    xs = x_sample.reshape(dbsz * dseq, d)
    zero_cache = (None, None,
                  jnp.zeros((1, bsz, HG_HEADS, HG_DK, HG_DV), F32),
                  jnp.zeros((1, bsz, CONV_WIDTH - 1, d), F32),
                  jnp.zeros((1, bsz, 1, d), F32))
    cache = (cache_attn_k.reshape(depth, dbsz * win_rows, _KV_W),
             cache_attn_v.reshape(depth, dbsz * win_rows, _KV_W),
             state_hgrn, state_conv, state_lru[:, :, None, :])
    st_p, st_s = [], []
    for l in range(depth):
        xp, sp = _layer(xp, pp, zero_cache, nseq=bsz, t=seq, layer=l, prompt=True)
        xs, ss = _layer(xs, pp, cache, nseq=dbsz, t=dseq, layer=l, prompt=False)
        st_p.append(sp)
        st_s.append(ss)

    def stack(sts, i):
        return jnp.stack([s[i] for s in sts], axis=0)

    return (xp.reshape(bsz, seq, d), xs.reshape(dbsz, dseq, d),
            stack(st_p, 0), stack(st_p, 1), stack(st_p, 2), stack(st_p, 3), stack(st_p, 4),
            stack(st_s, 0), stack(st_s, 1), stack(st_s, 2), stack(st_s, 3), stack(st_s, 4))


def kernel(x_prompt, x_sample, cache_attn_k, cache_attn_v, state_hgrn, state_conv, state_lru,
           norm_ffn1, w_ffn1_up, w_ffn1_down, norm_mix, w_in, q_norm, k_norm, attn_sinks, w_attn_o,
           hgrn_lb_logits, hgrn_o_norm, w_hgrn_o, conv_w, conv_b, lru_w_a, lru_b_a, lru_w_x, lru_b_x,
           lru_lambda, w_lru_o, w_out, norm_ffn2, w_ffn2_up, w_ffn2_down):
    w = dict(norm_ffn1=norm_ffn1, w_ffn1_up=w_ffn1_up, w_ffn1_down=w_ffn1_down, norm_mix=norm_mix,
             w_in=w_in, q_norm=q_norm, k_norm=k_norm, attn_sinks=attn_sinks, w_attn_o=w_attn_o,
             hgrn_lb_logits=hgrn_lb_logits, hgrn_o_norm=hgrn_o_norm, w_hgrn_o=w_hgrn_o,
             conv_w=conv_w, conv_b=conv_b, lru_w_a=lru_w_a, lru_b_a=lru_b_a, lru_w_x=lru_w_x,
             lru_b_x=lru_b_x, lru_lambda=lru_lambda, w_lru_o=w_lru_o, w_out=w_out,
             norm_ffn2=norm_ffn2, w_ffn2_up=w_ffn2_up, w_ffn2_down=w_ffn2_down)
    return _forward(x_prompt, x_sample, cache_attn_k, cache_attn_v, state_hgrn, state_conv,
                    state_lru, w)
```

```python
import functools

import numpy as np
import jax
import jax.numpy as jnp
from jax import lax
from jax.experimental import pallas as pl
from jax.experimental.pallas import tpu as pltpu

F32 = jnp.float32
BF16 = jnp.bfloat16

CHUNK = 64
N_HEADS = 16
N_KV_HEADS = 4
HEAD_DIM = 64
GROUP = N_HEADS // N_KV_HEADS
WINDOW = 128
SCALE = HEAD_DIM ** -0.5
HG_HEADS = 8
HG_DK = 128
HG_DV = 128
LOG_F_FLOOR = -60.0
CONV_WIDTH = 4
LRU_C = 8.0
LRU_BW = 64
EPS = 1e-6
NEG = -1e30

MXU_DIM = 256
VMEM_LIMIT_BYTES = 56 * 1024 * 1024

_SLAB = 1024
(_C_HQ, _C_HF, _C_HI, _C_HG) = range(4)
_N_HGRN_SLABS, _N_LRU_SLABS, _N_GATE_SLABS = 4, 2, 3
_KV_W = N_KV_HEADS * HEAD_DIM
_QKV_ROWS = N_HEADS * HEAD_DIM + 2 * _KV_W


def _cparams(sem):
    return pltpu.CompilerParams(dimension_semantics=sem, vmem_limit_bytes=VMEM_LIMIT_BYTES)


def _rms(x, g):
    return x * lax.rsqrt(jnp.mean(x * x, axis=-1, keepdims=True) + EPS) * g


def _sigmoid(x):
    return 1.0 / (1.0 + jnp.exp(-x))


def _silu(x):
    return x * _sigmoid(x)


def _dot(a, b):
    return jnp.dot(a, b, preferred_element_type=F32)


def _dot_nt(a, b):
    return lax.dot_general(a, b, (((1,), (1,)), ((), ())), preferred_element_type=F32)


def _dot_tn(a, b):
    return lax.dot_general(a, b, (((0,), (0,)), ((), ())), preferred_element_type=F32)


def _ff_chunks(dff, width):
    assert dff % MXU_DIM == 0
    edges = list(range(0, dff, width)) + [dff]
    return list(zip(edges[:-1], edges[1:]))


def _ffn_kernel(x_ref, g_ref, wu_ref, wd_ref, o_ref, *, chunks):
    x = x_ref[...]
    dff = wd_ref.shape[0]
    h = _rms(x, g_ref[...]).astype(BF16)
    acc = None
    for lo, hi in chunks:
        gate = _dot(h, wu_ref[:, lo:hi])
        val = _dot(h, wu_ref[:, dff + lo:dff + hi])
        part = _dot((_silu(gate) * val).astype(BF16), wd_ref[lo:hi, :])
        acc = part if acc is None else acc + part
    o_ref[...] = x + 0.5 * acc


def _resident(shape):
    return pl.BlockSpec(shape, lambda *_: (0,) * len(shape), pipeline_mode=pl.Buffered(1))


def _layer_resident(stacked, layer):
    shape = stacked.shape[1:]
    return pl.BlockSpec((None,) + shape, lambda *_: (layer,) + (0,) * len(shape),
                        pipeline_mode=pl.Buffered(1))


def _ffn(x, g, w_up, w_down, tm, layer):
    n, d = x.shape
    dff = w_down.shape[1]
    return pl.pallas_call(
        functools.partial(_ffn_kernel, chunks=_ff_chunks(dff, 6 * MXU_DIM)),
        grid=(n // tm,),
        in_specs=[
            pl.BlockSpec((tm, d), lambda i: (i, 0)),
            _layer_resident(g, layer),
            _layer_resident(w_up, layer),
            _layer_resident(w_down, layer),
        ],
        out_specs=pl.BlockSpec((tm, d), lambda i: (i, 0)),
        out_shape=jax.ShapeDtypeStruct((n, d), F32),
        compiler_params=_cparams(("parallel",)),
        name="ffn",
    )(x, g, w_up, w_down)


def _proj_kernel(x_ref, g_ref, w_ref, o_ref, h_ref):
    @pl.when(pl.program_id(1) == 0)
    def _():
        h_ref[...] = _rms(x_ref[...], g_ref[...]).astype(BF16)

    tn = o_ref.shape[1]
    col = pl.multiple_of(pl.program_id(1) * tn, tn)
    o_ref[...] = _dot(h_ref[...], w_ref[:, pl.ds(col, tn)])


def _proj(x, g, w, tm, tn, layer):
    n, d = x.shape
    cols = w.shape[2]
    return pl.pallas_call(
        _proj_kernel,
        grid=(n // tm, cols // tn),
        in_specs=[
            pl.BlockSpec((tm, d), lambda i, j: (i, 0)),
            _layer_resident(g, layer),
            _layer_resident(w, layer),
        ],
        out_specs=pl.BlockSpec((tm, tn), lambda i, j: (i, j)),
        out_shape=jax.ShapeDtypeStruct((n, cols), F32),
        scratch_shapes=[pltpu.VMEM((tm, d), BF16)],
        compiler_params=_cparams(("parallel", "arbitrary")),
        name="proj",
    )(x, g, w)


_LANES = 128


def _softmax_pv(s, sk, vt):
    m = jnp.maximum(jnp.max(s, axis=0, keepdims=True), sk)
    e = jnp.exp(s - m)
    den = jnp.sum(e, axis=0, keepdims=True) + jnp.exp(sk - m)
    return _dot(vt, (e * (1.0 / den)).astype(BF16))


def _group_queries(q_ref, j, lanes):
    qcat = jnp.concatenate(
        [q_ref[(j * GROUP + g) * HEAD_DIM:(j * GROUP + g + 1) * HEAD_DIM, lanes]
         for g in range(GROUP)], axis=1).astype(BF16)
    zero = jnp.zeros_like(qcat)
    return jnp.concatenate([qcat if i == j else zero for i in range(N_KV_HEADS)], axis=0)


def _sink_row(sinks_ref, j, width):
    return jnp.concatenate(
        [jnp.full((1, width), sinks_ref[j * GROUP + g], F32) for g in range(GROUP)], axis=1)


def _attn_sample_kernel(qkv_ref, kp_ref, vp_ref, sinks_ref, ya_ref, yat_ref, *, t):
    nq = N_HEADS * HEAD_DIM
    ncache = kp_ref.shape[0]
    keys = ncache + _LANES
    kfull = jnp.concatenate([kp_ref[...], qkv_ref[nq:nq + _KV_W, :].T], axis=0).astype(BF16)
    vt = jnp.concatenate([vp_ref[...].T, qkv_ref[nq + _KV_W:, :]], axis=1).astype(BF16)
    row = lax.broadcasted_iota(jnp.int32, (keys, 1), 0)
    key_seq = jnp.where(row < ncache, row // WINDOW, (row - ncache) // t)
    lane = lax.broadcasted_iota(jnp.int32, (1, GROUP * _LANES), 1)
    valid = key_seq == (lane % _LANES) // t
    for j in range(N_KV_HEADS):
        s = jnp.where(valid, _dot(kfull, _group_queries(qkv_ref, j, slice(None))), NEG)
        o = _softmax_pv(s, _sink_row(sinks_ref, j, _LANES), vt[j * HEAD_DIM:(j + 1) * HEAD_DIM])
        for g in range(GROUP):
            yat_ref[(j * GROUP + g) * HEAD_DIM:(j * GROUP + g + 1) * HEAD_DIM, :] = (
                o[:, g * _LANES:(g + 1) * _LANES])
    ya_ref[...] = yat_ref[...].T.astype(ya_ref.dtype)


def _attn_sample(qkv_t, cache_k, cache_v, sinks, *, nseq, t, layer):
    n = nseq * t
    nq = N_HEADS * HEAD_DIM
    sb = _LANES // t
    prev = pl.BlockSpec((None, sb * WINDOW, _KV_W), lambda b: (layer, b, 0))
    return pl.pallas_call(
        functools.partial(_attn_sample_kernel, t=t),
        grid=(nseq // sb,),
        in_specs=[
            pl.BlockSpec((_QKV_ROWS, _LANES), lambda b: (0, b)),
            prev, prev,
            pl.BlockSpec(memory_space=pltpu.SMEM),
        ],
        out_specs=pl.BlockSpec((_LANES, nq), lambda b: (b, 0)),
        out_shape=jax.ShapeDtypeStruct((n, nq), BF16),
        scratch_shapes=[pltpu.VMEM((nq, _LANES), F32)],
        compiler_params=_cparams(("parallel",)),
        name="attn_sample",
    )(qkv_t, cache_k, cache_v, sinks)


def _qkv_t_kernel(x_ref, g_ref, w_ref, gc_ref, o_ref):
    h = _rms(x_ref[...], g_ref[...]).astype(BF16)
    acc = _dot_nt(w_ref[...], h)
    nqk = gc_ref.shape[0]
    for r in range(0, nqk, HEAD_DIM):
        blk = acc[r:r + HEAD_DIM]
        ms = jnp.sum(blk * blk, axis=0, keepdims=True) * (1.0 / HEAD_DIM)
        o_ref[r:r + HEAD_DIM, :] = blk * lax.rsqrt(ms + EPS) * gc_ref[r:r + HEAD_DIM, :]
    o_ref[nqk:, :] = acc[nqk:]


def _qkv_t(x, g, w_t, gain_col, tm, layer):
    n, d = x.shape
    rows = w_t.shape[1]
    return pl.pallas_call(
        _qkv_t_kernel,
        grid=(n // tm,),
        in_specs=[
            pl.BlockSpec((tm, d), lambda i: (i, 0)),
            _layer_resident(g, layer),
            _layer_resident(w_t, layer),
            _layer_resident(gain_col, layer),
        ],
        out_specs=pl.BlockSpec((rows, tm), lambda i: (0, i)),
        out_shape=jax.ShapeDtypeStruct((rows, n), F32),
        compiler_params=_cparams(("parallel",)),
        name="qkv_t",
    )(x, g, w_t, gain_col)


_PAIR = 2 * CHUNK
_KEYS = _PAIR + WINDOW


def _attn_prompt_kernel(q_ref, kc_ref, vc_ref, kp_ref, vp_ref, sinks_ref, ya_ref, yat_ref):
    tq = q_ref.shape[1]
    t = pl.program_id(1)
    lane = lax.broadcasted_iota(jnp.int32, (1, GROUP * _PAIR), 1)
    first_chunk = (lane % _PAIR) < CHUNK
    k_cur = kc_ref[...].T.astype(BF16)
    k_prev = kp_ref[...].T.astype(BF16)
    for p in range(tq // _PAIR):
        lo = p * _PAIR
        if p == 0:
            kb = jnp.concatenate([k_prev, k_cur[:_PAIR]], axis=0)
        else:
            kb = k_cur[lo - WINDOW:lo + _PAIR]
        first_key_chunk = t * (tq // CHUNK) + (lo - WINDOW) // CHUNK
        for j in range(N_KV_HEADS):
            rows = slice(j * HEAD_DIM, (j + 1) * HEAD_DIM)
            if p == 0:
                vb = jnp.concatenate([vp_ref[rows, :], vc_ref[rows, :_PAIR]], axis=1)
            else:
                vb = vc_ref[rows, lo - WINDOW:lo + _PAIR]
            s = _dot(kb, _group_queries(q_ref, j, slice(lo, lo + _PAIR)))
            s = jnp.concatenate([
                jnp.where(first_chunk & (first_key_chunk >= 0), s[:CHUNK], NEG),
                jnp.where(first_key_chunk + 1 >= 0, s[CHUNK:2 * CHUNK], NEG),
                s[2 * CHUNK:3 * CHUNK],
                jnp.where(first_chunk, NEG, s[3 * CHUNK:]),
            ], axis=0)
            o = _softmax_pv(s, _sink_row(sinks_ref, j, _PAIR), vb.astype(BF16))
            for g in range(GROUP):
                yat_ref[(j * GROUP + g) * HEAD_DIM:(j * GROUP + g + 1) * HEAD_DIM, lo:lo + _PAIR] = (
                    o[:, g * _PAIR:(g + 1) * _PAIR])
    ya_ref[...] = yat_ref[...].T.astype(ya_ref.dtype)


def _attn_prompt(qkv_t, sinks, *, nseq, t, tq):
    n = nseq * t
    nt = t // tq
    nq = N_HEADS * HEAD_DIM
    kblk, vblk = nq // _KV_W, nq // _KV_W + 1

    def prev_map(blk):
        return lambda b, i: (blk, jnp.maximum(b * (t // WINDOW) + i * (tq // WINDOW) - 1, 0))

    return pl.pallas_call(
        _attn_prompt_kernel,
        grid=(nseq, nt),
        in_specs=[
            pl.BlockSpec((nq, tq), lambda b, i: (0, b * nt + i)),
            pl.BlockSpec((_KV_W, tq), lambda b, i: (kblk, b * nt + i)),
            pl.BlockSpec((_KV_W, tq), lambda b, i: (vblk, b * nt + i)),
            pl.BlockSpec((_KV_W, WINDOW), prev_map(kblk)),
            pl.BlockSpec((_KV_W, WINDOW), prev_map(vblk)),
            pl.BlockSpec(memory_space=pltpu.SMEM),
        ],
        out_specs=pl.BlockSpec((tq, nq), lambda b, i: (b * nt + i, 0)),
        out_shape=jax.ShapeDtypeStruct((n, nq), BF16),
        scratch_shapes=[pltpu.VMEM((nq, tq), F32)],
        compiler_params=_cparams(("parallel", "arbitrary")),
        name="attn_prompt",
    )(qkv_t, qkv_t, qkv_t, qkv_t, qkv_t, sinks)


def _hgrn_levels(c):
    return [c >> (i + 1) for i in range(int(np.log2(c)))]


_SUBLANES = 8


def _level_operands(q, k, g, s):
    rows = q.shape[0]
    if s < _SUBLANES:
        upper = (lax.broadcasted_iota(jnp.int32, (rows, 1), 0) & s) != 0
        return jnp.where(upper, q * g, 0.0), jnp.where(upper, 0.0, k * g)
    zero = jnp.zeros((s, q.shape[1]), F32)
    qparts, kparts = [], []
    for g0 in range(0, rows, 2 * s):
        lo, up = slice(g0, g0 + s), slice(g0 + s, g0 + 2 * s)
        kparts += [k[lo] * g[lo], zero]
        qparts += [zero, q[up] * g[up]]
    return jnp.concatenate(qparts, axis=0), jnp.concatenate(kparts, axis=0)


def _hgrn_kernel(hq_ref, hf_ref, hi_ref, hg_ref, lbl_ref, on_ref, s0_ref, yb_ref, sout_ref,
                 a_ref, a2_ref, m_ref, st_ref, *, layer, cs):
    c = pl.program_id(1)
    rows = hq_ref.shape[0]
    nsq = rows // cs
    levels = _hgrn_levels(cs)

    @pl.when(c == 0)
    def _():
        row = lax.broadcasted_iota(jnp.int32, (rows, rows), 0)
        col = lax.broadcasted_iota(jnp.int32, (rows, rows), 1)
        mats = [((col <= row) & ((row & ~(cs - 1)) == (col & ~(cs - 1)))).astype(F32)]
        for i, s in enumerate(levels):
            mid = (row & ~(2 * s - 1)) + s
            upper = (row & s) != 0
            up = ((col >= mid) & (col <= row)).astype(F32)
            lo = ((col > row) & (col < mid)).astype(F32)
            mats.append(jnp.where(upper, up, lo))
            same = (row & ~(2 * s - 1)) == (col & ~(2 * s - 1))
            m_ref[i] = same.astype(F32)
        for i, a in enumerate(mats):
            a = a.astype(BF16)
            a_ref[i] = a
            a2_ref[i] = jnp.concatenate([a, a], axis=1)
        for sq in range(nsq):
            for h in range(HG_HEADS):
                st_ref[sq * HG_HEADS + h] = s0_ref[sq, h].T

    lg = lbl_ref[...]
    e = jnp.exp(lg - jnp.max(lg, axis=0, keepdims=True))
    lb = jnp.zeros((1, e.shape[1]), F32)
    for i in range(1, layer + 1):
        lb = lb + e[i:i + 1]
    lb = lb / jnp.sum(e, axis=0, keepdims=True)

    sg = _sigmoid(hf_ref[...])
    f = lb + (1.0 - lb) * sg
    logf = jnp.maximum(jnp.log(jnp.maximum(f, 1e-26)), LOG_F_FLOOR)
    kk = (1.0 - lb) * (1.0 - sg)
    l1 = logf.astype(BF16)
    r1 = logf - l1.astype(F32)
    l2 = r1.astype(BF16)
    l3 = (r1 - l2.astype(F32)).astype(BF16)
    l12 = jnp.concatenate([l1, l2], axis=0)

    def lin(i):
        return _dot(a2_ref[i], l12) + _dot(a_ref[i], l3)

    bc = lin(0)
    ex = [lin(i + 1) for i in range(len(levels))]
    on = on_ref[...]

    for h in range(HG_HEADS):
        sl = slice(h * HG_DK, (h + 1) * HG_DK)
        q = _silu(hq_ref[:, sl])
        v = hi_ref[:, sl]
        vb = v.astype(BF16)
        k = kk[:, sl]
        b = bc[:, sl]
        att = None
        for i, s in enumerate(levels):
            qs, ks = _level_operands(q, k, jnp.exp(ex[i][:, sl]), s)
            part = _dot_nt(qs.astype(BF16), ks.astype(BF16))
            if 2 * s < rows:
                part = part * m_ref[i]
            att = part if att is None else att + part
        o = _dot(att.astype(BF16), vb) + jnp.sum(q * k, axis=-1, keepdims=True) * v
        qd = (q * jnp.exp(b)).astype(BF16)
        inter = []
        for sq in range(nsq):
            r = slice(sq * cs, (sq + 1) * cs)
            st = st_ref[sq * HG_HEADS + h]
            inter.append(_dot_nt(qd[r], st.astype(BF16)))
            blast = b[(sq + 1) * cs - 1:(sq + 1) * cs, :]
            kd = (k[r] * jnp.exp(blast - b[r])).astype(BF16)
            st_ref[sq * HG_HEADS + h] = st * jnp.exp(blast) + _dot_tn(vb[r], kd)
        o = o + (inter[0] if nsq == 1 else jnp.concatenate(inter, axis=0))
        yb_ref[:, sl] = (_rms(o, on) * _silu(hg_ref[:, sl])).astype(yb_ref.dtype)

    @pl.when(c == pl.num_programs(1) - 1)
    def _():
        for sq in range(nsq):
            for h in range(HG_HEADS):
                sout_ref[sq, h] = st_ref[sq * HG_HEADS + h].T


def _hgrn(p_all, lb_logits, o_norm, s0, *, nseq, t, cs, sb, layer, s0_layer):
    assert sb == 1 or cs == t
    n = nseq * t
    nc = t // cs
    rows = sb * cs
    nlev = len(_hgrn_levels(cs))

    def slab(col):
        return pl.BlockSpec((rows, _SLAB), lambda b, i: (b * nc + i, col))

    st_block = (sb, HG_HEADS, HG_DK, HG_DV)
    return pl.pallas_call(
        functools.partial(_hgrn_kernel, layer=layer, cs=cs),
        grid=(nseq // sb, nc),
        in_specs=[
            slab(_C_HQ), slab(_C_HF), slab(_C_HI), slab(_C_HG),
            _resident(lb_logits.shape),
            _layer_resident(o_norm, layer),
            pl.BlockSpec((None,) + st_block, lambda b, i: (s0_layer, b, 0, 0, 0)),
        ],
        out_specs=[pl.BlockSpec((rows, _SLAB), lambda b, i: (b * nc + i, 0)),
                   pl.BlockSpec(st_block, lambda b, i: (b, 0, 0, 0))],
        out_shape=[jax.ShapeDtypeStruct((n, _SLAB), BF16),
                   jax.ShapeDtypeStruct((nseq, HG_HEADS, HG_DK, HG_DV), F32)],
        scratch_shapes=[
            pltpu.VMEM((nlev + 1, rows, rows), BF16),
            pltpu.VMEM((nlev + 1, rows, 2 * rows), BF16),
            pltpu.VMEM((nlev, rows, rows), F32),
            pltpu.VMEM((sb * HG_HEADS, HG_DV, HG_DK), F32),
        ],
        compiler_params=_cparams(("parallel", "arbitrary")),
        name="hgrn_t%d" % cs,
    )(p_all, p_all, p_all, p_all, lb_logits, o_norm, s0)


_PAD = 8


def _lru_kernel(x_ref, g_ref, win_ref, cw_ref, cb_ref, wa_ref, ba_ref, wx_ref, bx_ref, lam_ref,
                cbuf_ref, h0_ref, yc_ref, nbuf_ref, hlast_ref, xe_ref, a_ref, b_ref, hc_ref):
    t = pl.program_id(1)
    tt, w = x_ref.shape
    hist = CONV_WIDTH - 1
    hn = _rms(x_ref[...], g_ref[...]).astype(BF16)
    lx = _dot(hn, win_ref[:, :w])
    lgv = _dot(hn, win_ref[:, w:])

    @pl.when(t == 0)
    def _():
        xe_ref[_PAD - hist:_PAD, :] = cbuf_ref[0]
        hc_ref[...] = h0_ref[0]

    @pl.when(t > 0)
    def _():
        xe_ref[_PAD - hist:_PAD, :] = xe_ref[_PAD + tt - hist:_PAD + tt, :]

    xe_ref[_PAD:_PAD + tt, :] = lx
    xc = xe_ref[_PAD - hist:_PAD - hist + tt, :] * cw_ref[0:1, :]
    for j in range(1, CONV_WIDTH):
        xc = xc + xe_ref[_PAD - hist + j:_PAD - hist + j + tt, :] * cw_ref[j:j + 1, :]
    xc = xc + cb_ref[...]

    ra, rx = [], []
    for g in range(w // MXU_DIM):
        xg = xc[:, g * MXU_DIM:(g + 1) * MXU_DIM].astype(BF16)
        ra.append(_dot(xg, wa_ref[g]))
        rx.append(_dot(xg, wx_ref[g]))
    r = _sigmoid(jnp.concatenate(ra, axis=1) + ba_ref[...])
    ig = _sigmoid(jnp.concatenate(rx, axis=1) + bx_ref[...])
    nl = -lam_ref[...]
    softplus = jnp.maximum(nl, 0.0) + jnp.log1p(jnp.exp(-jnp.abs(nl)))
    log_a = -LRU_C * r * softplus
    a = jnp.exp(log_a)
    a_ref[...] = a
    u = -jnp.tanh(log_a) * (a * a + 1.0)
    b_ref[...] = jnp.where(u > 0.0, u * lax.rsqrt(u), 0.0) * (ig * xc)

    sub = lax.broadcasted_iota(jnp.int32, (_SUBLANES, 1), 0)

    def step(i, h):
        r0 = pl.multiple_of(i * _SUBLANES, _SUBLANES)
        am = a_ref[pl.ds(r0, _SUBLANES), :]
        bm = b_ref[pl.ds(r0, _SUBLANES), :]
        d = 1
        while d < _SUBLANES:
            seen = sub >= d
            a_prev = jnp.where(seen, pltpu.roll(am, d, 0), 1.0)
            b_prev = jnp.where(seen, pltpu.roll(bm, d, 0), 0.0)
            bm = am * b_prev + bm
            am = am * a_prev
            d *= 2
        hs = am * h + bm
        b_ref[pl.ds(r0, _SUBLANES), :] = hs
        return jnp.broadcast_to(hs[_SUBLANES - 1:, :], hs.shape)

    h = lax.fori_loop(0, tt // _SUBLANES, step, jnp.broadcast_to(hc_ref[...], (_SUBLANES, w)))
    h = h[:1, :]
    hc_ref[...] = h
    gelu =0.5 * lgv * (1.0 + jnp.tanh(np.sqrt(2.0 / np.pi) * (lgv + 0.044715 * (lgv * lgv * lgv))))
    yc_ref[...] = (b_ref[...] * gelu).astype(yc_ref.dtype)

    @pl.when(t == pl.num_programs(1) - 1)
    def _():
        nbuf_ref[0] = xe_ref[_PAD + tt - hist:_PAD + tt, :]
        hlast_ref[0] = h


def _lru(x, params, cbuf, h0, *, nseq, t, tt, layer, state_layer):
    n, w = x.shape
    nt = t // tt
    hist = CONV_WIDTH - 1
    return pl.pallas_call(
        _lru_kernel,
        grid=(nseq, nt),
        in_specs=[
            pl.BlockSpec((tt, w), lambda b, i: (b * nt + i, 0)),
            *[_layer_resident(p, layer) for p in params],
            pl.BlockSpec((None, 1, hist, w), lambda b, i: (state_layer, b, 0, 0)),
            pl.BlockSpec((None, 1, 1, w), lambda b, i: (state_layer, b, 0, 0)),
        ],
        out_specs=[
            pl.BlockSpec((tt, w), lambda b, i: (b * nt + i, 0)),
            pl.BlockSpec((1, hist, w), lambda b, i: (b, 0, 0)),
            pl.BlockSpec((1, 1, w), lambda b, i: (b, 0, 0)),
        ],
        out_shape=[jax.ShapeDtypeStruct((n, w), BF16),
                   jax.ShapeDtypeStruct((nseq, hist, w), F32),
                   jax.ShapeDtypeStruct((nseq, 1, w), F32)],
        scratch_shapes=[
            pltpu.VMEM((_PAD + tt, w), F32),
            pltpu.VMEM((tt, w), F32),
            pltpu.VMEM((tt, w), F32),
            pltpu.VMEM((1, w), F32),
        ],
        compiler_params=_cparams(("parallel", "arbitrary")),
        name="lru_t%d" % tt,
    )(x, *params, cbuf, h0)


def _merge_kernel(x_ref, ya_ref, yb_ref, yc_ref, g_ref, wg_ref,
                  wa_ref, wb_ref, wc_ref, wo_ref, o_ref):
    x = x_ref[...]
    d = x.shape[1]
    hn = _rms(x, g_ref[...]).astype(BF16)
    m = None
    for i, (y_ref, w_ref) in enumerate(((ya_ref, wa_ref), (yb_ref, wb_ref), (yc_ref, wc_ref))):
        gate = _sigmoid(_dot(hn, wg_ref[:, i * d:(i + 1) * d]))
        part = gate * _dot(y_ref[...], w_ref[...])
        m = part if m is None else m + part
    o_ref[...] = x + _dot(m.astype(BF16), wo_ref[...])


def _merge(x, ya, yb, yc, params, tm, layer):
    n, d = x.shape
    row = pl.BlockSpec((tm, d), lambda i: (i, 0))
    return pl.pallas_call(
        _merge_kernel,
        grid=(n // tm,),
        in_specs=[row, row, row, row, *[_layer_resident(p, layer) for p in params]],
        out_specs=row,
        out_shape=jax.ShapeDtypeStruct((n, d), F32),
        compiler_params=_cparams(("parallel",)),
        name="merge",
    )(x, ya, yb, yc, *params)


def _block_diag(w):
    nl, nb, bw, _ = w.shape
    per = MXU_DIM // bw
    wg = w.reshape(nl, nb // per, per, bw, bw)
    eye = jnp.eye(per, dtype=w.dtype)
    out = jnp.einsum('lgpij,pq->lgpiqj', wg, eye)
    return out.reshape(nl, nb // per, MXU_DIM, MXU_DIM).astype(BF16)


def _prepare_params(w):
    def vec(name):
        return w[name][:, None, :]

    def bf16(name):
        return w[name].astype(BF16)

    w_in = w['w_in']
    c_hgrn = _QKV_ROWS
    c_lru = c_hgrn + _N_HGRN_SLABS * _SLAB
    c_gate = c_lru + _N_LRU_SLABS * _SLAB
    assert c_gate + _N_GATE_SLABS * _SLAB == w_in.shape[2]
    qk_gain =jnp.concatenate([jnp.tile(w['q_norm'] * SCALE, (1, N_HEADS)),
                               jnp.tile(w['k_norm'], (1, N_KV_HEADS))], axis=1)
    return {
        'norm_ffn1': vec('norm_ffn1'), 'norm_mix': vec('norm_mix'), 'norm_ffn2': vec('norm_ffn2'),
        'ffn1': (bf16('w_ffn1_up'), bf16('w_ffn1_down')),
        'ffn2': (bf16('w_ffn2_up'), bf16('w_ffn2_down')),
        'w_qkv_t': jnp.swapaxes(w_in[:, :, :_QKV_ROWS], 1, 2).astype(BF16),
        'w_hgrn_in': w_in[:, :, c_hgrn:c_lru].astype(BF16),
        'qk_gain_col': qk_gain[:, :, None],
        'attn_sinks': w['attn_sinks'],
        'merge': (vec('norm_mix'), w_in[:, :, c_gate:].astype(BF16),
                  bf16('w_attn_o'), bf16('w_hgrn_o'), bf16('w_lru_o'), bf16('w_out')),
        'hgrn_lb_logits': w['hgrn_lb_logits'], 'hgrn_o_norm': vec('hgrn_o_norm'),
        'lru': (vec('norm_mix'), w_in[:, :, c_lru:c_gate].astype(BF16),
                w['conv_w'], vec('conv_b'), _block_diag(w['lru_w_a']), vec('lru_b_a'),
                _block_diag(w['lru_w_x']), vec('lru_b_x'), vec('lru_lambda')),
    }


def _tiles(n, prefer):
    for tm in prefer:
        if n % tm == 0:
            return tm
    return n


def _layer(x, pp, cache, *, nseq, t, layer, prompt):
    n, d = x.shape
    tm = _tiles(n, (512, 256, 128))
    x = _ffn(x, pp['norm_ffn1'], *pp['ffn1'], tm, layer)
    ck, cv, s0, cbuf, h0 = cache
    state_layer = 0 if prompt else layer
    p_all = _proj(x, pp['norm_mix'], pp['w_hgrn_in'], _tiles(n, (1024, 512, 256, 128)),
                  2 * _SLAB, layer)
    qkv_t = _qkv_t(x, pp['norm_mix'], pp['w_qkv_t'], pp['qk_gain_col'], tm, layer)
    sinks = pp['attn_sinks'][layer]
    nq = N_HEADS * HEAD_DIM
    if prompt:
        ya = _attn_prompt(qkv_t, sinks, nseq=nseq, t=t, tq=_tiles(t, (256, 128)))
        last = jnp.stack([qkv_t[nq:, (b + 1) * t - WINDOW:(b + 1) * t] for b in range(nseq)])
        cs = _tiles(t, (128, 64, 32, 16))
        tt = _tiles(t, (512, 256, 128, 64, 32, 16))
        sb = 1
    else:
        ya = _attn_sample(qkv_t, ck, cv, sinks, nseq=nseq, t=t, layer=layer)
        last = qkv_t[nq:, :].reshape(2 * _KV_W, nseq, t).transpose(1, 0, 2)
        cs = tt = t
        sb = _tiles(nseq, (4, 2, 1))
    last = last.reshape(nseq, 2, N_KV_HEADS, HEAD_DIM, -1).transpose(1, 0, 4, 2, 3)
    yb, s_new = _hgrn(p_all, pp['hgrn_lb_logits'], pp['hgrn_o_norm'], s0, nseq=nseq, t=t, cs=cs,
                      sb=sb, layer=layer, s0_layer=state_layer)
    yc, nbuf, hlast = _lru(x, pp['lru'], cbuf, h0, nseq=nseq, t=t, tt=tt, layer=layer,
                           state_layer=state_layer)
    x = _merge(x, ya, yb, yc, pp['merge'], tm, layer)
    x = _ffn(x, pp['norm_ffn2'], *pp['ffn2'], tm, layer)
    return x, (last[0], last[1], s_new, nbuf, hlast.reshape(nseq, d))


def _forward(x_prompt, x_sample, cache_attn_k, cache_attn_v, state_hgrn, state_conv, state_lru, w):
    bsz, seq, d = x_prompt.shape
    dbsz, dseq, _ = x_sample.shape
    depth = w['w_in'].shape[0]
    win_rows = cache_attn_k.shape[2]
    assert win_rows == WINDOW, "the attention kernels assume a full cached window"
    pp = _prepare_params(w)
    xp = x_prompt.reshape(bsz * seq, d)
    xs = x_sample.reshape(dbsz * dseq, d)
    zero_cache = (None, None,
                  jnp.zeros((1, bsz, HG_HEADS, HG_DK, HG_DV), F32),
                  jnp.zeros((1, bsz, CONV_WIDTH - 1, d), F32),
                  jnp.zeros((1, bsz, 1, d), F32))
    cache = (cache_attn_k.reshape(depth, dbsz * win_rows, _KV_W),
             cache_attn_v.reshape(depth, dbsz * win_rows, _KV_W),
             state_hgrn, state_conv, state_lru[:, :, None, :])
    st_p, st_s = [], []
    for l in range(depth):
        xp, sp = _layer(xp, pp, zero_cache, nseq=bsz, t=seq, layer=l, prompt=True)
        xs, ss = _layer(xs, pp, cache, nseq=dbsz, t=dseq, layer=l, prompt=False)
        st_p.append(sp)
        st_s.append(ss)

    def stack(sts, i):
        return jnp.stack([s[i] for s in sts], axis=0)

    return (xp.reshape(bsz, seq, d), xs.reshape(dbsz, dseq, d),
            stack(st_p, 0), stack(st_p, 1), stack(st_p, 2), stack(st_p, 3), stack(st_p, 4),
            stack(st_s, 0), stack(st_s, 1), stack(st_s, 2), stack(st_s, 3), stack(st_s, 4))


def kernel(x_prompt, x_sample, cache_attn_k, cache_attn_v, state_hgrn, state_conv, state_lru,
           norm_ffn1, w_ffn1_up, w_ffn1_down, norm_mix, w_in, q_norm, k_norm, attn_sinks, w_attn_o,
           hgrn_lb_logits, hgrn_o_norm, w_hgrn_o, conv_w, conv_b, lru_w_a, lru_b_a, lru_w_x, lru_b_x,
           lru_lambda, w_lru_o, w_out, norm_ffn2, w_ffn2_up, w_ffn2_down):
    w = dict(norm_ffn1=norm_ffn1, w_ffn1_up=w_ffn1_up, w_ffn1_down=w_ffn1_down, norm_mix=norm_mix,
             w_in=w_in, q_norm=q_norm, k_norm=k_norm, attn_sinks=attn_sinks, w_attn_o=w_attn_o,
             hgrn_lb_logits=hgrn_lb_logits, hgrn_o_norm=hgrn_o_norm, w_hgrn_o=w_hgrn_o,
             conv_w=conv_w, conv_b=conv_b, lru_w_a=lru_w_a, lru_b_a=lru_b_a, lru_w_x=lru_w_x,
             lru_b_x=lru_b_x, lru_lambda=lru_lambda, w_lru_o=w_lru_o, w_out=w_out,
             norm_ffn2=norm_ffn2, w_ffn2_up=w_ffn2_up, w_ffn2_down=w_ffn2_down)
    return _forward(x_prompt, x_sample, cache_attn_k, cache_attn_v, state_hgrn, state_conv,
                    state_lru, w)
```

```python
import functools

import numpy as np
import jax
import jax.numpy as jnp
from jax import lax
from jax.experimental import pallas as pl
from jax.experimental.pallas import tpu as pltpu

F32 = jnp.float32
BF16 = jnp.bfloat16

CHUNK = 64
N_HEADS = 16
N_KV_HEADS = 4
HEAD_DIM = 64
GROUP = N_HEADS // N_KV_HEADS
WINDOW = 128
SCALE = HEAD_DIM ** -0.5
HG_HEADS = 8
HG_DK = 128
HG_DV = 128
LOG_F_FLOOR = -60.0
CONV_WIDTH = 4
LRU_C = 8.0
LRU_BW = 64
EPS = 1e-6
NEG = -1e30

MXU_DIM = 256
VMEM_LIMIT_BYTES = 56 * 1024 * 1024

_SLAB = 1024
(_C_HQ, _C_HF, _C_HI, _C_HG) = range(4)
_N_HGRN_SLABS, _N_LRU_SLABS, _N_GATE_SLABS = 4, 2, 3
_KV_W = N_KV_HEADS * HEAD_DIM
_QKV_ROWS = N_HEADS * HEAD_DIM + 2 * _KV_W


def _cparams(sem):
    return pltpu.CompilerParams(dimension_semantics=sem, vmem_limit_bytes=VMEM_LIMIT_BYTES)


def _rms(x, g):
    return x * lax.rsqrt(jnp.mean(x * x, axis=-1, keepdims=True) + EPS) * g


def _sigmoid(x):
    return 1.0 / (1.0 + jnp.exp(-x))


def _silu(x):
    return x * _sigmoid(x)


def _dot(a, b):
    return jnp.dot(a, b, preferred_element_type=F32)


def _dot_nt(a, b):
    return lax.dot_general(a, b, (((1,), (1,)), ((), ())), preferred_element_type=F32)


def _dot_tn(a, b):
    return lax.dot_general(a, b, (((0,), (0,)), ((), ())), preferred_element_type=F32)


def _ff_chunks(dff, width):
    assert dff % MXU_DIM == 0
    edges = list(range(0, dff, width)) + [dff]
    return list(zip(edges[:-1], edges[1:]))


def _ffn_kernel(x_ref, g_ref, wu_ref, wd_ref, o_ref, *, chunks):
    x = x_ref[...]
    dff = wd_ref.shape[0]
    h = _rms(x, g_ref[...]).astype(BF16)
    acc = None
    for lo, hi in chunks:
        gate = _dot(h, wu_ref[:, lo:hi])
        val = _dot(h, wu_ref[:, dff + lo:dff + hi])
        part = _dot((_silu(gate) * val).astype(BF16), wd_ref[lo:hi, :])
        acc = part if acc is None else acc + part
    o_ref[...] = x + 0.5 * acc


def _resident(shape):
    return pl.BlockSpec(shape, lambda *_: (0,) * len(shape), pipeline_mode=pl.Buffered(1))


def _layer_resident(stacked, layer):
    shape = stacked.shape[1:]
    return pl.BlockSpec((None,) + shape, lambda *_: (layer,) + (0,) * len(shape),
                        pipeline_mode=pl.Buffered(1))


def _ffn(x, g, w_up, w_down, tm, layer):
    n, d = x.shape
    dff = w_down.shape[1]
    return pl.pallas_call(
        functools.partial(_ffn_kernel, chunks=_ff_chunks(dff, 6 * MXU_DIM)),
        grid=(n // tm,),
        in_specs=[
            pl.BlockSpec((tm, d), lambda i: (i, 0)),
            _layer_resident(g, layer),
            _layer_resident(w_up, layer),
            _layer_resident(w_down, layer),
        ],
        out_specs=pl.BlockSpec((tm, d), lambda i: (i, 0)),
        out_shape=jax.ShapeDtypeStruct((n, d), F32),
        compiler_params=_cparams(("parallel",)),
        name="ffn",
    )(x, g, w_up, w_down)


_LANES = 128


def _softmax_pv(s, sk, vt):
    m = jnp.maximum(jnp.max(s, axis=0, keepdims=True), sk)
    e = jnp.exp(s - m)
    den = jnp.sum(e, axis=0, keepdims=True) + jnp.exp(sk - m)
    return _dot(vt, (e * (1.0 / den)).astype(BF16))


def _group_queries(q_ref, j, lanes):
    qcat = jnp.concatenate(
        [q_ref[(j * GROUP + g) * HEAD_DIM:(j * GROUP + g + 1) * HEAD_DIM, lanes]
         for g in range(GROUP)], axis=1).astype(BF16)
    zero = jnp.zeros_like(qcat)
    return jnp.concatenate([qcat if i == j else zero for i in range(N_KV_HEADS)], axis=0)


def _sink_row(sinks_ref, j, width):
    return jnp.concatenate(
        [jnp.full((1, width), sinks_ref[j * GROUP + g], F32) for g in range(GROUP)], axis=1)


def _attn_sample_kernel(qkv_ref, kp_ref, vp_ref, sinks_ref, ya_ref, yat_ref, *, t):
    nq = N_HEADS * HEAD_DIM
    ncache = kp_ref.shape[0]
    keys = ncache + _LANES
    kfull = jnp.concatenate([kp_ref[...], qkv_ref[nq:nq + _KV_W, :].T], axis=0).astype(BF16)
    vt = jnp.concatenate([vp_ref[...].T, qkv_ref[nq + _KV_W:, :]], axis=1).astype(BF16)
    row = lax.broadcasted_iota(jnp.int32, (keys, 1), 0)
    key_seq = jnp.where(row < ncache, row // WINDOW, (row - ncache) // t)
    lane = lax.broadcasted_iota(jnp.int32, (1, GROUP * _LANES), 1)
    valid = key_seq == (lane % _LANES) // t
    for j in range(N_KV_HEADS):
        s = jnp.where(valid, _dot(kfull, _group_queries(qkv_ref, j, slice(None))), NEG)
        o = _softmax_pv(s, _sink_row(sinks_ref, j, _LANES), vt[j * HEAD_DIM:(j + 1) * HEAD_DIM])
        for g in range(GROUP):
            yat_ref[(j * GROUP + g) * HEAD_DIM:(j * GROUP + g + 1) * HEAD_DIM, :] = (
                o[:, g * _LANES:(g + 1) * _LANES])
    ya_ref[...] = yat_ref[...].T.astype(ya_ref.dtype)


def _attn_sample(qkv_t, cache_k, cache_v, sinks, *, nseq, t, layer):
    n = nseq * t
    nq = N_HEADS * HEAD_DIM
    sb = _LANES // t
    prev = pl.BlockSpec((None, sb * WINDOW, _KV_W), lambda b: (layer, b, 0))
    return pl.pallas_call(
        functools.partial(_attn_sample_kernel, t=t),
        grid=(nseq // sb,),
        in_specs=[
            pl.BlockSpec((_QKV_ROWS, _LANES), lambda b: (0, b)),
            prev, prev,
            pl.BlockSpec(memory_space=pltpu.SMEM),
        ],
        out_specs=pl.BlockSpec((_LANES, nq), lambda b: (b, 0)),
        out_shape=jax.ShapeDtypeStruct((n, nq), BF16),
        scratch_shapes=[pltpu.VMEM((nq, _LANES), F32)],
        compiler_params=_cparams(("parallel",)),
        name="attn_sample",
    )(qkv_t, cache_k, cache_v, sinks)


def _qkv_t_kernel(x_ref, g_ref, w_ref, gc_ref, o_ref):
    h = _rms(x_ref[...], g_ref[...]).astype(BF16)
    acc = _dot_nt(w_ref[...], h)
    nqk = gc_ref.shape[0]
    for r in range(0, nqk, HEAD_DIM):
        blk = acc[r:r + HEAD_DIM]
        ms = jnp.sum(blk * blk, axis=0, keepdims=True) * (1.0 / HEAD_DIM)
        o_ref[r:r + HEAD_DIM, :] = blk * lax.rsqrt(ms + EPS) * gc_ref[r:r + HEAD_DIM, :]
    o_ref[nqk:, :] = acc[nqk:]


def _qkv_t(x, g, w_t, gain_col, tm, layer):
    n, d = x.shape
    rows = w_t.shape[1]
    return pl.pallas_call(
        _qkv_t_kernel,
        grid=(n // tm,),
        in_specs=[
            pl.BlockSpec((tm, d), lambda i: (i, 0)),
            _layer_resident(g, layer),
            _layer_resident(w_t, layer),
            _layer_resident(gain_col, layer),
        ],
        out_specs=pl.BlockSpec((rows, tm), lambda i: (0, i)),
        out_shape=jax.ShapeDtypeStruct((rows, n), F32),
        compiler_params=_cparams(("parallel",)),
        name="qkv_t",
    )(x, g, w_t, gain_col)


_PAIR = 2 * CHUNK
_KEYS = _PAIR + WINDOW


def _attn_prompt_kernel(q_ref, kc_ref, vc_ref, kp_ref, vp_ref, sinks_ref, ya_ref, yat_ref):
    tq = q_ref.shape[1]
    t = pl.program_id(1)
    lane = lax.broadcasted_iota(jnp.int32, (1, GROUP * _PAIR), 1)
    first_chunk = (lane % _PAIR) < CHUNK
    k_cur = kc_ref[...].T.astype(BF16)
    k_prev = kp_ref[...].T.astype(BF16)
    for p in range(tq // _PAIR):
        lo = p * _PAIR
        if p == 0:
            kb = jnp.concatenate([k_prev, k_cur[:_PAIR]], axis=0)
        else:
            kb = k_cur[lo - WINDOW:lo + _PAIR]
        first_key_chunk = t * (tq // CHUNK) + (lo - WINDOW) // CHUNK
        for j in range(N_KV_HEADS):
            rows = slice(j * HEAD_DIM, (j + 1) * HEAD_DIM)
            if p == 0:
                vb = jnp.concatenate([vp_ref[rows, :], vc_ref[rows, :_PAIR]], axis=1)
            else:
                vb = vc_ref[rows, lo - WINDOW:lo + _PAIR]
            s = _dot(kb, _group_queries(q_ref, j, slice(lo, lo + _PAIR)))
            s = jnp.concatenate([
                jnp.where(first_chunk & (first_key_chunk >= 0), s[:CHUNK], NEG),
                jnp.where(first_key_chunk + 1 >= 0, s[CHUNK:2 * CHUNK], NEG),
                s[2 * CHUNK:3 * CHUNK],
                jnp.where(first_chunk, NEG, s[3 * CHUNK:]),
            ], axis=0)
            o = _softmax_pv(s, _sink_row(sinks_ref, j, _PAIR), vb.astype(BF16))
            for g in range(GROUP):
                yat_ref[(j * GROUP + g) * HEAD_DIM:(j * GROUP + g + 1) * HEAD_DIM, lo:lo + _PAIR] = (
                    o[:, g * _PAIR:(g + 1) * _PAIR])
    ya_ref[...] = yat_ref[...].T.astype(ya_ref.dtype)


def _attn_prompt(qkv_t, sinks, *, nseq, t, tq):
    n = nseq * t
    nt = t // tq
    nq = N_HEADS * HEAD_DIM
    kblk, vblk = nq // _KV_W, nq // _KV_W + 1

    def prev_map(blk):
        return lambda b, i: (blk, jnp.maximum(b * (t // WINDOW) + i * (tq // WINDOW) - 1, 0))

    return pl.pallas_call(
        _attn_prompt_kernel,
        grid=(nseq, nt),
        in_specs=[
            pl.BlockSpec((nq, tq), lambda b, i: (0, b * nt + i)),
            pl.BlockSpec((_KV_W, tq), lambda b, i: (kblk, b * nt + i)),
            pl.BlockSpec((_KV_W, tq), lambda b, i: (vblk, b * nt + i)),
            pl.BlockSpec((_KV_W, WINDOW), prev_map(kblk)),
            pl.BlockSpec((_KV_W, WINDOW), prev_map(vblk)),
            pl.BlockSpec(memory_space=pltpu.SMEM),
        ],
        out_specs=pl.BlockSpec((tq, nq), lambda b, i: (b * nt + i, 0)),
        out_shape=jax.ShapeDtypeStruct((n, nq), BF16),
        scratch_shapes=[pltpu.VMEM((nq, tq), F32)],
        compiler_params=_cparams(("parallel", "arbitrary")),
        name="attn_prompt",
    )(qkv_t, qkv_t, qkv_t, qkv_t, qkv_t, sinks)


def _hgrn_levels(c):
    return [c >> (i + 1) for i in range(int(np.log2(c)))]


_SUBLANES = 8


def _level_operands(q, k, g, s):
    rows = q.shape[0]
    if s < _SUBLANES:
        upper = (lax.broadcasted_iota(jnp.int32, (rows, 1), 0) & s) != 0
        return jnp.where(upper, q * g, 0.0), jnp.where(upper, 0.0, k * g)
    zero = jnp.zeros((s, q.shape[1]), F32)
    qparts, kparts = [], []
    for g0 in range(0, rows, 2 * s):
        lo, up = slice(g0, g0 + s), slice(g0 + s, g0 + 2 * s)
        kparts += [k[lo] * g[lo], zero]
        qparts += [zero, q[up] * g[up]]
    return jnp.concatenate(qparts, axis=0), jnp.concatenate(kparts, axis=0)


def _hgrn_kernel(hq_ref, hf_ref, hi_ref, hg_ref, lbl_ref, on_ref, s0_ref, yb_ref, sout_ref,
                 a_ref, m_ref, st_ref, *, layer, cs):
    c = pl.program_id(1)
    rows = hq_ref.shape[0]
    nsq = rows // cs
    levels = _hgrn_levels(cs)

    @pl.when(c == 0)
    def _():
        row = lax.broadcasted_iota(jnp.int32, (rows, rows), 0)
        col = lax.broadcasted_iota(jnp.int32, (rows, rows), 1)
        mats = [((col <= row) & ((row & ~(cs - 1)) == (col & ~(cs - 1)))).astype(F32)]
        for i, s in enumerate(levels):
            mid = (row & ~(2 * s - 1)) + s
            upper = (row & s) != 0
            up = ((col >= mid) & (col <= row)).astype(F32)
            lo = ((col > row) & (col < mid)).astype(F32)
            mats.append(jnp.where(upper, up, lo))
            same = (row & ~(2 * s - 1)) == (col & ~(2 * s - 1))
            m_ref[i] = same.astype(F32)
        for i, a in enumerate(mats):
            a = a.astype(BF16)
            a_ref[i] = jnp.concatenate([a, a, a], axis=1)
        for sq in range(nsq):
            for h in range(HG_HEADS):
                st_ref[sq * HG_HEADS + h] = s0_ref[sq, h].T

    lg = lbl_ref[...]
    e = jnp.exp(lg - jnp.max(lg, axis=0, keepdims=True))
    lb = jnp.zeros((1, e.shape[1]), F32)
    for i in range(1, layer + 1):
        lb = lb + e[i:i + 1]
    lb = lb / jnp.sum(e, axis=0, keepdims=True)

    sg = _sigmoid(hf_ref[...])
    f = lb + (1.0 - lb) * sg
    logf = jnp.maximum(jnp.log(jnp.maximum(f, 1e-26)), LOG_F_FLOOR)
    kk = (1.0 - lb) * (1.0 - sg)
    l1 = logf.astype(BF16)
    r1 = logf - l1.astype(F32)
    l2 = r1.astype(BF16)
    l3 = (r1 - l2.astype(F32)).astype(BF16)
    pieces = jnp.concatenate([l1, l2, l3], axis=0)

    def lin(i):
        return _dot(a_ref[i], pieces)

    bc = lin(0)
    ex = [lin(i + 1) for i in range(len(levels))]
    on = on_ref[...]

    for h in range(HG_HEADS):
        sl = slice(h * HG_DK, (h + 1) * HG_DK)
        q = _silu(hq_ref[:, sl])
        v = hi_ref[:, sl]
        vb = v.astype(BF16)
        k = kk[:, sl]
        b = bc[:, sl]
        att = None
        for i, s in enumerate(levels):
            qs, ks = _level_operands(q, k, jnp.exp(ex[i][:, sl]), s)
            part = _dot_nt(qs.astype(BF16), ks.astype(BF16))
            if 2 * s < rows:
                part = part * m_ref[i]
            att = part if att is None else att + part
        o = _dot(att.astype(BF16), vb) + jnp.sum(q * k, axis=-1, keepdims=True) * v
        qd = (q * jnp.exp(b)).astype(BF16)
        inter = []
        for sq in range(nsq):
            r = slice(sq * cs, (sq + 1) * cs)
            st = st_ref[sq * HG_HEADS + h]
            inter.append(_dot_nt(qd[r], st.astype(BF16)))
            blast = b[(sq + 1) * cs - 1:(sq + 1) * cs, :]
            kd = (k[r] * jnp.exp(blast - b[r])).astype(BF16)
            st_ref[sq * HG_HEADS + h] = st * jnp.exp(blast) + _dot_tn(vb[r], kd)
        o = o + (inter[0] if nsq == 1 else jnp.concatenate(inter, axis=0))
        yb_ref[:, sl] = (_rms(o, on) * _silu(hg_ref[:, sl])).astype(yb_ref.dtype)

    @pl.when(c == pl.num_programs(1) - 1)
    def _():
        for sq in range(nsq):
            for h in range(HG_HEADS):
                sout_ref[sq, h] = st_ref[sq * HG_HEADS + h].T


def _hgrn(p_all, lb_logits, o_norm, s0, *, nseq, t, cs, sb, layer, s0_layer):
    assert sb == 1 or cs == t
    n = nseq * t
    nc = t // cs
    rows = sb * cs
    nlev = len(_hgrn_levels(cs))

    def slab(col):
        return pl.BlockSpec((rows, _SLAB), lambda b, i: (b * nc + i, col))

    st_block = (sb, HG_HEADS, HG_DK, HG_DV)
    return pl.pallas_call(
        functools.partial(_hgrn_kernel, layer=layer, cs=cs),
        grid=(nseq // sb, nc),
        in_specs=[
            slab(_C_HQ), slab(_C_HF), slab(_C_HI), slab(_C_HG),
            _resident(lb_logits.shape),
            _layer_resident(o_norm, layer),
            pl.BlockSpec((None,) + st_block, lambda b, i: (s0_layer, b, 0, 0, 0)),
        ],
        out_specs=[pl.BlockSpec((rows, _SLAB), lambda b, i: (b * nc + i, 0)),
                   pl.BlockSpec(st_block, lambda b, i: (b, 0, 0, 0))],
        out_shape=[jax.ShapeDtypeStruct((n, _SLAB), BF16),
                   jax.ShapeDtypeStruct((nseq, HG_HEADS, HG_DK, HG_DV), F32)],
        scratch_shapes=[
            pltpu.VMEM((nlev + 1, rows, 3 * rows), BF16),
            pltpu.VMEM((nlev, rows, rows), F32),
            pltpu.VMEM((sb * HG_HEADS, HG_DV, HG_DK), F32),
        ],
        compiler_params=_cparams(("parallel", "arbitrary")),
        name="hgrn_t%d" % cs,
    )(p_all, p_all, p_all, p_all, lb_logits, o_norm, s0)


_PAD = 8


def _lru_kernel(x_ref, g_ref, win_ref, whg_ref, cw_ref, cb_ref, wa_ref, ba_ref, wx_ref, bx_ref,
                lam_ref, cbuf_ref, h0_ref, yc_ref, p_ref, nbuf_ref, hlast_ref,
                xe_ref, a_ref, b_ref, gl_ref, hc_ref):
    t = pl.program_id(1)
    rows, w = x_ref.shape
    nsq = xe_ref.shape[0]
    tt = rows // nsq
    hist = CONV_WIDTH - 1
    hn = _rms(x_ref[...], g_ref[...]).astype(BF16)

    @pl.when(t == 0)
    def _():
        for s in range(nsq):
            xe_ref[s, _PAD - hist:_PAD, :] = cbuf_ref[s]
            hc_ref[s] = h0_ref[s]

    @pl.when(t > 0)
    def _():
        for s in range(nsq):
            xe_ref[s, _PAD - hist:_PAD, :] = xe_ref[s, _PAD + tt - hist:_PAD + tt, :]

    ngroups = w // MXU_DIM
    pslab = p_ref.shape[1] // ngroups
    def project(g):
        return (_dot(hn, win_ref[:, g * MXU_DIM:(g + 1) * MXU_DIM]),
                _dot(hn, win_ref[:, w + g * MXU_DIM:w + (g + 1) * MXU_DIM]))

    nxt = project(0)
    for g in range(ngroups):
        cols = slice(g * MXU_DIM, (g + 1) * MXU_DIM)
        lx, lgv = nxt
        if g + 1 < ngroups:
            nxt = project(g + 1)
        xcs = []
        for s in range(nsq):
            xe_ref[s, _PAD:_PAD + tt, cols] = lx[s * tt:(s + 1) * tt]
            xc = xe_ref[s, _PAD - hist:_PAD - hist + tt, cols] * cw_ref[0:1, cols]
            for j in range(1, CONV_WIDTH):
                xc = xc + (xe_ref[s, _PAD - hist + j:_PAD - hist + j + tt, cols]
                           * cw_ref[j:j + 1, cols])
            xcs.append(xc)
        xc = (xcs[0] if nsq == 1 else jnp.concatenate(xcs, axis=0)) + cb_ref[:, cols]
        gl_ref[:, cols] = 0.5 * lgv * (
            1.0 + jnp.tanh(np.sqrt(2.0 / np.pi) * (lgv + 0.044715 * (lgv * lgv * lgv))))
        xg = xc.astype(BF16)
        ra = _dot(xg, wa_ref[g])
        rx = _dot(xg, wx_ref[g])
        p_ref[:, g * pslab:(g + 1) * pslab] = _dot(hn, whg_ref[:, g * pslab:(g + 1) * pslab])
        r = _sigmoid(ra + ba_ref[:, cols])
        ig = _sigmoid(rx + bx_ref[:, cols])
        nl = -lam_ref[:, cols]
        softplus = jnp.maximum(nl, 0.0) + jnp.log1p(jnp.exp(-jnp.abs(nl)))
        log_a = -LRU_C * r * softplus
        a = jnp.exp(log_a)
        a_ref[:, cols] = a
        u = -jnp.tanh(log_a) * (a * a + 1.0)
        b_ref[:, cols] = jnp.where(u > 0.0, u * lax.rsqrt(u), 0.0) * (ig * xc)

    sub = lax.broadcasted_iota(jnp.int32, (_SUBLANES, 1), 0)

    def step(i, h, base):
        r0 = pl.multiple_of(base + i * _SUBLANES, _SUBLANES)
        am = a_ref[pl.ds(r0, _SUBLANES), :]
        bm = b_ref[pl.ds(r0, _SUBLANES), :]
        d = 1
        while d < _SUBLANES:
            seen = sub >= d
            a_prev = jnp.where(seen, pltpu.roll(am, d, 0), 1.0)
            b_prev = jnp.where(seen, pltpu.roll(bm, d, 0), 0.0)
            bm = am * b_prev + bm
            am = am * a_prev
            d *= 2
        hs = am * h + bm
        b_ref[pl.ds(r0, _SUBLANES), :] = hs
        return jnp.broadcast_to(hs[_SUBLANES - 1:, :], hs.shape)

    trips = tt // _SUBLANES
    for s in range(nsq):
        h = lax.fori_loop(0, trips, functools.partial(step, base=s * tt),
                          jnp.broadcast_to(hc_ref[s], (_SUBLANES, w)), unroll=trips <= 4)
        hc_ref[s] = h[:1, :]
    yc_ref[...] = (b_ref[...] * gl_ref[...]).astype(yc_ref.dtype)

    @pl.when(t == pl.num_programs(1) - 1)
    def _():
        for s in range(nsq):
            nbuf_ref[s] = xe_ref[s, _PAD + tt - hist:_PAD + tt, :]
            hlast_ref[s] = hc_ref[s]


def _lru(x, params, cbuf, h0, *, nseq, t, tt, sb, layer, state_layer):
    assert sb == 1 or tt == t
    n, w = x.shape
    nt = t // tt
    rows = sb * tt
    hist = CONV_WIDTH - 1
    pcols = params[2].shape[2]
    return pl.pallas_call(
        _lru_kernel,
        grid=(nseq // sb, nt),
        in_specs=[
            pl.BlockSpec((rows, w), lambda b, i: (b * nt + i, 0)),
            *[_layer_resident(p, layer) for p in params],
            pl.BlockSpec((None, sb, hist, w), lambda b, i: (state_layer, b, 0, 0)),
            pl.BlockSpec((None, sb, 1, w), lambda b, i: (state_layer, b, 0, 0)),
        ],
        out_specs=[
            pl.BlockSpec((rows, w), lambda b, i: (b * nt + i, 0)),
            pl.BlockSpec((rows, pcols), lambda b, i: (b * nt + i, 0)),
            pl.BlockSpec((sb, hist, w), lambda b, i: (b, 0, 0)),
            pl.BlockSpec((sb, 1, w), lambda b, i: (b, 0, 0)),
        ],
        out_shape=[jax.ShapeDtypeStruct((n, w), BF16),
                   jax.ShapeDtypeStruct((n, pcols), F32),
                   jax.ShapeDtypeStruct((nseq, hist, w), F32),
                   jax.ShapeDtypeStruct((nseq, 1, w), F32)],
        scratch_shapes=[
            pltpu.VMEM((sb, _PAD + tt, w), F32),
            pltpu.VMEM((rows, w), F32),
            pltpu.VMEM((rows, w), F32),
            pltpu.VMEM((rows, w), F32),
            pltpu.VMEM((sb, 1, w), F32),
        ],
        compiler_params=_cparams(("parallel", "arbitrary")),
        name="lru_t%d" % tt,
    )(x, *params, cbuf, h0)


def _merge_kernel(x_ref, ya_ref, yb_ref, yc_ref, g_ref, wg_ref,
                  wa_ref, wb_ref, wc_ref, wo_ref, o_ref):
    x = x_ref[...]
    d = x.shape[1]
    hn = _rms(x, g_ref[...]).astype(BF16)
    m = None
    for i, (y_ref, w_ref) in enumerate(((ya_ref, wa_ref), (yb_ref, wb_ref), (yc_ref, wc_ref))):
        gate = _sigmoid(_dot(hn, wg_ref[:, i * d:(i + 1) * d]))
        part = gate * _dot(y_ref[...], w_ref[...])
        m = part if m is None else m + part
    o_ref[...] = x + _dot(m.astype(BF16), wo_ref[...])


def _merge(x, ya, yb, yc, params, tm, layer):
    n, d = x.shape
    row = pl.BlockSpec((tm, d), lambda i: (i, 0))
    return pl.pallas_call(
        _merge_kernel,
        grid=(n // tm,),
        in_specs=[row, row, row, row, *[_layer_resident(p, layer) for p in params]],
        out_specs=row,
        out_shape=jax.ShapeDtypeStruct((n, d), F32),
        compiler_params=_cparams(("parallel",)),
        name="merge",
    )(x, ya, yb, yc, *params)


def _block_diag(w):
    nl, nb, bw, _ = w.shape
    per = MXU_DIM // bw
    wg = w.reshape(nl, nb // per, per, bw, bw)
    eye = jnp.eye(per, dtype=w.dtype)
    out = jnp.einsum('lgpij,pq->lgpiqj', wg, eye)
    return out.reshape(nl, nb // per, MXU_DIM, MXU_DIM).astype(BF16)


def _prepare_params(w):
    def vec(name):
        return w[name][:, None, :]

    def bf16(name):
        return w[name].astype(BF16)

    w_in = w['w_in']
    c_hgrn = _QKV_ROWS
    c_lru = c_hgrn + _N_HGRN_SLABS * _SLAB
    c_gate = c_lru + _N_LRU_SLABS * _SLAB
    assert c_gate + _N_GATE_SLABS * _SLAB == w_in.shape[2]
    qk_gain =jnp.concatenate([jnp.tile(w['q_norm'] * SCALE, (1, N_HEADS)),
                               jnp.tile(w['k_norm'], (1, N_KV_HEADS))], axis=1)
    return {
        'norm_ffn1': vec('norm_ffn1'), 'norm_mix': vec('norm_mix'), 'norm_ffn2': vec('norm_ffn2'),
        'ffn1': (bf16('w_ffn1_up'), bf16('w_ffn1_down')),
        'ffn2': (bf16('w_ffn2_up'), bf16('w_ffn2_down')),
        'w_qkv_t': jnp.swapaxes(w_in[:, :, :_QKV_ROWS], 1, 2).astype(BF16),
        'qk_gain_col': qk_gain[:, :, None],
        'attn_sinks': w['attn_sinks'],
        'merge': (vec('norm_mix'), w_in[:, :, c_gate:].astype(BF16),
                  bf16('w_attn_o'), bf16('w_hgrn_o'), bf16('w_lru_o'), bf16('w_out')),
        'hgrn_lb_logits': w['hgrn_lb_logits'], 'hgrn_o_norm': vec('hgrn_o_norm'),
        'lru': (vec('norm_mix'), w_in[:, :, c_lru:c_gate].astype(BF16),
                w_in[:, :, c_hgrn:c_lru].astype(BF16), w['conv_w'], vec('conv_b'), _block_diag(w['lru_w_a']), vec('lru_b_a'),
                _block_diag(w['lru_w_x']), vec('lru_b_x'), vec('lru_lambda')),
    }


def _tiles(n, prefer):
    for tm in prefer:
        if n % tm == 0:
            return tm
    return n


def _layer(x, pp, cache, *, nseq, t, layer, prompt):
    n, d = x.shape
    tm = _tiles(n, (512, 256, 128))
    x = _ffn(x, pp['norm_ffn1'], *pp['ffn1'], tm, layer)
    ck, cv, s0, cbuf, h0 = cache
    state_layer = 0 if prompt else layer
    qkv_t =_qkv_t(x, pp['norm_mix'], pp['w_qkv_t'], pp['qk_gain_col'], tm, layer)
    sinks = pp['attn_sinks'][layer]
    nq = N_HEADS * HEAD_DIM
    if prompt:
        ya = _attn_prompt(qkv_t, sinks, nseq=nseq, t=t, tq=_tiles(t, (256, 128)))
        last = jnp.stack([qkv_t[nq:, (b + 1) * t - WINDOW:(b + 1) * t] for b in range(nseq)])
        cs = _tiles(t, (128, 64, 32, 16))
        tt = _tiles(t, (512, 256, 128, 64, 32, 16))
        sb = 1
    else:
        ya = _attn_sample(qkv_t, ck, cv, sinks, nseq=nseq, t=t, layer=layer)
        last = qkv_t[nq:, :].reshape(2 * _KV_W, nseq, t).transpose(1, 0, 2)
        cs = tt = t
        sb = _tiles(nseq, (4, 2, 1))
    last = last.reshape(nseq, 2, N_KV_HEADS, HEAD_DIM, -1).transpose(1, 0, 4, 2, 3)
    yc, p_all, nbuf, hlast = _lru(x, pp['lru'], cbuf, h0, nseq=nseq, t=t, tt=tt,
                                  sb=1 if prompt else _tiles(nseq, (8, 4, 2, 1)), layer=layer,
                                  state_layer=state_layer)
    yb, s_new = _hgrn(p_all, pp['hgrn_lb_logits'], pp['hgrn_o_norm'], s0, nseq=nseq, t=t, cs=cs,
                      sb=sb, layer=layer, s0_layer=state_layer)
    x = _merge(x, ya, yb, yc, pp['merge'], tm, layer)
    x = _ffn(x, pp['norm_ffn2'], *pp['ffn2'], tm, layer)
    return x, (last[0], last[1], s_new, nbuf, hlast.reshape(nseq, d))


def _forward(x_prompt, x_sample, cache_attn_k, cache_attn_v, state_hgrn, state_conv, state_lru, w):
    bsz, seq, d = x_prompt.shape
    dbsz, dseq, _ = x_sample.shape
    depth = w['w_in'].shape[0]
    win_rows = cache_attn_k.shape[2]
    assert win_rows == WINDOW, "the attention kernels assume a full cached window"
    pp = _prepare_params(w)
    xp = x_prompt.reshape(bsz * seq, d)
    xs = x_sample.reshape(dbsz * dseq, d)
    zero_cache = (None, None,
                  jnp.zeros((1, bsz, HG_HEADS, HG_DK, HG_DV), F32),
                  jnp.zeros((1, bsz, CONV_WIDTH - 1, d), F32),
                  jnp.zeros((1, bsz, 1, d), F32))
    cache = (cache_attn_k.reshape(depth, dbsz * win_rows, _KV_W),
             cache_attn_v.reshape(depth, dbsz * win_rows, _KV_W),
             state_hgrn, state_conv, state_lru[:, :, None, :])
    st_p, st_s = [], []
    for l in range(depth):
        xp, sp = _layer(xp, pp, zero_cache, nseq=bsz, t=seq, layer=l, prompt=True)
        xs, ss = _layer(xs, pp, cache, nseq=dbsz, t=dseq, layer=l, prompt=False)
        st_p.append(sp)
        st_s.append(ss)

    def stack(sts, i):
        return jnp.stack([s[i] for s in sts], axis=0)

    return (xp.reshape(bsz, seq, d), xs.reshape(dbsz, dseq, d),
            stack(st_p, 0), stack(st_p, 1), stack(st_p, 2), stack(st_p, 3), stack(st_p, 4),
            stack(st_s, 0), stack(st_s, 1), stack(st_s, 2), stack(st_s, 3), stack(st_s, 4))


def kernel(x_prompt, x_sample, cache_attn_k, cache_attn_v, state_hgrn, state_conv, state_lru,
           norm_ffn1, w_ffn1_up, w_ffn1_down, norm_mix, w_in, q_norm, k_norm, attn_sinks, w_attn_o,
           hgrn_lb_logits, hgrn_o_norm, w_hgrn_o, conv_w, conv_b, lru_w_a, lru_b_a, lru_w_x, lru_b_x,
           lru_lambda, w_lru_o, w_out, norm_ffn2, w_ffn2_up, w_ffn2_down):
    w = dict(norm_ffn1=norm_ffn1, w_ffn1_up=w_ffn1_up, w_ffn1_down=w_ffn1_down, norm_mix=norm_mix,
             w_in=w_in, q_norm=q_norm, k_norm=k_norm, attn_sinks=attn_sinks, w_attn_o=w_attn_o,
             hgrn_lb_logits=hgrn_lb_logits, hgrn_o_norm=hgrn_o_norm, w_hgrn_o=w_hgrn_o,
             conv_w=conv_w, conv_b=conv_b, lru_w_a=lru_w_a, lru_b_a=lru_b_a, lru_w_x=lru_w_x,
             lru_b_x=lru_b_x, lru_lambda=lru_lambda, w_lru_o=w_lru_o, w_out=w_out,
             norm_ffn2=norm_ffn2, w_ffn2_up=w_ffn2_up, w_ffn2_down=w_ffn2_down)
    return _forward(x_prompt, x_sample, cache_attn_k, cache_attn_v, state_hgrn, state_conv,
                    state_lru, w)
```

```python
import functools

import numpy as np
import jax
import jax.numpy as jnp
from jax import lax
from jax.experimental import pallas as pl
from jax.experimental.pallas import tpu as pltpu

F32 = jnp.float32
BF16 = jnp.bfloat16

CHUNK = 64
N_HEADS = 16
N_KV_HEADS = 4
HEAD_DIM = 64
GROUP = N_HEADS // N_KV_HEADS
WINDOW = 128
SCALE = HEAD_DIM ** -0.5
HG_HEADS = 8
HG_DK = 128
HG_DV = 128
LOG_F_FLOOR = -60.0
CONV_WIDTH = 4
LRU_C = 8.0
LRU_BW = 64
EPS = 1e-6
NEG = -1e30

MXU_DIM = 256
VMEM_LIMIT_BYTES = 56 * 1024 * 1024

_SLAB = 1024
(_C_HQ, _C_HF, _C_HI, _C_HG) = range(4)
_N_HGRN_SLABS, _N_LRU_SLABS, _N_GATE_SLABS = 4, 2, 3
_KV_W = N_KV_HEADS * HEAD_DIM
_QKV_ROWS = N_HEADS * HEAD_DIM + 2 * _KV_W


def _cparams(sem):
    return pltpu.CompilerParams(dimension_semantics=sem, vmem_limit_bytes=VMEM_LIMIT_BYTES)


def _rms(x, g):
    return x * lax.rsqrt(jnp.mean(x * x, axis=-1, keepdims=True) + EPS) * g


def _sigmoid(x):
    return 1.0 / (1.0 + jnp.exp(-x))


def _silu(x):
    return x * _sigmoid(x)


def _dot(a, b):
    return jnp.dot(a, b, preferred_element_type=F32)


def _dot_nt(a, b):
    return lax.dot_general(a, b, (((1,), (1,)), ((), ())), preferred_element_type=F32)


def _dot_tn(a, b):
    return lax.dot_general(a, b, (((0,), (0,)), ((), ())), preferred_element_type=F32)


def _ff_chunks(dff, width):
    assert dff % MXU_DIM == 0
    edges = list(range(0, dff, width)) + [dff]
    return list(zip(edges[:-1], edges[1:]))


def _ffn_kernel(x_ref, g_ref, wu_ref, wd_ref, o_ref, *, chunks):
    x = x_ref[...]
    dff = wd_ref.shape[0]
    h = _rms(x, g_ref[...]).astype(BF16)
    acc = None
    for lo, hi in chunks:
        gate = _dot(h, wu_ref[:, lo:hi])
        val = _dot(h, wu_ref[:, dff + lo:dff + hi])
        part = _dot((_silu(gate) * val).astype(BF16), wd_ref[lo:hi, :])
        acc = part if acc is None else acc + part
    o_ref[...] = x + 0.5 * acc


def _resident(shape):
    return pl.BlockSpec(shape, lambda *_: (0,) * len(shape), pipeline_mode=pl.Buffered(1))


def _layer_resident(stacked, layer):
    shape = stacked.shape[1:]
    return pl.BlockSpec((None,) + shape, lambda *_: (layer,) + (0,) * len(shape),
                        pipeline_mode=pl.Buffered(1))


def _ffn(x, g, w_up, w_down, tm, layer):
    n, d = x.shape
    dff = w_down.shape[1]
    return pl.pallas_call(
        functools.partial(_ffn_kernel, chunks=_ff_chunks(dff, 6 * MXU_DIM)),
        grid=(n // tm,),
        in_specs=[
            pl.BlockSpec((tm, d), lambda i: (i, 0)),
            _layer_resident(g, layer),
            _layer_resident(w_up, layer),
            _layer_resident(w_down, layer),
        ],
        out_specs=pl.BlockSpec((tm, d), lambda i: (i, 0)),
        out_shape=jax.ShapeDtypeStruct((n, d), F32),
        compiler_params=_cparams(("parallel",)),
        name="ffn",
    )(x, g, w_up, w_down)


_LANES = 128


def _softmax_pv(s, sk, vt):
    m = jnp.maximum(jnp.max(s, axis=0, keepdims=True), sk)
    e = jnp.exp(s - m)
    den = jnp.sum(e, axis=0, keepdims=True) + jnp.exp(sk - m)
    return _dot(vt, (e * (1.0 / den)).astype(BF16))


def _group_queries(q_ref, j, lanes):
    qcat = jnp.concatenate(
        [q_ref[(j * GROUP + g) * HEAD_DIM:(j * GROUP + g + 1) * HEAD_DIM, lanes]
         for g in range(GROUP)], axis=1).astype(BF16)
    zero = jnp.zeros_like(qcat)
    return jnp.concatenate([qcat if i == j else zero for i in range(N_KV_HEADS)], axis=0)


def _sink_row(sinks_ref, j, width):
    return jnp.concatenate(
        [jnp.full((1, width), sinks_ref[j * GROUP + g], F32) for g in range(GROUP)], axis=1)


def _attn_sample_kernel(qkv_ref, kp_ref, vp_ref, sinks_ref, ya_ref, yat_ref, *, t):
    nq = N_HEADS * HEAD_DIM
    ncache = kp_ref.shape[0]
    keys = ncache + _LANES
    kfull = jnp.concatenate([kp_ref[...], qkv_ref[nq:nq + _KV_W, :].T], axis=0).astype(BF16)
    vt = jnp.concatenate([vp_ref[...].T, qkv_ref[nq + _KV_W:, :]], axis=1).astype(BF16)
    row = lax.broadcasted_iota(jnp.int32, (keys, 1), 0)
    key_seq = jnp.where(row < ncache, row // WINDOW, (row - ncache) // t)
    lane = lax.broadcasted_iota(jnp.int32, (1, GROUP * _LANES), 1)
    valid = key_seq == (lane % _LANES) // t
    for j in range(N_KV_HEADS):
        s = jnp.where(valid, _dot(kfull, _group_queries(qkv_ref, j, slice(None))), NEG)
        o = _softmax_pv(s, _sink_row(sinks_ref, j, _LANES), vt[j * HEAD_DIM:(j + 1) * HEAD_DIM])
        for g in range(GROUP):
            yat_ref[(j * GROUP + g) * HEAD_DIM:(j * GROUP + g + 1) * HEAD_DIM, :] = (
                o[:, g * _LANES:(g + 1) * _LANES])
    ya_ref[...] = yat_ref[...].T.astype(ya_ref.dtype)


def _attn_sample(qkv_t, cache_k, cache_v, sinks, *, nseq, t, layer):
    n = nseq * t
    nq = N_HEADS * HEAD_DIM
    sb = _LANES // t
    prev = pl.BlockSpec((None, sb * WINDOW, _KV_W), lambda b: (layer, b, 0))
    return pl.pallas_call(
        functools.partial(_attn_sample_kernel, t=t),
        grid=(nseq // sb,),
        in_specs=[
            pl.BlockSpec((_QKV_ROWS, _LANES), lambda b: (0, b)),
            prev, prev,
            pl.BlockSpec(memory_space=pltpu.SMEM),
        ],
        out_specs=pl.BlockSpec((_LANES, nq), lambda b: (b, 0)),
        out_shape=jax.ShapeDtypeStruct((n, nq), BF16),
        scratch_shapes=[pltpu.VMEM((nq, _LANES), F32)],
        compiler_params=_cparams(("parallel",)),
        name="attn_sample",
    )(qkv_t, cache_k, cache_v, sinks)


def _qkv_t_kernel(x_ref, g_ref, w_ref, gc_ref, o_ref):
    h = _rms(x_ref[...], g_ref[...]).astype(BF16)
    acc = _dot_nt(w_ref[...], h)
    nqk = gc_ref.shape[0]
    for r in range(0, nqk, HEAD_DIM):
        blk = acc[r:r + HEAD_DIM]
        ms = jnp.sum(blk * blk, axis=0, keepdims=True) * (1.0 / HEAD_DIM)
        o_ref[r:r + HEAD_DIM, :] = blk * lax.rsqrt(ms + EPS) * gc_ref[r:r + HEAD_DIM, :]
    o_ref[nqk:, :] = acc[nqk:]


def _qkv_t(x, g, w_t, gain_col, tm, layer):
    n, d = x.shape
    rows = w_t.shape[1]
    return pl.pallas_call(
        _qkv_t_kernel,
        grid=(n // tm,),
        in_specs=[
            pl.BlockSpec((tm, d), lambda i: (i, 0)),
            _layer_resident(g, layer),
            _layer_resident(w_t, layer),
            _layer_resident(gain_col, layer),
        ],
        out_specs=pl.BlockSpec((rows, tm), lambda i: (0, i)),
        out_shape=jax.ShapeDtypeStruct((rows, n), F32),
        compiler_params=_cparams(("parallel",)),
        name="qkv_t",
    )(x, g, w_t, gain_col)


_PAIR = 2 * CHUNK
_KEYS = _PAIR + WINDOW


def _attn_prompt_kernel(q_ref, kc_ref, vc_ref, kp_ref, vp_ref, sinks_ref, ya_ref, yat_ref):
    tq = q_ref.shape[1]
    t = pl.program_id(1)
    lane = lax.broadcasted_iota(jnp.int32, (1, GROUP * _PAIR), 1)
    first_chunk = (lane % _PAIR) < CHUNK
    k_cur = kc_ref[...].T.astype(BF16)
    k_prev = kp_ref[...].T.astype(BF16)
    for p in range(tq // _PAIR):
        lo = p * _PAIR
        if p == 0:
            kb = jnp.concatenate([k_prev, k_cur[:_PAIR]], axis=0)
        else:
            kb = k_cur[lo - WINDOW:lo + _PAIR]
        first_key_chunk = t * (tq // CHUNK) + (lo - WINDOW) // CHUNK
        for j in range(N_KV_HEADS):
            rows = slice(j * HEAD_DIM, (j + 1) * HEAD_DIM)
            if p == 0:
                vb = jnp.concatenate([vp_ref[rows, :], vc_ref[rows, :_PAIR]], axis=1)
            else:
                vb = vc_ref[rows, lo - WINDOW:lo + _PAIR]
            s = _dot(kb, _group_queries(q_ref, j, slice(lo, lo + _PAIR)))
            s = jnp.concatenate([
                jnp.where(first_chunk & (first_key_chunk >= 0), s[:CHUNK], NEG),
                jnp.where(first_key_chunk + 1 >= 0, s[CHUNK:2 * CHUNK], NEG),
                s[2 * CHUNK:3 * CHUNK],
                jnp.where(first_chunk, NEG, s[3 * CHUNK:]),
            ], axis=0)
            o = _softmax_pv(s, _sink_row(sinks_ref, j, _PAIR), vb.astype(BF16))
            for g in range(GROUP):
                yat_ref[(j * GROUP + g) * HEAD_DIM:(j * GROUP + g + 1) * HEAD_DIM, lo:lo + _PAIR] = (
                    o[:, g * _PAIR:(g + 1) * _PAIR])
    ya_ref[...] = yat_ref[...].T.astype(ya_ref.dtype)


def _attn_prompt(qkv_t, sinks, *, nseq, t, tq):
    n = nseq * t
    nt = t // tq
    nq = N_HEADS * HEAD_DIM
    kblk, vblk = nq // _KV_W, nq // _KV_W + 1

    def prev_map(blk):
        return lambda b, i: (blk, jnp.maximum(b * (t // WINDOW) + i * (tq // WINDOW) - 1, 0))

    return pl.pallas_call(
        _attn_prompt_kernel,
        grid=(nseq, nt),
        in_specs=[
            pl.BlockSpec((nq, tq), lambda b, i: (0, b * nt + i)),
            pl.BlockSpec((_KV_W, tq), lambda b, i: (kblk, b * nt + i)),
            pl.BlockSpec((_KV_W, tq), lambda b, i: (vblk, b * nt + i)),
            pl.BlockSpec((_KV_W, WINDOW), prev_map(kblk)),
            pl.BlockSpec((_KV_W, WINDOW), prev_map(vblk)),
            pl.BlockSpec(memory_space=pltpu.SMEM),
        ],
        out_specs=pl.BlockSpec((tq, nq), lambda b, i: (b * nt + i, 0)),
        out_shape=jax.ShapeDtypeStruct((n, nq), BF16),
        scratch_shapes=[pltpu.VMEM((nq, tq), F32)],
        compiler_params=_cparams(("parallel", "arbitrary")),
        name="attn_prompt",
    )(qkv_t, qkv_t, qkv_t, qkv_t, qkv_t, sinks)


def _hgrn_levels(c):
    return [c >> (i + 1) for i in range(int(np.log2(c)))]


_SUBLANES = 8


def _level_operands(q, k, g, s):
    rows = q.shape[0]
    if s < _SUBLANES:
        upper = (lax.broadcasted_iota(jnp.int32, (rows, 1), 0) & s) != 0
        return jnp.where(upper, q * g, 0.0), jnp.where(upper, 0.0, k * g)
    zero = jnp.zeros((s, q.shape[1]), F32)
    qparts, kparts = [], []
    for g0 in range(0, rows, 2 * s):
        lo, up = slice(g0, g0 + s), slice(g0 + s, g0 + 2 * s)
        kparts += [k[lo] * g[lo], zero]
        qparts += [zero, q[up] * g[up]]
    return jnp.concatenate(qparts, axis=0), jnp.concatenate(kparts, axis=0)


def _pair_block_diag(x):
    c = x.shape[1] // 2
    zero = jnp.zeros((x.shape[0], c), x.dtype)
    return jnp.concatenate([jnp.concatenate([x[:, :c], zero], axis=1),
                            jnp.concatenate([zero, x[:, c:]], axis=1)], axis=0)


def _hgrn_kernel(hq_ref, hf_ref, hi_ref, hg_ref, lbl_ref, on_ref, s0_ref, yb_ref, sout_ref,
                 a_ref, m_ref, st_ref, *, layer, cs):
    c = pl.program_id(1)
    rows = hq_ref.shape[0]
    nsq = rows // cs
    levels = _hgrn_levels(cs)

    @pl.when(c == 0)
    def _():
        row = lax.broadcasted_iota(jnp.int32, (rows, rows), 0)
        col = lax.broadcasted_iota(jnp.int32, (rows, rows), 1)
        mats = [((col <= row) & ((row & ~(cs - 1)) == (col & ~(cs - 1)))).astype(F32)]
        for i, s in enumerate(levels):
            mid = (row & ~(2 * s - 1)) + s
            upper = (row & s) != 0
            up = ((col >= mid) & (col <= row)).astype(F32)
            lo = ((col > row) & (col < mid)).astype(F32)
            mats.append(jnp.where(upper, up, lo))
            same = (row & ~(2 * s - 1)) == (col & ~(2 * s - 1))
            same = same.astype(F32)
            m_ref[i] = jnp.concatenate([same, same], axis=1)
        for i, a in enumerate(mats):
            a = a.astype(BF16)
            a_ref[i * rows:(i + 1) * rows, :] = jnp.concatenate([a, a, a], axis=1)
        for sq in range(nsq):
            for h in range(HG_HEADS):
                st_ref[sq * HG_HEADS + h] = s0_ref[sq, h].T

    lg = lbl_ref[...]
    e = jnp.exp(lg - jnp.max(lg, axis=0, keepdims=True))
    lb = jnp.zeros((1, e.shape[1]), F32)
    for i in range(1, layer + 1):
        lb = lb + e[i:i + 1]
    lb = lb / jnp.sum(e, axis=0, keepdims=True)

    sg = _sigmoid(hf_ref[...])
    f = lb + (1.0 - lb) * sg
    logf = jnp.maximum(jnp.log(jnp.maximum(f, 1e-26)), LOG_F_FLOOR)
    kk = (1.0 - lb) * (1.0 - sg)
    l1 = logf.astype(BF16)
    r1 = logf - l1.astype(F32)
    l2 = r1.astype(BF16)
    l3 = (r1 - l2.astype(F32)).astype(BF16)
    pieces = jnp.concatenate([l1, l2, l3], axis=0)

    lin = _dot(a_ref[...], pieces)
    bc = lin[:rows]
    ex = [lin[(i + 1) * rows:(i + 2) * rows] for i in range(len(levels))]
    on = on_ref[...]

    dk = HG_DK
    for hp in range(HG_HEADS // 2):
        sl = slice(2 * hp * dk, 2 * (hp + 1) * dk)
        q = _silu(hq_ref[:, sl])
        v = hi_ref[:, sl]
        vb = v.astype(BF16)
        k = kk[:, sl]
        b = bc[:, sl]
        att = None
        for i, s in enumerate(levels):
            qs, ks = _level_operands(q, k, jnp.exp(ex[i][:, sl]), s)
            part = _dot_nt(qs.astype(BF16), _pair_block_diag(ks.astype(BF16)))
            if 2 * s < rows:
                part = part * m_ref[i]
            att = part if att is None else att + part
        qk = q * k
        diag = jnp.concatenate(
            [jnp.broadcast_to(jnp.sum(qk[:, :dk], axis=-1, keepdims=True), (rows, dk)),
             jnp.broadcast_to(jnp.sum(qk[:, dk:], axis=-1, keepdims=True), (rows, dk))], axis=1)
        o = _dot(att.astype(BF16), _pair_block_diag(vb)) + diag * v
        qd = (q * jnp.exp(b)).astype(BF16)
        inter = []
        for sq in range(nsq):
            r = slice(sq * cs, (sq + 1) * cs)
            ia = sq * HG_HEADS + 2 * hp
            sta, stb = st_ref[ia], st_ref[ia + 1]
            zero = jnp.zeros((dk, dk), BF16)
            st2 = jnp.concatenate(
                [jnp.concatenate([sta.astype(BF16), zero], axis=1),
                 jnp.concatenate([zero, stb.astype(BF16)], axis=1)], axis=0)
            inter.append(_dot_nt(qd[r], st2))
            blast = b[(sq + 1) * cs - 1:(sq + 1) * cs, :]
            kd = (k[r] * jnp.exp(blast - b[r])).astype(BF16)
            upd = _dot_tn(vb[r], kd)
            decay = jnp.exp(blast)
            st_ref[ia] = sta * decay[:, :dk] + upd[:dk, :dk]
            st_ref[ia + 1] = stb * decay[:, dk:] + upd[dk:, dk:]
        o = o + (inter[0] if nsq == 1 else jnp.concatenate(inter, axis=0))
        on2 = jnp.concatenate([_rms(o[:, :dk], on), _rms(o[:, dk:], on)], axis=1)
        yb_ref[:, sl] = (on2 * _silu(hg_ref[:, sl])).astype(yb_ref.dtype)

    @pl.when(c == pl.num_programs(1) - 1)
    def _():
        for sq in range(nsq):
            for h in range(HG_HEADS):
                sout_ref[sq, h] = st_ref[sq * HG_HEADS + h].T


def _hgrn(p_all, lb_logits, o_norm, s0, *, nseq, t, cs, sb, layer, s0_layer):
    assert sb == 1 or cs == t
    n = nseq * t
    nc = t // cs
    rows = sb * cs
    nlev = len(_hgrn_levels(cs))

    def slab(col):
        return pl.BlockSpec((rows, _SLAB), lambda b, i: (b * nc + i, col))

    st_block = (sb, HG_HEADS, HG_DK, HG_DV)
    return pl.pallas_call(
        functools.partial(_hgrn_kernel, layer=layer, cs=cs),
        grid=(nseq // sb, nc),
        in_specs=[
            slab(_C_HQ), slab(_C_HF), slab(_C_HI), slab(_C_HG),
            _resident(lb_logits.shape),
            _layer_resident(o_norm, layer),
            pl.BlockSpec((None,) + st_block, lambda b, i: (s0_layer, b, 0, 0, 0)),
        ],
        out_specs=[pl.BlockSpec((rows, _SLAB), lambda b, i: (b * nc + i, 0)),
                   pl.BlockSpec(st_block, lambda b, i: (b, 0, 0, 0))],
        out_shape=[jax.ShapeDtypeStruct((n, _SLAB), BF16),
                   jax.ShapeDtypeStruct((nseq, HG_HEADS, HG_DK, HG_DV), F32)],
        scratch_shapes=[
            pltpu.VMEM(((nlev + 1) * rows, 3 * rows), BF16),
            pltpu.VMEM((nlev, rows, 2 * rows), F32),
            pltpu.VMEM((sb * HG_HEADS, HG_DV, HG_DK), F32),
        ],
        compiler_params=_cparams(("parallel", "arbitrary")),
        name="hgrn_t%d" % cs,
    )(p_all, p_all, p_all, p_all, lb_logits, o_norm, s0)


_PAD = 8


def _lru_kernel(x_ref, g_ref, win_ref, whg_ref, cw_ref, cb_ref, wa_ref, ba_ref, wx_ref, bx_ref,
                lam_ref, cbuf_ref, h0_ref, yc_ref, p_ref, nbuf_ref, hlast_ref,
                xe_ref, hs_ref, hc_ref):
    t = pl.program_id(1)
    rows, w = x_ref.shape
    nsq = xe_ref.shape[0]
    tt = rows // nsq
    hist = CONV_WIDTH - 1
    hn = _rms(x_ref[...], g_ref[...]).astype(BF16)

    @pl.when(t == 0)
    def _():
        for s in range(nsq):
            xe_ref[s, _PAD - hist:_PAD, :] = cbuf_ref[s]
            hc_ref[s] = h0_ref[s]

    @pl.when(t > 0)
    def _():
        for s in range(nsq):
            xe_ref[s, _PAD - hist:_PAD, :] = xe_ref[s, _PAD + tt - hist:_PAD + tt, :]

    ngroups = w // MXU_DIM
    pslab = p_ref.shape[1] // ngroups
    sub = lax.broadcasted_iota(jnp.int32, (_SUBLANES, 1), 0)

    def project(g):
        return (_dot(hn, win_ref[:, g * MXU_DIM:(g + 1) * MXU_DIM]),
                _dot(hn, win_ref[:, w + g * MXU_DIM:w + (g + 1) * MXU_DIM]))

    nxt = project(0)
    for g in range(ngroups):
        cols = slice(g * MXU_DIM, (g + 1) * MXU_DIM)
        lx, lgv = nxt
        if g + 1 < ngroups:
            nxt = project(g + 1)
        xcs = []
        for s in range(nsq):
            xe_ref[s, _PAD:_PAD + tt, cols] = lx[s * tt:(s + 1) * tt]
            xc = xe_ref[s, _PAD - hist:_PAD - hist + tt, cols] * cw_ref[0:1, cols]
            for j in range(1, CONV_WIDTH):
                xc = xc + (xe_ref[s, _PAD - hist + j:_PAD - hist + j + tt, cols]
                           * cw_ref[j:j + 1, cols])
            xcs.append(xc)
        xc = (xcs[0] if nsq == 1 else jnp.concatenate(xcs, axis=0)) + cb_ref[:, cols]
        gelu = 0.5 * lgv * (
            1.0 + jnp.tanh(np.sqrt(2.0 / np.pi) * (lgv + 0.044715 * (lgv * lgv * lgv))))
        xg = xc.astype(BF16)
        ra = _dot(xg, wa_ref[g])
        rx = _dot(xg, wx_ref[g])
        p_ref[:, g * pslab:(g + 1) * pslab] = _dot(hn, whg_ref[:, g * pslab:(g + 1) * pslab])
        r = _sigmoid(ra + ba_ref[:, cols])
        ig = _sigmoid(rx + bx_ref[:, cols])
        nl = -lam_ref[:, cols]
        softplus = jnp.maximum(nl, 0.0) + jnp.log1p(jnp.exp(-jnp.abs(nl)))
        log_a = -LRU_C * r * softplus
        a = jnp.exp(log_a)
        u = -jnp.tanh(log_a) * (a * a + 1.0)
        bv = jnp.where(u > 0.0, u * lax.rsqrt(u), 0.0) * (ig * xc)

        for s in range(nsq):
            h = jnp.broadcast_to(hc_ref[s, :, cols], (_SUBLANES, MXU_DIM))
            for r0 in range(s * tt, (s + 1) * tt, _SUBLANES):
                am = a[r0:r0 + _SUBLANES]
                bm = bv[r0:r0 + _SUBLANES]
                d = 1
                while d < _SUBLANES:
                    seen = sub >= d
                    a_prev = jnp.where(seen, pltpu.roll(am, d, 0), 1.0)
                    b_prev = jnp.where(seen, pltpu.roll(bm, d, 0), 0.0)
                    bm = am * b_prev + bm
                    am = am * a_prev
                    d *= 2
                hs = am * h + bm
                hs_ref[r0:r0 + _SUBLANES, cols] = hs
                h = jnp.broadcast_to(hs[_SUBLANES - 1:, :], hs.shape)
            hc_ref[s, :, cols] = h[:1, :]
        yc_ref[:, cols] = (hs_ref[:, cols] * gelu).astype(yc_ref.dtype)

    @pl.when(t == pl.num_programs(1) - 1)
    def _():
        for s in range(nsq):
            nbuf_ref[s] = xe_ref[s, _PAD + tt - hist:_PAD + tt, :]
            hlast_ref[s] = hc_ref[s]


def _lru(x, params, cbuf, h0, *, nseq, t, tt, sb, layer, state_layer):
    assert sb == 1 or tt == t
    n, w = x.shape
    nt = t // tt
    rows = sb * tt
    hist = CONV_WIDTH - 1
    pcols = params[2].shape[2]
    return pl.pallas_call(
        _lru_kernel,
        grid=(nseq // sb, nt),
        in_specs=[
            pl.BlockSpec((rows, w), lambda b, i: (b * nt + i, 0)),
            *[_layer_resident(p, layer) for p in params],
            pl.BlockSpec((None, sb, hist, w), lambda b, i: (state_layer, b, 0, 0)),
            pl.BlockSpec((None, sb, 1, w), lambda b, i: (state_layer, b, 0, 0)),
        ],
        out_specs=[
            pl.BlockSpec((rows, w), lambda b, i: (b * nt + i, 0)),
            pl.BlockSpec((rows, pcols), lambda b, i: (b * nt + i, 0)),
            pl.BlockSpec((sb, hist, w), lambda b, i: (b, 0, 0)),
            pl.BlockSpec((sb, 1, w), lambda b, i: (b, 0, 0)),
        ],
        out_shape=[jax.ShapeDtypeStruct((n, w), BF16),
                   jax.ShapeDtypeStruct((n, pcols), F32),
                   jax.ShapeDtypeStruct((nseq, hist, w), F32),
                   jax.ShapeDtypeStruct((nseq, 1, w), F32)],
        scratch_shapes=[
            pltpu.VMEM((sb, _PAD + tt, w), F32),
            pltpu.VMEM((rows, w), F32),
            pltpu.VMEM((sb, 1, w), F32),
        ],
        compiler_params=_cparams(("parallel", "arbitrary")),
        name="lru_t%d" % tt,
    )(x, *params, cbuf, h0)


def _merge_kernel(x_ref, ya_ref, yb_ref, yc_ref, g_ref, wg_ref,
                  wa_ref, wb_ref, wc_ref, wo_ref, o_ref):
    x = x_ref[...]
    d = x.shape[1]
    hn = _rms(x, g_ref[...]).astype(BF16)
    m = None
    for i, (y_ref, w_ref) in enumerate(((ya_ref, wa_ref), (yb_ref, wb_ref), (yc_ref, wc_ref))):
        gate = _sigmoid(_dot(hn, wg_ref[:, i * d:(i + 1) * d]))
        part = gate * _dot(y_ref[...], w_ref[...])
        m = part if m is None else m + part
    o_ref[...] = x + _dot(m.astype(BF16), wo_ref[...])


def _merge(x, ya, yb, yc, params, tm, layer):
    n, d = x.shape
    row = pl.BlockSpec((tm, d), lambda i: (i, 0))
    return pl.pallas_call(
        _merge_kernel,
        grid=(n // tm,),
        in_specs=[row, row, row, row, *[_layer_resident(p, layer) for p in params]],
        out_specs=row,
        out_shape=jax.ShapeDtypeStruct((n, d), F32),
        compiler_params=_cparams(("parallel",)),
        name="merge",
    )(x, ya, yb, yc, *params)


def _block_diag(w):
    nl, nb, bw, _ = w.shape
    per = MXU_DIM // bw
    wg = w.reshape(nl, nb // per, per, bw, bw)
    eye = jnp.eye(per, dtype=w.dtype)
    out = jnp.einsum('lgpij,pq->lgpiqj', wg, eye)
    return out.reshape(nl, nb // per, MXU_DIM, MXU_DIM).astype(BF16)


def _prepare_params(w):
    def vec(name):
        return w[name][:, None, :]

    def bf16(name):
        return w[name].astype(BF16)

    w_in = w['w_in']
    c_hgrn = _QKV_ROWS
    c_lru = c_hgrn + _N_HGRN_SLABS * _SLAB
    c_gate = c_lru + _N_LRU_SLABS * _SLAB
    assert c_gate + _N_GATE_SLABS * _SLAB == w_in.shape[2]
    qk_gain =jnp.concatenate([jnp.tile(w['q_norm'] * SCALE, (1, N_HEADS)),
                               jnp.tile(w['k_norm'], (1, N_KV_HEADS))], axis=1)
    return {
        'norm_ffn1': vec('norm_ffn1'), 'norm_mix': vec('norm_mix'), 'norm_ffn2': vec('norm_ffn2'),
        'ffn1': (bf16('w_ffn1_up'), bf16('w_ffn1_down')),
        'ffn2': (bf16('w_ffn2_up'), bf16('w_ffn2_down')),
        'w_qkv_t': jnp.swapaxes(w_in[:, :, :_QKV_ROWS], 1, 2).astype(BF16),
        'qk_gain_col': qk_gain[:, :, None],
        'attn_sinks': w['attn_sinks'],
        'merge': (vec('norm_mix'), w_in[:, :, c_gate:].astype(BF16),
                  bf16('w_attn_o'), bf16('w_hgrn_o'), bf16('w_lru_o'), bf16('w_out')),
        'hgrn_lb_logits': w['hgrn_lb_logits'], 'hgrn_o_norm': vec('hgrn_o_norm'),
        'lru': (vec('norm_mix'), w_in[:, :, c_lru:c_gate].astype(BF16),
                w_in[:, :, c_hgrn:c_lru].astype(BF16), w['conv_w'], vec('conv_b'), _block_diag(w['lru_w_a']), vec('lru_b_a'),
                _block_diag(w['lru_w_x']), vec('lru_b_x'), vec('lru_lambda')),
    }


def _tiles(n, prefer):
    for tm in prefer:
        if n % tm == 0:
            return tm
    return n


def _layer(x, pp, cache, *, nseq, t, layer, prompt):
    n, d = x.shape
    tm = _tiles(n, (512, 256, 128))
    x = _ffn(x, pp['norm_ffn1'], *pp['ffn1'], tm, layer)
    ck, cv, s0, cbuf, h0 = cache
    state_layer = 0 if prompt else layer
    qkv_t =_qkv_t(x, pp['norm_mix'], pp['w_qkv_t'], pp['qk_gain_col'], tm, layer)
    sinks = pp['attn_sinks'][layer]
    nq = N_HEADS * HEAD_DIM
    if prompt:
        ya = _attn_prompt(qkv_t, sinks, nseq=nseq, t=t, tq=_tiles(t, (256, 128)))
        last = jnp.stack([qkv_t[nq:, (b + 1) * t - WINDOW:(b + 1) * t] for b in range(nseq)])
        cs = _tiles(t, (128, 64, 32, 16))
        tt = _tiles(t, (512, 256, 128, 64, 32, 16))
        sb = 1
    else:
        ya = _attn_sample(qkv_t, ck, cv, sinks, nseq=nseq, t=t, layer=layer)
        last = qkv_t[nq:, :].reshape(2 * _KV_W, nseq, t).transpose(1, 0, 2)
        cs = tt = t
        sb = _tiles(nseq, (4, 2, 1))
    last = last.reshape(nseq, 2, N_KV_HEADS, HEAD_DIM, -1).transpose(1, 0, 4, 2, 3)
    yc, p_all, nbuf, hlast = _lru(x, pp['lru'], cbuf, h0, nseq=nseq, t=t, tt=tt,
                                  sb=1 if prompt else _tiles(nseq, (8, 4, 2, 1)), layer=layer,
                                  state_layer=state_layer)
    yb, s_new = _hgrn(p_all, pp['hgrn_lb_logits'], pp['hgrn_o_norm'], s0, nseq=nseq, t=t, cs=cs,
                      sb=sb, layer=layer, s0_layer=state_layer)
    x = _merge(x, ya, yb, yc, pp['merge'], tm, layer)
    x = _ffn(x, pp['norm_ffn2'], *pp['ffn2'], tm, layer)
    return x, (last[0], last[1], s_new, nbuf, hlast.reshape(nseq, d))


def _forward(x_prompt, x_sample, cache_attn_k, cache_attn_v, state_hgrn, state_conv, state_lru, w):
    bsz, seq, d = x_prompt.shape
    dbsz, dseq, _ = x_sample.shape
    depth = w['w_in'].shape[0]
    win_rows = cache_attn_k.shape[2]
    assert win_rows == WINDOW, "the attention kernels assume a full cached window"
    pp = _prepare_params(w)
    xp = x_prompt.reshape(bsz * seq, d)
    xs = x_sample.reshape(dbsz * dseq, d)
    zero_cache = (None, None,
                  jnp.zeros((1, bsz, HG_HEADS, HG_DK, HG_DV), F32),
                  jnp.zeros((1, bsz, CONV_WIDTH - 1, d), F32),
                  jnp.zeros((1, bsz, 1, d), F32))
    cache = (cache_attn_k.reshape(depth, dbsz * win_rows, _KV_W),
             cache_attn_v.reshape(depth, dbsz * win_rows, _KV_W),
             state_hgrn, state_conv, state_lru[:, :, None, :])
    st_p, st_s = [], []
    for l in range(depth):
        xp, sp = _layer(xp, pp, zero_cache, nseq=bsz, t=seq, layer=l, prompt=True)
        xs, ss = _layer(xs, pp, cache, nseq=dbsz, t=dseq, layer=l, prompt=False)
        st_p.append(sp)
        st_s.append(ss)

    def stack(sts, i):
        return jnp.stack([s[i] for s in sts], axis=0)

    return (xp.reshape(bsz, seq, d), xs.reshape(dbsz, dseq, d),
            stack(st_p, 0), stack(st_p, 1), stack(st_p, 2), stack(st_p, 3), stack(st_p, 4),
            stack(st_s, 0), stack(st_s, 1), stack(st_s, 2), stack(st_s, 3), stack(st_s, 4))


def kernel(x_prompt, x_sample, cache_attn_k, cache_attn_v, state_hgrn, state_conv, state_lru,
           norm_ffn1, w_ffn1_up, w_ffn1_down, norm_mix, w_in, q_norm, k_norm, attn_sinks, w_attn_o,
           hgrn_lb_logits, hgrn_o_norm, w_hgrn_o, conv_w, conv_b, lru_w_a, lru_b_a, lru_w_x, lru_b_x,
           lru_lambda, w_lru_o, w_out, norm_ffn2, w_ffn2_up, w_ffn2_down):
    w = dict(norm_ffn1=norm_ffn1, w_ffn1_up=w_ffn1_up, w_ffn1_down=w_ffn1_down, norm_mix=norm_mix,
             w_in=w_in, q_norm=q_norm, k_norm=k_norm, attn_sinks=attn_sinks, w_attn_o=w_attn_o,
             hgrn_lb_logits=hgrn_lb_logits, hgrn_o_norm=hgrn_o_norm, w_hgrn_o=w_hgrn_o,
             conv_w=conv_w, conv_b=conv_b, lru_w_a=lru_w_a, lru_b_a=lru_b_a, lru_w_x=lru_w_x,
             lru_b_x=lru_b_x, lru_lambda=lru_lambda, w_lru_o=w_lru_o, w_out=w_out,
             norm_ffn2=norm_ffn2, w_ffn2_up=w_ffn2_up, w_ffn2_down=w_ffn2_down)
    return _forward(x_prompt, x_sample, cache_attn_k, cache_attn_v, state_hgrn, state_conv,
                    state_lru, w)
```

```python
import functools

import numpy as np
import jax
import jax.numpy as jnp
from jax import lax
from jax.experimental import pallas as pl
from jax.experimental.pallas import tpu as pltpu

F32 = jnp.float32
BF16 = jnp.bfloat16

CHUNK = 64
N_HEADS = 16
N_KV_HEADS = 4
HEAD_DIM = 64
GROUP = N_HEADS // N_KV_HEADS
WINDOW = 128
SCALE = HEAD_DIM ** -0.5
HG_HEADS = 8
HG_DK = 128
HG_DV = 128
LOG_F_FLOOR = -60.0
CONV_WIDTH = 4
LRU_C = 8.0
LRU_BW = 64
EPS = 1e-6
NEG = -1e30

MXU_DIM = 256
VMEM_LIMIT_BYTES = 56 * 1024 * 1024

_SLAB = 1024
(_C_HQ, _C_HF, _C_HI, _C_HG) = range(4)
_N_HGRN_SLABS, _N_LRU_SLABS, _N_GATE_SLABS = 4, 2, 3
_KV_W = N_KV_HEADS * HEAD_DIM
_QKV_ROWS = N_HEADS * HEAD_DIM + 2 * _KV_W


def _cparams(sem):
    return pltpu.CompilerParams(dimension_semantics=sem, vmem_limit_bytes=VMEM_LIMIT_BYTES)


def _rms(x, g):
    return x * lax.rsqrt(jnp.mean(x * x, axis=-1, keepdims=True) + EPS) * g


def _sigmoid(x):
    return 1.0 / (1.0 + jnp.exp(-x))


def _silu(x):
    return x * _sigmoid(x)


def _dot(a, b):
    return jnp.dot(a, b, preferred_element_type=F32)


def _dot_nt(a, b):
    return lax.dot_general(a, b, (((1,), (1,)), ((), ())), preferred_element_type=F32)


def _dot_tn(a, b):
    return lax.dot_general(a, b, (((0,), (0,)), ((), ())), preferred_element_type=F32)


def _ff_chunks(dff, width):
    assert dff % MXU_DIM == 0
    edges = list(range(0, dff, width)) + [dff]
    return list(zip(edges[:-1], edges[1:]))


def _ffn_kernel(x_ref, g_ref, wu_ref, wd_ref, o_ref, *, chunks):
    x = x_ref[...]
    dff = wd_ref.shape[0]
    h = _rms(x, g_ref[...]).astype(BF16)
    acc = None
    for lo, hi in chunks:
        gate = _dot(h, wu_ref[:, lo:hi])
        val = _dot(h, wu_ref[:, dff + lo:dff + hi])
        part = _dot((_silu(gate) * val).astype(BF16), wd_ref[lo:hi, :])
        acc = part if acc is None else acc + part
    o_ref[...] = x + 0.5 * acc


def _resident(shape):
    return pl.BlockSpec(shape, lambda *_: (0,) * len(shape), pipeline_mode=pl.Buffered(1))


def _layer_resident(stacked, layer):
    shape = stacked.shape[1:]
    return pl.BlockSpec((None,) + shape, lambda *_: (layer,) + (0,) * len(shape),
                        pipeline_mode=pl.Buffered(1))


def _ffn(x, g, w_up, w_down, tm, layer):
    n, d = x.shape
    dff = w_down.shape[1]
    return pl.pallas_call(
        functools.partial(_ffn_kernel, chunks=_ff_chunks(dff, 6 * MXU_DIM)),
        grid=(n // tm,),
        in_specs=[
            pl.BlockSpec((tm, d), lambda i: (i, 0)),
            _layer_resident(g, layer),
            _layer_resident(w_up, layer),
            _layer_resident(w_down, layer),
        ],
        out_specs=pl.BlockSpec((tm, d), lambda i: (i, 0)),
        out_shape=jax.ShapeDtypeStruct((n, d), F32),
        compiler_params=_cparams(("parallel",)),
        name="ffn",
    )(x, g, w_up, w_down)


_LANES = 128


def _softmax_pv(s, sk, vt):
    m = jnp.maximum(jnp.max(s, axis=0, keepdims=True), sk)
    e = jnp.exp(s - m)
    den = jnp.sum(e, axis=0, keepdims=True) + jnp.exp(sk - m)
    return _dot(vt, (e * (1.0 / den)).astype(BF16))


def _group_queries(q_ref, j, lanes):
    qcat = jnp.concatenate(
        [q_ref[(j * GROUP + g) * HEAD_DIM:(j * GROUP + g + 1) * HEAD_DIM, lanes]
         for g in range(GROUP)], axis=1).astype(BF16)
    zero = jnp.zeros_like(qcat)
    return jnp.concatenate([qcat if i == j else zero for i in range(N_KV_HEADS)], axis=0)


def _sink_row(sinks_ref, j, width):
    return jnp.concatenate(
        [jnp.full((1, width), sinks_ref[j * GROUP + g], F32) for g in range(GROUP)], axis=1)


def _attn_sample_kernel(qkv_ref, kp_ref, vp_ref, sinks_ref, ya_ref, yat_ref, *, t):
    nq = N_HEADS * HEAD_DIM
    ncache = kp_ref.shape[0]
    keys = ncache + _LANES
    kfull = jnp.concatenate([kp_ref[...], qkv_ref[nq:nq + _KV_W, :].T], axis=0).astype(BF16)
    vt = jnp.concatenate([vp_ref[...].T, qkv_ref[nq + _KV_W:, :]], axis=1).astype(BF16)
    row = lax.broadcasted_iota(jnp.int32, (keys, 1), 0)
    key_seq = jnp.where(row < ncache, row // WINDOW, (row - ncache) // t)
    lane = lax.broadcasted_iota(jnp.int32, (1, GROUP * _LANES), 1)
    valid = key_seq == (lane % _LANES) // t
    for j in range(N_KV_HEADS):
        s = jnp.where(valid, _dot(kfull, _group_queries(qkv_ref, j, slice(None))), NEG)
        o = _softmax_pv(s, _sink_row(sinks_ref, j, _LANES), vt[j * HEAD_DIM:(j + 1) * HEAD_DIM])
        for g in range(GROUP):
            yat_ref[(j * GROUP + g) * HEAD_DIM:(j * GROUP + g + 1) * HEAD_DIM, :] = (
                o[:, g * _LANES:(g + 1) * _LANES])
    ya_ref[...] = yat_ref[...].T.astype(ya_ref.dtype)


def _attn_sample(qkv_t, cache_k, cache_v, sinks, *, nseq, t, layer):
    n = nseq * t
    nq = N_HEADS * HEAD_DIM
    sb = _LANES // t
    prev = pl.BlockSpec((None, sb * WINDOW, _KV_W), lambda b: (layer, b, 0))
    return pl.pallas_call(
        functools.partial(_attn_sample_kernel, t=t),
        grid=(nseq // sb,),
        in_specs=[
            pl.BlockSpec((_QKV_ROWS, _LANES), lambda b: (0, b)),
            prev, prev,
            pl.BlockSpec(memory_space=pltpu.SMEM),
        ],
        out_specs=pl.BlockSpec((_LANES, nq), lambda b: (b, 0)),
        out_shape=jax.ShapeDtypeStruct((n, nq), BF16),
        scratch_shapes=[pltpu.VMEM((nq, _LANES), F32)],
        compiler_params=_cparams(("parallel",)),
        name="attn_sample",
    )(qkv_t, cache_k, cache_v, sinks)


def _qkv_t_kernel(x_ref, g_ref, w_ref, gc_ref, o_ref):
    h = _rms(x_ref[...], g_ref[...]).astype(BF16)
    acc = _dot_nt(w_ref[...], h)
    nqk = gc_ref.shape[0]
    for r in range(0, nqk, HEAD_DIM):
        blk = acc[r:r + HEAD_DIM]
        ms = jnp.sum(blk * blk, axis=0, keepdims=True) * (1.0 / HEAD_DIM)
        o_ref[r:r + HEAD_DIM, :] = blk * lax.rsqrt(ms + EPS) * gc_ref[r:r + HEAD_DIM, :]
    o_ref[nqk:, :] = acc[nqk:]


def _qkv_t(x, g, w_t, gain_col, tm, layer):
    n, d = x.shape
    rows = w_t.shape[1]
    return pl.pallas_call(
        _qkv_t_kernel,
        grid=(n // tm,),
        in_specs=[
            pl.BlockSpec((tm, d), lambda i: (i, 0)),
            _layer_resident(g, layer),
            _layer_resident(w_t, layer),
            _layer_resident(gain_col, layer),
        ],
        out_specs=pl.BlockSpec((rows, tm), lambda i: (0, i)),
        out_shape=jax.ShapeDtypeStruct((rows, n), F32),
        compiler_params=_cparams(("parallel",)),
        name="qkv_t",
    )(x, g, w_t, gain_col)


_PAIR = 2 * CHUNK
_KEYS = _PAIR + WINDOW


def _attn_prompt_kernel(q_ref, kc_ref, vc_ref, kp_ref, vp_ref, sinks_ref, ya_ref, yat_ref):
    tq = q_ref.shape[1]
    t = pl.program_id(1)
    lane = lax.broadcasted_iota(jnp.int32, (1, GROUP * _PAIR), 1)
    first_chunk = (lane % _PAIR) < CHUNK
    k_cur = kc_ref[...].T.astype(BF16)
    k_prev = kp_ref[...].T.astype(BF16)
    for p in range(tq // _PAIR):
        lo = p * _PAIR
        if p == 0:
            kb = jnp.concatenate([k_prev, k_cur[:_PAIR]], axis=0)
        else:
            kb = k_cur[lo - WINDOW:lo + _PAIR]
        first_key_chunk = t * (tq // CHUNK) + (lo - WINDOW) // CHUNK
        for j in range(N_KV_HEADS):
            rows = slice(j * HEAD_DIM, (j + 1) * HEAD_DIM)
            if p == 0:
                vb = jnp.concatenate([vp_ref[rows, :], vc_ref[rows, :_PAIR]], axis=1)
            else:
                vb = vc_ref[rows, lo - WINDOW:lo + _PAIR]
            s = _dot(kb, _group_queries(q_ref, j, slice(lo, lo + _PAIR)))
            s = jnp.concatenate([
                jnp.where(first_chunk & (first_key_chunk >= 0), s[:CHUNK], NEG),
                jnp.where(first_key_chunk + 1 >= 0, s[CHUNK:2 * CHUNK], NEG),
                s[2 * CHUNK:3 * CHUNK],
                jnp.where(first_chunk, NEG, s[3 * CHUNK:]),
            ], axis=0)
            o = _softmax_pv(s, _sink_row(sinks_ref, j, _PAIR), vb.astype(BF16))
            for g in range(GROUP):
                yat_ref[(j * GROUP + g) * HEAD_DIM:(j * GROUP + g + 1) * HEAD_DIM, lo:lo + _PAIR] = (
                    o[:, g * _PAIR:(g + 1) * _PAIR])
    ya_ref[...] = yat_ref[...].T.astype(ya_ref.dtype)


def _attn_prompt(qkv_t, sinks, *, nseq, t, tq):
    n = nseq * t
    nt = t // tq
    nq = N_HEADS * HEAD_DIM
    kblk, vblk = nq // _KV_W, nq // _KV_W + 1

    def prev_map(blk):
        return lambda b, i: (blk, jnp.maximum(b * (t // WINDOW) + i * (tq // WINDOW) - 1, 0))

    return pl.pallas_call(
        _attn_prompt_kernel,
        grid=(nseq, nt),
        in_specs=[
            pl.BlockSpec((nq, tq), lambda b, i: (0, b * nt + i)),
            pl.BlockSpec((_KV_W, tq), lambda b, i: (kblk, b * nt + i)),
            pl.BlockSpec((_KV_W, tq), lambda b, i: (vblk, b * nt + i)),
            pl.BlockSpec((_KV_W, WINDOW), prev_map(kblk)),
            pl.BlockSpec((_KV_W, WINDOW), prev_map(vblk)),
            pl.BlockSpec(memory_space=pltpu.SMEM),
        ],
        out_specs=pl.BlockSpec((tq, nq), lambda b, i: (b * nt + i, 0)),
        out_shape=jax.ShapeDtypeStruct((n, nq), BF16),
        scratch_shapes=[pltpu.VMEM((nq, tq), F32)],
        compiler_params=_cparams(("parallel", "arbitrary")),
        name="attn_prompt",
    )(qkv_t, qkv_t, qkv_t, qkv_t, qkv_t, sinks)


def _hgrn_levels(c):
    return [c >> (i + 1) for i in range(int(np.log2(c)))]


_SUBLANES = 8


def _level_operands(q, k, g, s):
    rows = q.shape[0]
    if s < _SUBLANES:
        upper = (lax.broadcasted_iota(jnp.int32, (rows, 1), 0) & s) != 0
        return jnp.where(upper, q * g, 0.0), jnp.where(upper, 0.0, k * g)
    zero = jnp.zeros((s, q.shape[1]), F32)
    qparts, kparts = [], []
    for g0 in range(0, rows, 2 * s):
        lo, up = slice(g0, g0 + s), slice(g0 + s, g0 + 2 * s)
        kparts += [k[lo] * g[lo], zero]
        qparts += [zero, q[up] * g[up]]
    return jnp.concatenate(qparts, axis=0), jnp.concatenate(kparts, axis=0)


def _pair_block_diag(x):
    c = x.shape[1] // 2
    zero = jnp.zeros((x.shape[0], c), x.dtype)
    return jnp.concatenate([jnp.concatenate([x[:, :c], zero], axis=1),
                            jnp.concatenate([zero, x[:, c:]], axis=1)], axis=0)


def _hgrn_kernel(hq_ref, hf_ref, hi_ref, hg_ref, lbl_ref, on_ref, s0_ref, yb_ref, sout_ref,
                 a_ref, m_ref, st_ref, *, layer, cs):
    c = pl.program_id(1)
    rows = hq_ref.shape[0]
    nsq = rows // cs
    levels = _hgrn_levels(cs)

    @pl.when(c == 0)
    def _():
        row = lax.broadcasted_iota(jnp.int32, (rows, rows), 0)
        col = lax.broadcasted_iota(jnp.int32, (rows, rows), 1)
        mats = [((col <= row) & ((row & ~(cs - 1)) == (col & ~(cs - 1)))).astype(F32)]
        for i, s in enumerate(levels):
            mid = (row & ~(2 * s - 1)) + s
            upper = (row & s) != 0
            if s < _SUBLANES:
                up = ((col >= mid) & (col <= row)).astype(F32)
                lo = ((col > row) & (col < mid)).astype(F32)
                mats.append(jnp.where(upper, up, lo))
            same =(row & ~(2 * s - 1)) == (col & ~(2 * s - 1))
            same = same.astype(F32)
            m_ref[i] = jnp.concatenate([same, same], axis=1)
        for i, a in enumerate(mats):
            a = a.astype(BF16)
            a_ref[i * rows:(i + 1) * rows, :] = jnp.concatenate([a, a, a], axis=1)
        for sq in range(nsq):
            for h in range(HG_HEADS):
                st_ref[sq * HG_HEADS + h] = s0_ref[sq, h].T

    lg = lbl_ref[...]
    e = jnp.exp(lg - jnp.max(lg, axis=0, keepdims=True))
    lb = jnp.zeros((1, e.shape[1]), F32)
    for i in range(1, layer + 1):
        lb = lb + e[i:i + 1]
    lb = lb / jnp.sum(e, axis=0, keepdims=True)

    sg = _sigmoid(hf_ref[...])
    f = lb + (1.0 - lb) * sg
    logf = jnp.maximum(jnp.log(jnp.maximum(f, 1e-26)), LOG_F_FLOOR)
    kk = (1.0 - lb) * (1.0 - sg)
    l1 = logf.astype(BF16)
    r1 = logf - l1.astype(F32)
    l2 = r1.astype(BF16)
    l3 = (r1 - l2.astype(F32)).astype(BF16)
    pieces = jnp.concatenate([l1, l2, l3], axis=0)

    lin = _dot(a_ref[...], pieces)
    bc = lin[:rows]
    ex, nmat = [], 1
    for s in levels:
        if s < _SUBLANES:
            ex.append(lin[nmat * rows:(nmat + 1) * rows])
            nmat += 1
        else:
            ex.append(jnp.concatenate(
                [-jnp.abs(bc[g0:g0 + 2 * s] - bc[g0 + s - 1:g0 + s]) for g0 in range(0, rows, 2 * s)],
                axis=0))
    on = on_ref[...]

    dk = HG_DK
    for hp in range(HG_HEADS // 2):
        sl = slice(2 * hp * dk, 2 * (hp + 1) * dk)
        q = _silu(hq_ref[:, sl])
        v = hi_ref[:, sl]
        vb = v.astype(BF16)
        k = kk[:, sl]
        b = bc[:, sl]
        att = None
        for i, s in enumerate(levels):
            qs, ks = _level_operands(q, k, jnp.exp(ex[i][:, sl]), s)
            part = _dot_nt(qs.astype(BF16), _pair_block_diag(ks.astype(BF16)))
            if 2 * s < rows:
                part = part * m_ref[i]
            att = part if att is None else att + part
        qk = q * k
        diag = jnp.concatenate(
            [jnp.broadcast_to(jnp.sum(qk[:, :dk], axis=-1, keepdims=True), (rows, dk)),
             jnp.broadcast_to(jnp.sum(qk[:, dk:], axis=-1, keepdims=True), (rows, dk))], axis=1)
        o = _dot(att.astype(BF16), _pair_block_diag(vb)) + diag * v
        qd = (q * jnp.exp(b)).astype(BF16)
        inter = []
        for sq in range(nsq):
            r = slice(sq * cs, (sq + 1) * cs)
            ia = sq * HG_HEADS + 2 * hp
            sta, stb = st_ref[ia], st_ref[ia + 1]
            zero = jnp.zeros((dk, dk), BF16)
            st2 = jnp.concatenate(
                [jnp.concatenate([sta.astype(BF16), zero], axis=1),
                 jnp.concatenate([zero, stb.astype(BF16)], axis=1)], axis=0)
            inter.append(_dot_nt(qd[r], st2))
            blast = b[(sq + 1) * cs - 1:(sq + 1) * cs, :]
            kd = (k[r] * jnp.exp(blast - b[r])).astype(BF16)
            upd = _dot_tn(vb[r], kd)
            decay = jnp.exp(blast)
            st_ref[ia] = sta * decay[:, :dk] + upd[:dk, :dk]
            st_ref[ia + 1] = stb * decay[:, dk:] + upd[dk:, dk:]
        o = o + (inter[0] if nsq == 1 else jnp.concatenate(inter, axis=0))
        on2 = jnp.concatenate([_rms(o[:, :dk], on), _rms(o[:, dk:], on)], axis=1)
        yb_ref[:, sl] = (on2 * _silu(hg_ref[:, sl])).astype(yb_ref.dtype)

    @pl.when(c == pl.num_programs(1) - 1)
    def _():
        for sq in range(nsq):
            for h in range(HG_HEADS):
                sout_ref[sq, h] = st_ref[sq * HG_HEADS + h].T


def _hgrn(p_all, lb_logits, o_norm, s0, *, nseq, t, cs, sb, layer, s0_layer):
    assert sb == 1 or cs == t
    n = nseq * t
    nc = t // cs
    rows = sb * cs
    nlev = len(_hgrn_levels(cs))
    nmat = sum(1 for s in _hgrn_levels(cs) if s < _SUBLANES)

    def slab(col):
        return pl.BlockSpec((rows, _SLAB), lambda b, i: (b * nc + i, col))

    st_block = (sb, HG_HEADS, HG_DK, HG_DV)
    return pl.pallas_call(
        functools.partial(_hgrn_kernel, layer=layer, cs=cs),
        grid=(nseq // sb, nc),
        in_specs=[
            slab(_C_HQ), slab(_C_HF), slab(_C_HI), slab(_C_HG),
            _resident(lb_logits.shape),
            _layer_resident(o_norm, layer),
            pl.BlockSpec((None,) + st_block, lambda b, i: (s0_layer, b, 0, 0, 0)),
        ],
        out_specs=[pl.BlockSpec((rows, _SLAB), lambda b, i: (b * nc + i, 0)),
                   pl.BlockSpec(st_block, lambda b, i: (b, 0, 0, 0))],
        out_shape=[jax.ShapeDtypeStruct((n, _SLAB), BF16),
                   jax.ShapeDtypeStruct((nseq, HG_HEADS, HG_DK, HG_DV), F32)],
        scratch_shapes=[
            pltpu.VMEM(((nmat + 1) * rows, 3 * rows), BF16),
            pltpu.VMEM((nlev, rows, 2 * rows), F32),
            pltpu.VMEM((sb * HG_HEADS, HG_DV, HG_DK), F32),
        ],
        compiler_params=_cparams(("parallel", "arbitrary")),
        name="hgrn_t%d" % cs,
    )(p_all, p_all, p_all, p_all, lb_logits, o_norm, s0)


_PAD = 8


def _lru_kernel(x_ref, g_ref, win_ref, whg_ref, cw_ref, cb_ref, wa_ref, ba_ref, wx_ref, bx_ref,
                lam_ref, cbuf_ref, h0_ref, yc_ref, p_ref, nbuf_ref, hlast_ref,
                xe_ref, hs_ref, hc_ref):
    t = pl.program_id(1)
    rows, w = x_ref.shape
    nsq = xe_ref.shape[0]
    tt = rows // nsq
    hist = CONV_WIDTH - 1
    hn = _rms(x_ref[...], g_ref[...]).astype(BF16)

    @pl.when(t == 0)
    def _():
        for s in range(nsq):
            xe_ref[s, _PAD - hist:_PAD, :] = cbuf_ref[s]
            hc_ref[s] = h0_ref[s]

    @pl.when(t > 0)
    def _():
        for s in range(nsq):
            xe_ref[s, _PAD - hist:_PAD, :] = xe_ref[s, _PAD + tt - hist:_PAD + tt, :]

    ngroups = w // MXU_DIM
    pslab = p_ref.shape[1] // ngroups
    sub = lax.broadcasted_iota(jnp.int32, (_SUBLANES, 1), 0)

    def project(g):
        return (_dot(hn, win_ref[:, g * MXU_DIM:(g + 1) * MXU_DIM]),
                _dot(hn, win_ref[:, w + g * MXU_DIM:w + (g + 1) * MXU_DIM]))

    nxt = project(0)
    for g in range(ngroups):
        cols = slice(g * MXU_DIM, (g + 1) * MXU_DIM)
        lx, lgv = nxt
        if g + 1 < ngroups:
            nxt = project(g + 1)
        xcs = []
        for s in range(nsq):
            xe_ref[s, _PAD:_PAD + tt, cols] = lx[s * tt:(s + 1) * tt]
            xc = xe_ref[s, _PAD - hist:_PAD - hist + tt, cols] * cw_ref[0:1, cols]
            for j in range(1, CONV_WIDTH):
                xc = xc + (xe_ref[s, _PAD - hist + j:_PAD - hist + j + tt, cols]
                           * cw_ref[j:j + 1, cols])
            xcs.append(xc)
        xc = (xcs[0] if nsq == 1 else jnp.concatenate(xcs, axis=0)) + cb_ref[:, cols]
        gelu = 0.5 * lgv * (
            1.0 + jnp.tanh(np.sqrt(2.0 / np.pi) * (lgv + 0.044715 * (lgv * lgv * lgv))))
        xg = xc.astype(BF16)
        ra = _dot(xg, wa_ref[g])
        rx = _dot(xg, wx_ref[g])
        p_ref[:, g * pslab:(g + 1) * pslab] = _dot(hn, whg_ref[:, g * pslab:(g + 1) * pslab])
        r = _sigmoid(ra + ba_ref[:, cols])
        ig = _sigmoid(rx + bx_ref[:, cols])
        nl = -lam_ref[:, cols]
        softplus = jnp.maximum(nl, 0.0) + jnp.log1p(jnp.exp(-jnp.abs(nl)))
        log_a = -LRU_C * r * softplus
        a = jnp.exp(log_a)
        u = -jnp.tanh(log_a) * (a * a + 1.0)
        bv = jnp.where(u > 0.0, u * lax.rsqrt(u), 0.0) * (ig * xc)

        for s in range(nsq):
            h = jnp.broadcast_to(hc_ref[s, :, cols], (_SUBLANES, MXU_DIM))
            for r0 in range(s * tt, (s + 1) * tt, _SUBLANES):
                am = a[r0:r0 + _SUBLANES]
                bm = bv[r0:r0 + _SUBLANES]
                d = 1
                while d < _SUBLANES:
                    seen = sub >= d
                    a_prev = jnp.where(seen, pltpu.roll(am, d, 0), 1.0)
                    b_prev = jnp.where(seen, pltpu.roll(bm, d, 0), 0.0)
                    bm = am * b_prev + bm
                    am = am * a_prev
                    d *= 2
                hs = am * h + bm
                hs_ref[r0:r0 + _SUBLANES, cols] = hs
                h = jnp.broadcast_to(hs[_SUBLANES - 1:, :], hs.shape)
            hc_ref[s, :, cols] = h[:1, :]
        yc_ref[:, cols] = (hs_ref[:, cols] * gelu).astype(yc_ref.dtype)

    @pl.when(t == pl.num_programs(1) - 1)
    def _():
        for s in range(nsq):
            nbuf_ref[s] = xe_ref[s, _PAD + tt - hist:_PAD + tt, :]
            hlast_ref[s] = hc_ref[s]


def _lru(x, params, cbuf, h0, *, nseq, t, tt, sb, layer, state_layer):
    assert sb == 1 or tt == t
    n, w = x.shape
    nt = t // tt
    rows = sb * tt
    hist = CONV_WIDTH - 1
    pcols = params[2].shape[2]
    return pl.pallas_call(
        _lru_kernel,
        grid=(nseq // sb, nt),
        in_specs=[
            pl.BlockSpec((rows, w), lambda b, i: (b * nt + i, 0)),
            *[_layer_resident(p, layer) for p in params],
            pl.BlockSpec((None, sb, hist, w), lambda b, i: (state_layer, b, 0, 0)),
            pl.BlockSpec((None, sb, 1, w), lambda b, i: (state_layer, b, 0, 0)),
        ],
        out_specs=[
            pl.BlockSpec((rows, w), lambda b, i: (b * nt + i, 0)),
            pl.BlockSpec((rows, pcols), lambda b, i: (b * nt + i, 0)),
            pl.BlockSpec((sb, hist, w), lambda b, i: (b, 0, 0)),
            pl.BlockSpec((sb, 1, w), lambda b, i: (b, 0, 0)),
        ],
        out_shape=[jax.ShapeDtypeStruct((n, w), BF16),
                   jax.ShapeDtypeStruct((n, pcols), F32),
                   jax.ShapeDtypeStruct((nseq, hist, w), F32),
                   jax.ShapeDtypeStruct((nseq, 1, w), F32)],
        scratch_shapes=[
            pltpu.VMEM((sb, _PAD + tt, w), F32),
            pltpu.VMEM((rows, w), F32),
            pltpu.VMEM((sb, 1, w), F32),
        ],
        compiler_params=_cparams(("parallel", "arbitrary")),
        name="lru_t%d" % tt,
    )(x, *params, cbuf, h0)


def _merge_kernel(x_ref, ya_ref, yb_ref, yc_ref, g_ref, wg_ref,
                  wa_ref, wb_ref, wc_ref, wo_ref, o_ref):
    x = x_ref[...]
    d = x.shape[1]
    hn = _rms(x, g_ref[...]).astype(BF16)
    m = None
    for i, (y_ref, w_ref) in enumerate(((ya_ref, wa_ref), (yb_ref, wb_ref), (yc_ref, wc_ref))):
        gate = _sigmoid(_dot(hn, wg_ref[:, i * d:(i + 1) * d]))
        part = gate * _dot(y_ref[...], w_ref[...])
        m = part if m is None else m + part
    o_ref[...] = x + _dot(m.astype(BF16), wo_ref[...])


def _merge(x, ya, yb, yc, params, tm, layer):
    n, d = x.shape
    row = pl.BlockSpec((tm, d), lambda i: (i, 0))
    return pl.pallas_call(
        _merge_kernel,
        grid=(n // tm,),
        in_specs=[row, row, row, row, *[_layer_resident(p, layer) for p in params]],
        out_specs=row,
        out_shape=jax.ShapeDtypeStruct((n, d), F32),
        compiler_params=_cparams(("parallel",)),
        name="merge",
    )(x, ya, yb, yc, *params)


def _block_diag(w):
    nl, nb, bw, _ = w.shape
    per = MXU_DIM // bw
    wg = w.reshape(nl, nb // per, per, bw, bw)
    eye = jnp.eye(per, dtype=w.dtype)
    out = jnp.einsum('lgpij,pq->lgpiqj', wg, eye)
    return out.reshape(nl, nb // per, MXU_DIM, MXU_DIM).astype(BF16)


def _prepare_params(w):
    def vec(name):
        return w[name][:, None, :]

    def bf16(name):
        return w[name].astype(BF16)

    w_in = w['w_in']
    c_hgrn = _QKV_ROWS
    c_lru = c_hgrn + _N_HGRN_SLABS * _SLAB
    c_gate = c_lru + _N_LRU_SLABS * _SLAB
    assert c_gate + _N_GATE_SLABS * _SLAB == w_in.shape[2]
    qk_gain =jnp.concatenate([jnp.tile(w['q_norm'] * SCALE, (1, N_HEADS)),
                               jnp.tile(w['k_norm'], (1, N_KV_HEADS))], axis=1)
    return {
        'norm_ffn1': vec('norm_ffn1'), 'norm_mix': vec('norm_mix'), 'norm_ffn2': vec('norm_ffn2'),
        'ffn1': (bf16('w_ffn1_up'), bf16('w_ffn1_down')),
        'ffn2': (bf16('w_ffn2_up'), bf16('w_ffn2_down')),
        'w_qkv_t': jnp.swapaxes(w_in[:, :, :_QKV_ROWS], 1, 2).astype(BF16),
        'qk_gain_col': qk_gain[:, :, None],
        'attn_sinks': w['attn_sinks'],
        'merge': (vec('norm_mix'), w_in[:, :, c_gate:].astype(BF16),
                  bf16('w_attn_o'), bf16('w_hgrn_o'), bf16('w_lru_o'), bf16('w_out')),
        'hgrn_lb_logits': w['hgrn_lb_logits'], 'hgrn_o_norm': vec('hgrn_o_norm'),
        'lru': (vec('norm_mix'), w_in[:, :, c_lru:c_gate].astype(BF16),
                w_in[:, :, c_hgrn:c_lru].astype(BF16), w['conv_w'], vec('conv_b'), _block_diag(w['lru_w_a']), vec('lru_b_a'),
                _block_diag(w['lru_w_x']), vec('lru_b_x'), vec('lru_lambda')),
    }


def _tiles(n, prefer):
    for tm in prefer:
        if n % tm == 0:
            return tm
    return n


def _layer(x, pp, cache, *, nseq, t, layer, prompt):
    n, d = x.shape
    tm = _tiles(n, (512, 256, 128))
    x = _ffn(x, pp['norm_ffn1'], *pp['ffn1'], tm, layer)
    ck, cv, s0, cbuf, h0 = cache
    state_layer = 0 if prompt else layer
    qkv_t = _qkv_t(x, pp['norm_mix'], pp['w_qkv_t'], pp['qk_gain_col'],
                   _tiles(n, (1024, 512, 256, 128)), layer)
    sinks = pp['attn_sinks'][layer]
    nq = N_HEADS * HEAD_DIM
    if prompt:
        ya = _attn_prompt(qkv_t, sinks, nseq=nseq, t=t, tq=_tiles(t, (512, 256, 128)))
        last = jnp.stack([qkv_t[nq:, (b + 1) * t - WINDOW:(b + 1) * t] for b in range(nseq)])
        cs = _tiles(t, (128, 64, 32, 16))
        tt = _tiles(t, (512, 256, 128, 64, 32, 16))
        sb = 1
    else:
        ya = _attn_sample(qkv_t, ck, cv, sinks, nseq=nseq, t=t, layer=layer)
        last = qkv_t[nq:, :].reshape(2 * _KV_W, nseq, t).transpose(1, 0, 2)
        cs = tt = t
        sb = _tiles(nseq, (4, 2, 1))
    last = last.reshape(nseq, 2, N_KV_HEADS, HEAD_DIM, -1).transpose(1, 0, 4, 2, 3)
    yc, p_all, nbuf, hlast = _lru(x, pp['lru'], cbuf, h0, nseq=nseq, t=t, tt=tt,
                                  sb=1 if prompt else _tiles(nseq, (8, 4, 2, 1)), layer=layer,
                                  state_layer=state_layer)
    yb, s_new = _hgrn(p_all, pp['hgrn_lb_logits'], pp['hgrn_o_norm'], s0, nseq=nseq, t=t, cs=cs,
                      sb=sb, layer=layer, s0_layer=state_layer)
    x = _merge(x, ya, yb, yc, pp['merge'], tm, layer)
    x = _ffn(x, pp['norm_ffn2'], *pp['ffn2'], tm, layer)
    return x, (last[0], last[1], s_new, nbuf, hlast.reshape(nseq, d))


def _forward(x_prompt, x_sample, cache_attn_k, cache_attn_v, state_hgrn, state_conv, state_lru, w):
    bsz, seq, d = x_prompt.shape
    dbsz, dseq, _ = x_sample.shape
    depth = w['w_in'].shape[0]
    win_rows = cache_attn_k.shape[2]
    assert win_rows == WINDOW, "the attention kernels assume a full cached window"
    pp = _prepare_params(w)
    xp = x_prompt.reshape(bsz * seq, d)
    xs = x_sample.reshape(dbsz * dseq, d)
    zero_cache = (None, None,
                  jnp.zeros((1, bsz, HG_HEADS, HG_DK, HG_DV), F32),
                  jnp.zeros((1, bsz, CONV_WIDTH - 1, d), F32),
                  jnp.zeros((1, bsz, 1, d), F32))
    cache = (cache_attn_k.reshape(depth, dbsz * win_rows, _KV_W),
             cache_attn_v.reshape(depth, dbsz * win_rows, _KV_W),
             state_hgrn, state_conv, state_lru[:, :, None, :])
    st_p, st_s = [], []
    for l in range(depth):
        xp, sp = _layer(xp, pp, zero_cache, nseq=bsz, t=seq, layer=l, prompt=True)
        xs, ss = _layer(xs, pp, cache, nseq=dbsz, t=dseq, layer=l, prompt=False)
        st_p.append(sp)
        st_s.append(ss)

    def stack(sts, i):
        return jnp.stack([s[i] for s in sts], axis=0)

    return (xp.reshape(bsz, seq, d), xs.reshape(dbsz, dseq, d),
            stack(st_p, 0), stack(st_p, 1), stack(st_p, 2), stack(st_p, 3), stack(st_p, 4),
            stack(st_s, 0), stack(st_s, 1), stack(st_s, 2), stack(st_s, 3), stack(st_s, 4))


def kernel(x_prompt, x_sample, cache_attn_k, cache_attn_v, state_hgrn, state_conv, state_lru,
           norm_ffn1, w_ffn1_up, w_ffn1_down, norm_mix, w_in, q_norm, k_norm, attn_sinks, w_attn_o,
           hgrn_lb_logits, hgrn_o_norm, w_hgrn_o, conv_w, conv_b, lru_w_a, lru_b_a, lru_w_x, lru_b_x,
           lru_lambda, w_lru_o, w_out, norm_ffn2, w_ffn2_up, w_ffn2_down):
    w = dict(norm_ffn1=norm_ffn1, w_ffn1_up=w_ffn1_up, w_ffn1_down=w_ffn1_down, norm_mix=norm_mix,
             w_in=w_in, q_norm=q_norm, k_norm=k_norm, attn_sinks=attn_sinks, w_attn_o=w_attn_o,
             hgrn_lb_logits=hgrn_lb_logits, hgrn_o_norm=hgrn_o_norm, w_hgrn_o=w_hgrn_o,
             conv_w=conv_w, conv_b=conv_b, lru_w_a=lru_w_a, lru_b_a=lru_b_a, lru_w_x=lru_w_x,
             lru_b_x=lru_b_x, lru_lambda=lru_lambda, w_lru_o=w_lru_o, w_out=w_out,
             norm_ffn2=norm_ffn2, w_ffn2_up=w_ffn2_up, w_ffn2_down=w_ffn2_down)
    return _forward(x_prompt, x_sample, cache_attn_k, cache_attn_v, state_hgrn, state_conv,
                    state_lru, w)
```

```python
import functools

import numpy as np
import jax
import jax.numpy as jnp
from jax import lax
from jax.experimental import pallas as pl
from jax.experimental.pallas import tpu as pltpu

F32 = jnp.float32
BF16 = jnp.bfloat16

CHUNK = 64
N_HEADS = 16
N_KV_HEADS = 4
HEAD_DIM = 64
GROUP = N_HEADS // N_KV_HEADS
WINDOW = 128
SCALE = HEAD_DIM ** -0.5
HG_HEADS = 8
HG_DK = 128
HG_DV = 128
LOG_F_FLOOR = -60.0
CONV_WIDTH = 4
LRU_C = 8.0
LRU_BW = 64
EPS = 1e-6
NEG = -1e30

MXU_DIM = 256
VMEM_LIMIT_BYTES = 56 * 1024 * 1024

_SLAB = 1024
(_C_HQ, _C_HF, _C_HI, _C_HG) = range(4)
_N_HGRN_SLABS, _N_LRU_SLABS, _N_GATE_SLABS = 4, 2, 3
_KV_W = N_KV_HEADS * HEAD_DIM
_QKV_ROWS = N_HEADS * HEAD_DIM + 2 * _KV_W


def _cparams(sem):
    return pltpu.CompilerParams(dimension_semantics=sem, vmem_limit_bytes=VMEM_LIMIT_BYTES)


def _rms(x, g):
    return x * lax.rsqrt(jnp.mean(x * x, axis=-1, keepdims=True) + EPS) * g


def _sigmoid(x):
    return 1.0 / (1.0 + jnp.exp(-x))


def _silu(x):
    return x * _sigmoid(x)


def _dot(a, b):
    return jnp.dot(a, b, preferred_element_type=F32)


def _dot_nt(a, b):
    return lax.dot_general(a, b, (((1,), (1,)), ((), ())), preferred_element_type=F32)


def _dot_tn(a, b):
    return lax.dot_general(a, b, (((0,), (0,)), ((), ())), preferred_element_type=F32)


def _ff_chunks(dff, width):
    assert dff % MXU_DIM == 0
    edges = list(range(0, dff, width)) + [dff]
    return list(zip(edges[:-1], edges[1:]))


def _ffn_kernel(x_ref, g_ref, wu_ref, wd_ref, o_ref, *, chunks):
    x = x_ref[...]
    dff = wd_ref.shape[0]
    h = _rms(x, g_ref[...]).astype(BF16)
    acc = None
    for lo, hi in chunks:
        gate = _dot(h, wu_ref[:, lo:hi])
        val = _dot(h, wu_ref[:, dff + lo:dff + hi])
        part = _dot((_silu(gate) * val).astype(BF16), wd_ref[lo:hi, :])
        acc = part if acc is None else acc + part
    o_ref[...] = x + 0.5 * acc


def _resident(shape):
    return pl.BlockSpec(shape, lambda *_: (0,) * len(shape), pipeline_mode=pl.Buffered(1))


def _layer_resident(stacked, layer):
    shape = stacked.shape[1:]
    return pl.BlockSpec((None,) + shape, lambda *_: (layer,) + (0,) * len(shape),
                        pipeline_mode=pl.Buffered(1))


def _ffn(x, g, w_up, w_down, tm, layer):
    n, d = x.shape
    dff = w_down.shape[1]
    return pl.pallas_call(
        functools.partial(_ffn_kernel, chunks=_ff_chunks(dff, 6 * MXU_DIM)),
        grid=(n // tm,),
        in_specs=[
            pl.BlockSpec((tm, d), lambda i: (i, 0)),
            _layer_resident(g, layer),
            _layer_resident(w_up, layer),
            _layer_resident(w_down, layer),
        ],
        out_specs=pl.BlockSpec((tm, d), lambda i: (i, 0)),
        out_shape=jax.ShapeDtypeStruct((n, d), F32),
        compiler_params=_cparams(("parallel",)),
        name="ffn",
    )(x, g, w_up, w_down)


_LANES = 128


def _softmax_pv(s, sk, vt):
    m = jnp.maximum(jnp.max(s, axis=0, keepdims=True), sk)
    e = jnp.exp(s - m)
    den = jnp.sum(e, axis=0, keepdims=True) + jnp.exp(sk - m)
    return _dot(vt, (e * (1.0 / den)).astype(BF16))


def _group_queries(q_ref, j, lanes):
    qcat = jnp.concatenate(
        [q_ref[(j * GROUP + g) * HEAD_DIM:(j * GROUP + g + 1) * HEAD_DIM, lanes]
         for g in range(GROUP)], axis=1).astype(BF16)
    zero = jnp.zeros_like(qcat)
    return jnp.concatenate([qcat if i == j else zero for i in range(N_KV_HEADS)], axis=0)


def _sink_row(sinks_ref, j, width):
    return jnp.concatenate(
        [jnp.full((1, width), sinks_ref[j * GROUP + g], F32) for g in range(GROUP)], axis=1)


def _attn_sample_kernel(qkv_ref, kp_ref, vp_ref, sinks_ref, x_ref, g_ref, whg_ref,
                        ya_ref, p_ref, yat_ref, *, t):
    nq = N_HEADS * HEAD_DIM
    hn = _rms(x_ref[...], g_ref[...]).astype(BF16)
    pcols = p_ref.shape[1] // N_KV_HEADS
    ncache = kp_ref.shape[0]
    keys = ncache + _LANES
    kfull = jnp.concatenate([kp_ref[...], qkv_ref[nq:nq + _KV_W, :].T], axis=0).astype(BF16)
    vt = jnp.concatenate([vp_ref[...].T, qkv_ref[nq + _KV_W:, :]], axis=1).astype(BF16)
    row = lax.broadcasted_iota(jnp.int32, (keys, 1), 0)
    key_seq = jnp.where(row < ncache, row // WINDOW, (row - ncache) // t)
    lane = lax.broadcasted_iota(jnp.int32, (1, GROUP * _LANES), 1)
    valid = key_seq == (lane % _LANES) // t
    for j in range(N_KV_HEADS):
        s = jnp.where(valid, _dot(kfull, _group_queries(qkv_ref, j, slice(None))), NEG)
        o = _softmax_pv(s, _sink_row(sinks_ref, j, _LANES), vt[j * HEAD_DIM:(j + 1) * HEAD_DIM])
        for g in range(GROUP):
            yat_ref[(j * GROUP + g) * HEAD_DIM:(j * GROUP + g + 1) * HEAD_DIM, :] = (
                o[:, g * _LANES:(g + 1) * _LANES])
        pc = slice(j * pcols, (j + 1) * pcols)
        p_ref[:, pc] = _dot(hn, whg_ref[:, pc])
    ya_ref[...] = yat_ref[...].T.astype(ya_ref.dtype)


def _attn_sample(qkv_t, cache_k, cache_v, sinks, x, g, whg, *, nseq, t, layer):
    n = nseq * t
    nq = N_HEADS * HEAD_DIM
    sb = _LANES // t
    d = x.shape[1]
    pcols = whg.shape[2]
    prev = pl.BlockSpec((None, sb * WINDOW, _KV_W), lambda b: (layer, b, 0))
    return pl.pallas_call(
        functools.partial(_attn_sample_kernel, t=t),
        grid=(nseq // sb,),
        in_specs=[
            pl.BlockSpec((_QKV_ROWS, _LANES), lambda b: (0, b)),
            prev, prev,
            pl.BlockSpec(memory_space=pltpu.SMEM),
            pl.BlockSpec((_LANES, d), lambda b: (b, 0)),
            _layer_resident(g, layer),
            _layer_resident(whg, layer),
        ],
        out_specs=[pl.BlockSpec((_LANES, nq), lambda b: (b, 0)),
                   pl.BlockSpec((_LANES, pcols), lambda b: (b, 0))],
        out_shape=[jax.ShapeDtypeStruct((n, nq), BF16),
                   jax.ShapeDtypeStruct((n, pcols), F32)],
        scratch_shapes=[pltpu.VMEM((nq, _LANES), F32)],
        compiler_params=_cparams(("parallel",)),
        name="attn_sample",
    )(qkv_t, cache_k, cache_v, sinks, x, g, whg)


def _qkv_t_kernel(x_ref, g_ref, w_ref, gc_ref, o_ref):
    h = _rms(x_ref[...], g_ref[...]).astype(BF16)
    acc = _dot_nt(w_ref[...], h)
    nqk = gc_ref.shape[0]
    for r in range(0, nqk, HEAD_DIM):
        blk = acc[r:r + HEAD_DIM]
        ms = jnp.sum(blk * blk, axis=0, keepdims=True) * (1.0 / HEAD_DIM)
        o_ref[r:r + HEAD_DIM, :] = blk * lax.rsqrt(ms + EPS) * gc_ref[r:r + HEAD_DIM, :]
    o_ref[nqk:, :] = acc[nqk:]


def _qkv_t(x, g, w_t, gain_col, tm, layer):
    n, d = x.shape
    rows = w_t.shape[1]
    return pl.pallas_call(
        _qkv_t_kernel,
        grid=(n // tm,),
        in_specs=[
            pl.BlockSpec((tm, d), lambda i: (i, 0)),
            _layer_resident(g, layer),
            _layer_resident(w_t, layer),
            _layer_resident(gain_col, layer),
        ],
        out_specs=pl.BlockSpec((rows, tm), lambda i: (0, i)),
        out_shape=jax.ShapeDtypeStruct((rows, n), F32),
        compiler_params=_cparams(("parallel",)),
        name="qkv_t",
    )(x, g, w_t, gain_col)


_PAIR = 2 * CHUNK
_KEYS = _PAIR + WINDOW


def _attn_prompt_kernel(q_ref, kc_ref, vc_ref, kp_ref, vp_ref, sinks_ref, x_ref, g_ref, whg_ref,
                        ya_ref, p_ref, yat_ref):
    tq = q_ref.shape[1]
    t = pl.program_id(1)
    hn = _rms(x_ref[...], g_ref[...]).astype(BF16)
    pcols = p_ref.shape[1] // ((tq // _PAIR) * N_KV_HEADS)
    lane = lax.broadcasted_iota(jnp.int32, (1, GROUP * _PAIR), 1)
    first_chunk = (lane % _PAIR) < CHUNK
    k_cur = kc_ref[...].T.astype(BF16)
    k_prev = kp_ref[...].T.astype(BF16)
    for p in range(tq // _PAIR):
        lo = p * _PAIR
        if p == 0:
            kb = jnp.concatenate([k_prev, k_cur[:_PAIR]], axis=0)
        else:
            kb = k_cur[lo - WINDOW:lo + _PAIR]
        first_key_chunk = t * (tq // CHUNK) + (lo - WINDOW) // CHUNK
        for j in range(N_KV_HEADS):
            rows = slice(j * HEAD_DIM, (j + 1) * HEAD_DIM)
            if p == 0:
                vb = jnp.concatenate([vp_ref[rows, :], vc_ref[rows, :_PAIR]], axis=1)
            else:
                vb = vc_ref[rows, lo - WINDOW:lo + _PAIR]
            s = _dot(kb, _group_queries(q_ref, j, slice(lo, lo + _PAIR)))
            s = jnp.concatenate([
                jnp.where(first_chunk & (first_key_chunk >= 0), s[:CHUNK], NEG),
                jnp.where(first_key_chunk + 1 >= 0, s[CHUNK:2 * CHUNK], NEG),
                s[2 * CHUNK:3 * CHUNK],
                jnp.where(first_chunk, NEG, s[3 * CHUNK:]),
            ], axis=0)
            o = _softmax_pv(s, _sink_row(sinks_ref, j, _PAIR), vb.astype(BF16))
            for g in range(GROUP):
                yat_ref[(j * GROUP + g) * HEAD_DIM:(j * GROUP + g + 1) * HEAD_DIM, lo:lo + _PAIR] = (
                    o[:, g * _PAIR:(g + 1) * _PAIR])
            pc = slice((p * N_KV_HEADS + j) * pcols, (p * N_KV_HEADS + j + 1) * pcols)
            p_ref[:, pc] = _dot(hn, whg_ref[:, pc])
    ya_ref[...] = yat_ref[...].T.astype(ya_ref.dtype)


def _attn_prompt(qkv_t, sinks, x, g, whg, *, nseq, t, tq, layer):
    n = nseq * t
    nt = t // tq
    nq = N_HEADS * HEAD_DIM
    kblk, vblk = nq // _KV_W, nq // _KV_W + 1
    d = x.shape[1]
    pcols = whg.shape[2]

    def prev_map(blk):
        return lambda b, i: (blk, jnp.maximum(b * (t // WINDOW) + i * (tq // WINDOW) - 1, 0))

    return pl.pallas_call(
        _attn_prompt_kernel,
        grid=(nseq, nt),
        in_specs=[
            pl.BlockSpec((nq, tq), lambda b, i: (0, b * nt + i)),
            pl.BlockSpec((_KV_W, tq), lambda b, i: (kblk, b * nt + i)),
            pl.BlockSpec((_KV_W, tq), lambda b, i: (vblk, b * nt + i)),
            pl.BlockSpec((_KV_W, WINDOW), prev_map(kblk)),
            pl.BlockSpec((_KV_W, WINDOW), prev_map(vblk)),
            pl.BlockSpec(memory_space=pltpu.SMEM),
            pl.BlockSpec((tq, d), lambda b, i: (b * nt + i, 0)),
            _layer_resident(g, layer),
            _layer_resident(whg, layer),
        ],
        out_specs=[pl.BlockSpec((tq, nq), lambda b, i: (b * nt + i, 0)),
                   pl.BlockSpec((tq, pcols), lambda b, i: (b * nt + i, 0))],
        out_shape=[jax.ShapeDtypeStruct((n, nq), BF16),
                   jax.ShapeDtypeStruct((n, pcols), F32)],
        scratch_shapes=[pltpu.VMEM((nq, tq), F32)],
        compiler_params=_cparams(("parallel", "arbitrary")),
        name="attn_prompt",
    )(qkv_t, qkv_t, qkv_t, qkv_t, qkv_t, sinks, x, g, whg)


def _hgrn_levels(c):
    return [c >> (i + 1) for i in range(int(np.log2(c)))]


_SUBLANES = 8


def _level_operands(q, k, g, s):
    rows = q.shape[0]
    if s < _SUBLANES:
        upper = (lax.broadcasted_iota(jnp.int32, (rows, 1), 0) & s) != 0
        return jnp.where(upper, q * g, 0.0), jnp.where(upper, 0.0, k * g)
    zero = jnp.zeros((s, q.shape[1]), F32)
    qparts, kparts = [], []
    for g0 in range(0, rows, 2 * s):
        lo, up = slice(g0, g0 + s), slice(g0 + s, g0 + 2 * s)
        kparts += [k[lo] * g[lo], zero]
        qparts += [zero, q[up] * g[up]]
    return jnp.concatenate(qparts, axis=0), jnp.concatenate(kparts, axis=0)


def _pair_block_diag(x):
    c = x.shape[1] // 2
    zero = jnp.zeros((x.shape[0], c), x.dtype)
    return jnp.concatenate([jnp.concatenate([x[:, :c], zero], axis=1),
                            jnp.concatenate([zero, x[:, c:]], axis=1)], axis=0)


def _hgrn_kernel(hq_ref, hf_ref, hi_ref, hg_ref, lbl_ref, on_ref, s0_ref, yb_ref, sout_ref,
                 a_ref, m_ref, st_ref, *, layer, cs):
    c = pl.program_id(1)
    rows = hq_ref.shape[0]
    nsq = rows // cs
    levels = _hgrn_levels(cs)

    @pl.when(c == 0)
    def _():
        row = lax.broadcasted_iota(jnp.int32, (rows, rows), 0)
        col = lax.broadcasted_iota(jnp.int32, (rows, rows), 1)
        mats = [((col <= row) & ((row & ~(cs - 1)) == (col & ~(cs - 1)))).astype(F32)]
        for i, s in enumerate(levels):
            mid = (row & ~(2 * s - 1)) + s
            upper = (row & s) != 0
            if s < _SUBLANES:
                up = ((col >= mid) & (col <= row)).astype(F32)
                lo = ((col > row) & (col < mid)).astype(F32)
                mats.append(jnp.where(upper, up, lo))
            same =(row & ~(2 * s - 1)) == (col & ~(2 * s - 1))
            same = same.astype(F32)
            m_ref[i] = jnp.concatenate([same, same], axis=1)
        for i, a in enumerate(mats):
            a = a.astype(BF16)
            a_ref[i * rows:(i + 1) * rows, :] = jnp.concatenate([a, a, a], axis=1)
        for sq in range(nsq):
            for h in range(HG_HEADS):
                st_ref[sq * HG_HEADS + h] = s0_ref[sq, h].T

    lg = lbl_ref[...]
    e = jnp.exp(lg - jnp.max(lg, axis=0, keepdims=True))
    lb = jnp.zeros((1, e.shape[1]), F32)
    for i in range(1, layer + 1):
        lb = lb + e[i:i + 1]
    lb = lb / jnp.sum(e, axis=0, keepdims=True)

    sg = _sigmoid(hf_ref[...])
    f = lb + (1.0 - lb) * sg
    logf = jnp.maximum(jnp.log(jnp.maximum(f, 1e-26)), LOG_F_FLOOR)
    kk = (1.0 - lb) * (1.0 - sg)
    l1 = logf.astype(BF16)
    r1 = logf - l1.astype(F32)
    l2 = r1.astype(BF16)
    l3 = (r1 - l2.astype(F32)).astype(BF16)
    pieces = jnp.concatenate([l1, l2, l3], axis=0)

    lin = _dot(a_ref[...], pieces)
    bc = lin[:rows]
    ex, nmat = [], 1
    for s in levels:
        if s < _SUBLANES:
            ex.append(lin[nmat * rows:(nmat + 1) * rows])
            nmat += 1
        else:
            ex.append(jnp.concatenate(
                [-jnp.abs(bc[g0:g0 + 2 * s] - bc[g0 + s - 1:g0 + s]) for g0 in range(0, rows, 2 * s)],
                axis=0))
    on = on_ref[...]

    dk = HG_DK
    for hp in range(HG_HEADS // 2):
        sl = slice(2 * hp * dk, 2 * (hp + 1) * dk)
        q = _silu(hq_ref[:, sl])
        v = hi_ref[:, sl]
        vb = v.astype(BF16)
        k = kk[:, sl]
        b = bc[:, sl]
        att = None
        for i, s in enumerate(levels):
            qs, ks = _level_operands(q, k, jnp.exp(ex[i][:, sl]), s)
            part = _dot_nt(qs.astype(BF16), _pair_block_diag(ks.astype(BF16)))
            if 2 * s < rows:
                part = part * m_ref[i]
            att = part if att is None else att + part
        qk = q * k
        diag = jnp.concatenate(
            [jnp.broadcast_to(jnp.sum(qk[:, :dk], axis=-1, keepdims=True), (rows, dk)),
             jnp.broadcast_to(jnp.sum(qk[:, dk:], axis=-1, keepdims=True), (rows, dk))], axis=1)
        o = _dot(att.astype(BF16), _pair_block_diag(vb)) + diag * v
        qd = (q * jnp.exp(b)).astype(BF16)
        inter = []
        for sq in range(nsq):
            r = slice(sq * cs, (sq + 1) * cs)
            ia = sq * HG_HEADS + 2 * hp
            sta, stb = st_ref[ia], st_ref[ia + 1]
            zero = jnp.zeros((dk, dk), BF16)
            st2 = jnp.concatenate(
                [jnp.concatenate([sta.astype(BF16), zero], axis=1),
                 jnp.concatenate([zero, stb.astype(BF16)], axis=1)], axis=0)
            inter.append(_dot_nt(qd[r], st2))
            blast = b[(sq + 1) * cs - 1:(sq + 1) * cs, :]
            kd = (k[r] * jnp.exp(blast - b[r])).astype(BF16)
            upd = _dot_tn(vb[r], kd)
            decay = jnp.exp(blast)
            st_ref[ia] = sta * decay[:, :dk] + upd[:dk, :dk]
            st_ref[ia + 1] = stb * decay[:, dk:] + upd[dk:, dk:]
        o = o + (inter[0] if nsq == 1 else jnp.concatenate(inter, axis=0))
        on2 = jnp.concatenate([_rms(o[:, :dk], on), _rms(o[:, dk:], on)], axis=1)
        yb_ref[:, sl] = (on2 * _silu(hg_ref[:, sl])).astype(yb_ref.dtype)

    @pl.when(c == pl.num_programs(1) - 1)
    def _():
        for sq in range(nsq):
            for h in range(HG_HEADS):
                sout_ref[sq, h] = st_ref[sq * HG_HEADS + h].T


def _hgrn(p_all, lb_logits, o_norm, s0, *, nseq, t, cs, sb, layer, s0_layer):
    assert sb == 1 or cs == t
    n = nseq * t
    nc = t // cs
    rows = sb * cs
    nlev = len(_hgrn_levels(cs))
    nmat = sum(1 for s in _hgrn_levels(cs) if s < _SUBLANES)

    def slab(col):
        return pl.BlockSpec((rows, _SLAB), lambda b, i: (b * nc + i, col))

    st_block = (sb, HG_HEADS, HG_DK, HG_DV)
    return pl.pallas_call(
        functools.partial(_hgrn_kernel, layer=layer, cs=cs),
        grid=(nseq // sb, nc),
        in_specs=[
            slab(_C_HQ), slab(_C_HF), slab(_C_HI), slab(_C_HG),
            _resident(lb_logits.shape),
            _layer_resident(o_norm, layer),
            pl.BlockSpec((None,) + st_block, lambda b, i: (s0_layer, b, 0, 0, 0)),
        ],
        out_specs=[pl.BlockSpec((rows, _SLAB), lambda b, i: (b * nc + i, 0)),
                   pl.BlockSpec(st_block, lambda b, i: (b, 0, 0, 0))],
        out_shape=[jax.ShapeDtypeStruct((n, _SLAB), BF16),
                   jax.ShapeDtypeStruct((nseq, HG_HEADS, HG_DK, HG_DV), F32)],
        scratch_shapes=[
            pltpu.VMEM(((nmat + 1) * rows, 3 * rows), BF16),
            pltpu.VMEM((nlev, rows, 2 * rows), F32),
            pltpu.VMEM((sb * HG_HEADS, HG_DV, HG_DK), F32),
        ],
        compiler_params=_cparams(("parallel", "arbitrary")),
        name="hgrn_t%d" % cs,
    )(p_all, p_all, p_all, p_all, lb_logits, o_norm, s0)


_PAD = 8


def _lru_kernel(x_ref, g_ref, win_ref, cw_ref, cb_ref, wa_ref, ba_ref, wx_ref, bx_ref,
                lam_ref, cbuf_ref, h0_ref, yc_ref, nbuf_ref, hlast_ref,
                xe_ref, hs_ref, hc_ref):
    t = pl.program_id(1)
    rows, w = x_ref.shape
    nsq = xe_ref.shape[0]
    tt = rows // nsq
    hist = CONV_WIDTH - 1
    hn = _rms(x_ref[...], g_ref[...]).astype(BF16)

    @pl.when(t == 0)
    def _():
        for s in range(nsq):
            xe_ref[s, _PAD - hist:_PAD, :] = cbuf_ref[s]
            hc_ref[s] = h0_ref[s]

    @pl.when(t > 0)
    def _():
        for s in range(nsq):
            xe_ref[s, _PAD - hist:_PAD, :] = xe_ref[s, _PAD + tt - hist:_PAD + tt, :]

    ngroups = w // MXU_DIM
    sub =lax.broadcasted_iota(jnp.int32, (_SUBLANES, 1), 0)

    def project(g):
        return (_dot(hn, win_ref[:, g * MXU_DIM:(g + 1) * MXU_DIM]),
                _dot(hn, win_ref[:, w + g * MXU_DIM:w + (g + 1) * MXU_DIM]))

    nxt = project(0)
    for g in range(ngroups):
        cols = slice(g * MXU_DIM, (g + 1) * MXU_DIM)
        lx, lgv = nxt
        if g + 1 < ngroups:
            nxt = project(g + 1)
        xcs = []
        for s in range(nsq):
            xe_ref[s, _PAD:_PAD + tt, cols] = lx[s * tt:(s + 1) * tt]
            xc = xe_ref[s, _PAD - hist:_PAD - hist + tt, cols] * cw_ref[0:1, cols]
            for j in range(1, CONV_WIDTH):
                xc = xc + (xe_ref[s, _PAD - hist + j:_PAD - hist + j + tt, cols]
                           * cw_ref[j:j + 1, cols])
            xcs.append(xc)
        xc = (xcs[0] if nsq == 1 else jnp.concatenate(xcs, axis=0)) + cb_ref[:, cols]
        gelu = 0.5 * lgv * (
            1.0 + jnp.tanh(np.sqrt(2.0 / np.pi) * (lgv + 0.044715 * (lgv * lgv * lgv))))
        xg = xc.astype(BF16)
        ra = _dot(xg, wa_ref[g])
        rx = _dot(xg, wx_ref[g])
        r =_sigmoid(ra + ba_ref[:, cols])
        ig = _sigmoid(rx + bx_ref[:, cols])
        nl = -lam_ref[:, cols]
        softplus = jnp.maximum(nl, 0.0) + jnp.log1p(jnp.exp(-jnp.abs(nl)))
        log_a = -LRU_C * r * softplus
        a = jnp.exp(log_a)
        u = -jnp.tanh(log_a) * (a * a + 1.0)
        bv = jnp.where(u > 0.0, u * lax.rsqrt(u), 0.0) * (ig * xc)

        for s in range(nsq):
            h = jnp.broadcast_to(hc_ref[s, :, cols], (_SUBLANES, MXU_DIM))
            for r0 in range(s * tt, (s + 1) * tt, _SUBLANES):
                am = a[r0:r0 + _SUBLANES]
                bm = bv[r0:r0 + _SUBLANES]
                d = 1
                while d < _SUBLANES:
                    seen = sub >= d
                    a_prev = jnp.where(seen, pltpu.roll(am, d, 0), 1.0)
                    b_prev = jnp.where(seen, pltpu.roll(bm, d, 0), 0.0)
                    bm = am * b_prev + bm
                    am = am * a_prev
                    d *= 2
                hs = am * h + bm
                hs_ref[r0:r0 + _SUBLANES, cols] = hs
                h = jnp.broadcast_to(hs[_SUBLANES - 1:, :], hs.shape)
            hc_ref[s, :, cols] = h[:1, :]
        yc_ref[:, cols] = (hs_ref[:, cols] * gelu).astype(yc_ref.dtype)

    @pl.when(t == pl.num_programs(1) - 1)
    def _():
        for s in range(nsq):
            nbuf_ref[s] = xe_ref[s, _PAD + tt - hist:_PAD + tt, :]
            hlast_ref[s] = hc_ref[s]


def _lru(x, params, cbuf, h0, *, nseq, t, tt, sb, layer, state_layer):
    assert sb == 1 or tt == t
    n, w = x.shape
    nt = t // tt
    rows = sb * tt
    hist = CONV_WIDTH - 1
    return pl.pallas_call(
        _lru_kernel,
        grid=(nseq // sb, nt),
        in_specs=[
            pl.BlockSpec((rows, w), lambda b, i: (b * nt + i, 0)),
            *[_layer_resident(p, layer) for p in params],
            pl.BlockSpec((None, sb, hist, w), lambda b, i: (state_layer, b, 0, 0)),
            pl.BlockSpec((None, sb, 1, w), lambda b, i: (state_layer, b, 0, 0)),
        ],
        out_specs=[
            pl.BlockSpec((rows, w), lambda b, i: (b * nt + i, 0)),
            pl.BlockSpec((sb, hist, w), lambda b, i: (b, 0, 0)),
            pl.BlockSpec((sb, 1, w), lambda b, i: (b, 0, 0)),
        ],
        out_shape=[jax.ShapeDtypeStruct((n, w), BF16),
                   jax.ShapeDtypeStruct((nseq, hist, w), F32),
                   jax.ShapeDtypeStruct((nseq, 1, w), F32)],
        scratch_shapes=[
            pltpu.VMEM((sb, _PAD + tt, w), F32),
            pltpu.VMEM((rows, w), F32),
            pltpu.VMEM((sb, 1, w), F32),
        ],
        compiler_params=_cparams(("parallel", "arbitrary")),
        name="lru_t%d" % tt,
    )(x, *params, cbuf, h0)


def _merge_kernel(x_ref, ya_ref, yb_ref, yc_ref, g_ref, wg_ref,
                  wa_ref, wb_ref, wc_ref, wo_ref, o_ref):
    x = x_ref[...]
    d = x.shape[1]
    hn = _rms(x, g_ref[...]).astype(BF16)
    m = None
    for i, (y_ref, w_ref) in enumerate(((ya_ref, wa_ref), (yb_ref, wb_ref), (yc_ref, wc_ref))):
        gate = _sigmoid(_dot(hn, wg_ref[:, i * d:(i + 1) * d]))
        part = gate * _dot(y_ref[...], w_ref[...])
        m = part if m is None else m + part
    o_ref[...] = x + _dot(m.astype(BF16), wo_ref[...])


def _merge(x, ya, yb, yc, params, tm, layer):
    n, d = x.shape
    row = pl.BlockSpec((tm, d), lambda i: (i, 0))
    return pl.pallas_call(
        _merge_kernel,
        grid=(n // tm,),
        in_specs=[row, row, row, row, *[_layer_resident(p, layer) for p in params]],
        out_specs=row,
        out_shape=jax.ShapeDtypeStruct((n, d), F32),
        compiler_params=_cparams(("parallel",)),
        name="merge",
    )(x, ya, yb, yc, *params)


def _block_diag(w):
    nl, nb, bw, _ = w.shape
    per = MXU_DIM // bw
    wg = w.reshape(nl, nb // per, per, bw, bw)
    eye = jnp.eye(per, dtype=w.dtype)
    out = jnp.einsum('lgpij,pq->lgpiqj', wg, eye)
    return out.reshape(nl, nb // per, MXU_DIM, MXU_DIM).astype(BF16)


def _prepare_params(w):
    def vec(name):
        return w[name][:, None, :]

    def bf16(name):
        return w[name].astype(BF16)

    w_in = w['w_in']
    c_hgrn = _QKV_ROWS
    c_lru = c_hgrn + _N_HGRN_SLABS * _SLAB
    c_gate = c_lru + _N_LRU_SLABS * _SLAB
    assert c_gate + _N_GATE_SLABS * _SLAB == w_in.shape[2]
    qk_gain =jnp.concatenate([jnp.tile(w['q_norm'] * SCALE, (1, N_HEADS)),
                               jnp.tile(w['k_norm'], (1, N_KV_HEADS))], axis=1)
    return {
        'norm_ffn1': vec('norm_ffn1'), 'norm_mix': vec('norm_mix'), 'norm_ffn2': vec('norm_ffn2'),
        'ffn1': (bf16('w_ffn1_up'), bf16('w_ffn1_down')),
        'ffn2': (bf16('w_ffn2_up'), bf16('w_ffn2_down')),
        'w_qkv_t': jnp.swapaxes(w_in[:, :, :_QKV_ROWS], 1, 2).astype(BF16),
        'qk_gain_col': qk_gain[:, :, None],
        'attn_sinks': w['attn_sinks'],
        'merge': (vec('norm_mix'), w_in[:, :, c_gate:].astype(BF16),
                  bf16('w_attn_o'), bf16('w_hgrn_o'), bf16('w_lru_o'), bf16('w_out')),
        'hgrn_lb_logits': w['hgrn_lb_logits'], 'hgrn_o_norm': vec('hgrn_o_norm'),
        'w_hgrn_in': w_in[:, :, c_hgrn:c_lru].astype(BF16),
        'lru': (vec('norm_mix'), w_in[:, :, c_lru:c_gate].astype(BF16),
                w['conv_w'], vec('conv_b'), _block_diag(w['lru_w_a']), vec('lru_b_a'),
                _block_diag(w['lru_w_x']), vec('lru_b_x'), vec('lru_lambda')),
    }


def _tiles(n, prefer):
    for tm in prefer:
        if n % tm == 0:
            return tm
    return n


def _layer(x, pp, cache, *, nseq, t, layer, prompt):
    n, d = x.shape
    tm = _tiles(n, (512, 256, 128))
    x = _ffn(x, pp['norm_ffn1'], *pp['ffn1'], tm, layer)
    ck, cv, s0, cbuf, h0 = cache
    state_layer = 0 if prompt else layer
    qkv_t = _qkv_t(x, pp['norm_mix'], pp['w_qkv_t'], pp['qk_gain_col'],
                   _tiles(n, (1024, 512, 256, 128)), layer)
    sinks = pp['attn_sinks'][layer]
    nq = N_HEADS * HEAD_DIM
    hgrn_in = (x, pp['norm_mix'], pp['w_hgrn_in'])
    if prompt:
        ya, p_all = _attn_prompt(qkv_t, sinks, *hgrn_in, nseq=nseq, t=t,
                                 tq=_tiles(t, (512, 256, 128)), layer=layer)
        last = jnp.stack([qkv_t[nq:, (b + 1) * t - WINDOW:(b + 1) * t] for b in range(nseq)])
        cs = _tiles(t, (128, 64, 32, 16))
        tt = _tiles(t, (512, 256, 128, 64, 32, 16))
        sb = 1
    else:
        ya, p_all = _attn_sample(qkv_t, ck, cv, sinks, *hgrn_in, nseq=nseq, t=t, layer=layer)
        last = qkv_t[nq:, :].reshape(2 * _KV_W, nseq, t).transpose(1, 0, 2)
        cs = tt = t
        sb = _tiles(nseq, (4, 2, 1))
    last = last.reshape(nseq, 2, N_KV_HEADS, HEAD_DIM, -1).transpose(1, 0, 4, 2, 3)
    yc, nbuf, hlast = _lru(x, pp['lru'], cbuf, h0, nseq=nseq, t=t, tt=tt,
                           sb=1 if prompt else _tiles(nseq, (8, 4, 2, 1)), layer=layer,
                           state_layer=state_layer)
    yb, s_new = _hgrn(p_all, pp['hgrn_lb_logits'], pp['hgrn_o_norm'], s0, nseq=nseq, t=t, cs=cs,
                      sb=sb, layer=layer, s0_layer=state_layer)
    x = _merge(x, ya, yb, yc, pp['merge'], tm, layer)
    x = _ffn(x, pp['norm_ffn2'], *pp['ffn2'], tm, layer)
    return x, (last[0], last[1], s_new, nbuf, hlast.reshape(nseq, d))


def _forward(x_prompt, x_sample, cache_attn_k, cache_attn_v, state_hgrn, state_conv, state_lru, w):
    bsz, seq, d = x_prompt.shape
    dbsz, dseq, _ = x_sample.shape
    depth = w['w_in'].shape[0]
    win_rows = cache_attn_k.shape[2]
    assert win_rows == WINDOW, "the attention kernels assume a full cached window"
    pp = _prepare_params(w)
    xp = x_prompt.reshape(bsz * seq, d)
    xs = x_sample.reshape(dbsz * dseq, d)
    zero_cache = (None, None,
                  jnp.zeros((1, bsz, HG_HEADS, HG_DK, HG_DV), F32),
                  jnp.zeros((1, bsz, CONV_WIDTH - 1, d), F32),
                  jnp.zeros((1, bsz, 1, d), F32))
    cache = (cache_attn_k.reshape(depth, dbsz * win_rows, _KV_W),
             cache_attn_v.reshape(depth, dbsz * win_rows, _KV_W),
             state_hgrn, state_conv, state_lru[:, :, None, :])
    st_p, st_s = [], []
    for l in range(depth):
        xp, sp = _layer(xp, pp, zero_cache, nseq=bsz, t=seq, layer=l, prompt=True)
        xs, ss = _layer(xs, pp, cache, nseq=dbsz, t=dseq, layer=l, prompt=False)
        st_p.append(sp)
        st_s.append(ss)

    def stack(sts, i):
        return jnp.stack([s[i] for s in sts], axis=0)

    return (xp.reshape(bsz, seq, d), xs.reshape(dbsz, dseq, d),
            stack(st_p, 0), stack(st_p, 1), stack(st_p, 2), stack(st_p, 3), stack(st_p, 4),
            stack(st_s, 0), stack(st_s, 1), stack(st_s, 2), stack(st_s, 3), stack(st_s, 4))


def kernel(x_prompt, x_sample, cache_attn_k, cache_attn_v, state_hgrn, state_conv, state_lru,
           norm_ffn1, w_ffn1_up, w_ffn1_down, norm_mix, w_in, q_norm, k_norm, attn_sinks, w_attn_o,
           hgrn_lb_logits, hgrn_o_norm, w_hgrn_o, conv_w, conv_b, lru_w_a, lru_b_a, lru_w_x, lru_b_x,
           lru_lambda, w_lru_o, w_out, norm_ffn2, w_ffn2_up, w_ffn2_down):
    w = dict(norm_ffn1=norm_ffn1, w_ffn1_up=w_ffn1_up, w_ffn1_down=w_ffn1_down, norm_mix=norm_mix,
             w_in=w_in, q_norm=q_norm, k_norm=k_norm, attn_sinks=attn_sinks, w_attn_o=w_attn_o,
             hgrn_lb_logits=hgrn_lb_logits, hgrn_o_norm=hgrn_o_norm, w_hgrn_o=w_hgrn_o,
             conv_w=conv_w, conv_b=conv_b, lru_w_a=lru_w_a, lru_b_a=lru_b_a, lru_w_x=lru_w_x,
             lru_b_x=lru_b_x, lru_lambda=lru_lambda, w_lru_o=w_lru_o, w_out=w_out,
             norm_ffn2=norm_ffn2, w_ffn2_up=w_ffn2_up, w_ffn2_down=w_ffn2_down)
    return _forward(x_prompt, x_sample, cache_attn_k, cache_attn_v, state_hgrn, state_conv,
                    state_lru, w)
```

```python
import functools

import numpy as np
import jax
import jax.numpy as jnp
from jax import lax
from jax.experimental import pallas as pl
from jax.experimental.pallas import tpu as pltpu

F32 = jnp.float32
BF16 = jnp.bfloat16

CHUNK = 64
N_HEADS = 16
N_KV_HEADS = 4
HEAD_DIM = 64
GROUP = N_HEADS // N_KV_HEADS
WINDOW = 128
SCALE = HEAD_DIM ** -0.5
HG_HEADS = 8
HG_DK = 128
HG_DV = 128
LOG_F_FLOOR = -60.0
CONV_WIDTH = 4
LRU_C = 8.0
LRU_BW = 64
EPS = 1e-6
NEG = -1e30

MXU_DIM = 256
VMEM_LIMIT_BYTES = 56 * 1024 * 1024

_SLAB = 1024
(_C_HQ, _C_HF, _C_HI, _C_HG) = range(4)
_N_HGRN_SLABS, _N_LRU_SLABS, _N_GATE_SLABS = 4, 2, 3
_KV_W = N_KV_HEADS * HEAD_DIM
_QKV_ROWS = N_HEADS * HEAD_DIM + 2 * _KV_W


def _cparams(sem):
    return pltpu.CompilerParams(dimension_semantics=sem, vmem_limit_bytes=VMEM_LIMIT_BYTES)


def _rms(x, g):
    return x * lax.rsqrt(jnp.mean(x * x, axis=-1, keepdims=True) + EPS) * g


def _sigmoid(x):
    return 1.0 / (1.0 + jnp.exp(-x))


def _silu(x):
    return x * _sigmoid(x)


def _dot(a, b):
    return jnp.dot(a, b, preferred_element_type=F32)


def _dot_nt(a, b):
    return lax.dot_general(a, b, (((1,), (1,)), ((), ())), preferred_element_type=F32)


def _dot_tn(a, b):
    return lax.dot_general(a, b, (((0,), (0,)), ((), ())), preferred_element_type=F32)


def _ff_chunks(dff, width):
    assert dff % MXU_DIM == 0
    edges = list(range(0, dff, width)) + [dff]
    return list(zip(edges[:-1], edges[1:]))


def _ffn_kernel(x_ref, g_ref, wu_ref, wd_ref, o_ref, *, chunks):
    x = x_ref[...]
    dff = wd_ref.shape[0]
    h = _rms(x, g_ref[...]).astype(BF16)
    acc = None
    for lo, hi in chunks:
        gate = _dot(h, wu_ref[:, lo:hi])
        val = _dot(h, wu_ref[:, dff + lo:dff + hi])
        part = _dot((_silu(gate) * val).astype(BF16), wd_ref[lo:hi, :])
        acc = part if acc is None else acc + part
    o_ref[...] = x + 0.5 * acc


def _resident(shape):
    return pl.BlockSpec(shape, lambda *_: (0,) * len(shape), pipeline_mode=pl.Buffered(1))


def _layer_resident(stacked, layer):
    shape = stacked.shape[1:]
    return pl.BlockSpec((None,) + shape, lambda *_: (layer,) + (0,) * len(shape),
                        pipeline_mode=pl.Buffered(1))


def _ffn(x, g, w_up, w_down, tm, layer):
    n, d = x.shape
    dff = w_down.shape[1]
    return pl.pallas_call(
        functools.partial(_ffn_kernel, chunks=_ff_chunks(dff, 6 * MXU_DIM)),
        grid=(n // tm,),
        in_specs=[
            pl.BlockSpec((tm, d), lambda i: (i, 0)),
            _layer_resident(g, layer),
            _layer_resident(w_up, layer),
            _layer_resident(w_down, layer),
        ],
        out_specs=pl.BlockSpec((tm, d), lambda i: (i, 0)),
        out_shape=jax.ShapeDtypeStruct((n, d), F32),
        compiler_params=_cparams(("parallel",)),
        name="ffn",
    )(x, g, w_up, w_down)


_LANES = 128


def _softmax_pv(s, sk, vt):
    m = jnp.maximum(jnp.max(s, axis=0, keepdims=True), sk)
    e = jnp.exp(s - m)
    den = jnp.sum(e, axis=0, keepdims=True) + jnp.exp(sk - m)
    return _dot(vt, (e * (1.0 / den)).astype(BF16))


def _group_queries(q_ref, j, lanes):
    qcat = jnp.concatenate(
        [q_ref[(j * GROUP + g) * HEAD_DIM:(j * GROUP + g + 1) * HEAD_DIM, lanes]
         for g in range(GROUP)], axis=1).astype(BF16)
    zero = jnp.zeros_like(qcat)
    return jnp.concatenate([qcat if i == j else zero for i in range(N_KV_HEADS)], axis=0)


def _sink_row(sinks_ref, j, width):
    return jnp.concatenate(
        [jnp.full((1, width), sinks_ref[j * GROUP + g], F32) for g in range(GROUP)], axis=1)


def _attn_sample_kernel(qkv_ref, kp_ref, vp_ref, sinks_ref, x_ref, g_ref, whg_ref,
                        ya_ref, p_ref, yat_ref, *, t):
    nq = N_HEADS * HEAD_DIM
    hn = _rms(x_ref[...], g_ref[...]).astype(BF16)
    pcols = p_ref.shape[1] // N_KV_HEADS
    ncache = kp_ref.shape[0]
    keys = ncache + _LANES
    kfull = jnp.concatenate([kp_ref[...], qkv_ref[nq:nq + _KV_W, :].T], axis=0).astype(BF16)
    vt = jnp.concatenate([vp_ref[...].T, qkv_ref[nq + _KV_W:, :]], axis=1).astype(BF16)
    row = lax.broadcasted_iota(jnp.int32, (keys, 1), 0)
    key_seq = jnp.where(row < ncache, row // WINDOW, (row - ncache) // t)
    lane = lax.broadcasted_iota(jnp.int32, (1, GROUP * _LANES), 1)
    valid = key_seq == (lane % _LANES) // t
    for j in range(N_KV_HEADS):
        s = jnp.where(valid, _dot(kfull, _group_queries(qkv_ref, j, slice(None))), NEG)
        o = _softmax_pv(s, _sink_row(sinks_ref, j, _LANES), vt[j * HEAD_DIM:(j + 1) * HEAD_DIM])
        for g in range(GROUP):
            yat_ref[(j * GROUP + g) * HEAD_DIM:(j * GROUP + g + 1) * HEAD_DIM, :] = (
                o[:, g * _LANES:(g + 1) * _LANES])
        pc = slice(j * pcols, (j + 1) * pcols)
        p_ref[:, pc] = _dot(hn, whg_ref[:, pc])
    ya_ref[...] = yat_ref[...].T.astype(ya_ref.dtype)


def _attn_sample(qkv_t, cache_k, cache_v, sinks, x, g, whg, *, nseq, t, layer):
    n = nseq * t
    nq = N_HEADS * HEAD_DIM
    sb = _LANES // t
    d = x.shape[1]
    pcols = whg.shape[2]
    prev = pl.BlockSpec((None, sb * WINDOW, _KV_W), lambda b: (layer, b, 0))
    return pl.pallas_call(
        functools.partial(_attn_sample_kernel, t=t),
        grid=(nseq // sb,),
        in_specs=[
            pl.BlockSpec((_QKV_ROWS, _LANES), lambda b: (0, b)),
            prev, prev,
            pl.BlockSpec(memory_space=pltpu.SMEM),
            pl.BlockSpec((_LANES, d), lambda b: (b, 0)),
            _layer_resident(g, layer),
            _layer_resident(whg, layer),
        ],
        out_specs=[pl.BlockSpec((_LANES, nq), lambda b: (b, 0)),
                   pl.BlockSpec((_LANES, pcols), lambda b: (b, 0))],
        out_shape=[jax.ShapeDtypeStruct((n, nq), BF16),
                   jax.ShapeDtypeStruct((n, pcols), F32)],
        scratch_shapes=[pltpu.VMEM((nq, _LANES), F32)],
        compiler_params=_cparams(("parallel",)),
        name="attn_sample",
    )(qkv_t, cache_k, cache_v, sinks, x, g, whg)


def _qkv_t_kernel(x_ref, g_ref, w_ref, gc_ref, o_ref):
    h = _rms(x_ref[...], g_ref[...]).astype(BF16)
    acc = _dot_nt(w_ref[...], h)
    nqk = gc_ref.shape[0]
    for r in range(0, nqk, HEAD_DIM):
        blk = acc[r:r + HEAD_DIM]
        ms = jnp.sum(blk * blk, axis=0, keepdims=True) * (1.0 / HEAD_DIM)
        o_ref[r:r + HEAD_DIM, :] = blk * lax.rsqrt(ms + EPS) * gc_ref[r:r + HEAD_DIM, :]
    o_ref[nqk:, :] = acc[nqk:]


def _qkv_t(x, g, w_t, gain_col, tm, layer):
    n, d = x.shape
    rows = w_t.shape[1]
    return pl.pallas_call(
        _qkv_t_kernel,
        grid=(n // tm,),
        in_specs=[
            pl.BlockSpec((tm, d), lambda i: (i, 0)),
            _layer_resident(g, layer),
            _layer_resident(w_t, layer),
            _layer_resident(gain_col, layer),
        ],
        out_specs=pl.BlockSpec((rows, tm), lambda i: (0, i)),
        out_shape=jax.ShapeDtypeStruct((rows, n), F32),
        compiler_params=_cparams(("parallel",)),
        name="qkv_t",
    )(x, g, w_t, gain_col)


_PAIR = 2 * CHUNK
_KEYS = _PAIR + WINDOW


def _attn_prompt_kernel(q_ref, kc_ref, vc_ref, kp_ref, vp_ref, sinks_ref, x_ref, g_ref, whg_ref,
                        ya_ref, p_ref, yat_ref):
    tq = q_ref.shape[1]
    t = pl.program_id(1)
    hn = _rms(x_ref[...], g_ref[...]).astype(BF16)
    pcols = p_ref.shape[1] // ((tq // _PAIR) * N_KV_HEADS)
    lane = lax.broadcasted_iota(jnp.int32, (1, GROUP * _PAIR), 1)
    first_chunk = (lane % _PAIR) < CHUNK
    k_cur = kc_ref[...].T.astype(BF16)
    k_prev = kp_ref[...].T.astype(BF16)
    for p in range(tq // _PAIR):
        lo = p * _PAIR
        if p == 0:
            kb = jnp.concatenate([k_prev, k_cur[:_PAIR]], axis=0)
        else:
            kb = k_cur[lo - WINDOW:lo + _PAIR]
        first_key_chunk = t * (tq // CHUNK) + (lo - WINDOW) // CHUNK
        for j in range(N_KV_HEADS):
            rows = slice(j * HEAD_DIM, (j + 1) * HEAD_DIM)
            if p == 0:
                vb = jnp.concatenate([vp_ref[rows, :], vc_ref[rows, :_PAIR]], axis=1)
            else:
                vb = vc_ref[rows, lo - WINDOW:lo + _PAIR]
            s = _dot(kb, _group_queries(q_ref, j, slice(lo, lo + _PAIR)))
            s = jnp.concatenate([
                jnp.where(first_chunk & (first_key_chunk >= 0), s[:CHUNK], NEG),
                jnp.where(first_key_chunk + 1 >= 0, s[CHUNK:2 * CHUNK], NEG),
                s[2 * CHUNK:3 * CHUNK],
                jnp.where(first_chunk, NEG, s[3 * CHUNK:]),
            ], axis=0)
            o = _softmax_pv(s, _sink_row(sinks_ref, j, _PAIR), vb.astype(BF16))
            for g in range(GROUP):
                yat_ref[(j * GROUP + g) * HEAD_DIM:(j * GROUP + g + 1) * HEAD_DIM, lo:lo + _PAIR] = (
                    o[:, g * _PAIR:(g + 1) * _PAIR])
            pc = slice((p * N_KV_HEADS + j) * pcols, (p * N_KV_HEADS + j + 1) * pcols)
            p_ref[:, pc] = _dot(hn, whg_ref[:, pc])
    ya_ref[...] = yat_ref[...].T.astype(ya_ref.dtype)


def _attn_prompt(qkv_t, sinks, x, g, whg, *, nseq, t, tq, layer):
    n = nseq * t
    nt = t // tq
    nq = N_HEADS * HEAD_DIM
    kblk, vblk = nq // _KV_W, nq // _KV_W + 1
    d = x.shape[1]
    pcols = whg.shape[2]

    def prev_map(blk):
        return lambda b, i: (blk, jnp.maximum(b * (t // WINDOW) + i * (tq // WINDOW) - 1, 0))

    return pl.pallas_call(
        _attn_prompt_kernel,
        grid=(nseq, nt),
        in_specs=[
            pl.BlockSpec((nq, tq), lambda b, i: (0, b * nt + i)),
            pl.BlockSpec((_KV_W, tq), lambda b, i: (kblk, b * nt + i)),
            pl.BlockSpec((_KV_W, tq), lambda b, i: (vblk, b * nt + i)),
            pl.BlockSpec((_KV_W, WINDOW), prev_map(kblk)),
            pl.BlockSpec((_KV_W, WINDOW), prev_map(vblk)),
            pl.BlockSpec(memory_space=pltpu.SMEM),
            pl.BlockSpec((tq, d), lambda b, i: (b * nt + i, 0)),
            _layer_resident(g, layer),
            _layer_resident(whg, layer),
        ],
        out_specs=[pl.BlockSpec((tq, nq), lambda b, i: (b * nt + i, 0)),
                   pl.BlockSpec((tq, pcols), lambda b, i: (b * nt + i, 0))],
        out_shape=[jax.ShapeDtypeStruct((n, nq), BF16),
                   jax.ShapeDtypeStruct((n, pcols), F32)],
        scratch_shapes=[pltpu.VMEM((nq, tq), F32)],
        compiler_params=_cparams(("parallel", "arbitrary")),
        name="attn_prompt",
    )(qkv_t, qkv_t, qkv_t, qkv_t, qkv_t, sinks, x, g, whg)


def _hgrn_levels(c):
    return [c >> (i + 1) for i in range(int(np.log2(c)))]


_SUBLANES = 8


def _level_operands(q, k, g, s):
    rows = q.shape[0]
    if s < _SUBLANES:
        upper = (lax.broadcasted_iota(jnp.int32, (rows, 1), 0) & s) != 0
        return jnp.where(upper, q * g, 0.0), jnp.where(upper, 0.0, k * g)
    zero = jnp.zeros((s, q.shape[1]), F32)
    qparts, kparts = [], []
    for g0 in range(0, rows, 2 * s):
        lo, up = slice(g0, g0 + s), slice(g0 + s, g0 + 2 * s)
        kparts += [k[lo] * g[lo], zero]
        qparts += [zero, q[up] * g[up]]
    return jnp.concatenate(qparts, axis=0), jnp.concatenate(kparts, axis=0)


def _pair_block_diag(x):
    c = x.shape[1] // 2
    zero = jnp.zeros((x.shape[0], c), x.dtype)
    return jnp.concatenate([jnp.concatenate([x[:, :c], zero], axis=1),
                            jnp.concatenate([zero, x[:, c:]], axis=1)], axis=0)


def _hgrn_kernel(hq_ref, hf_ref, hi_ref, hg_ref, lbl_ref, on_ref, s0_ref, yb_ref, sout_ref,
                 a_ref, m_ref, st_ref, *, layer, cs, nsub):
    c = pl.program_id(1)
    rows = hq_ref.shape[0] // nsub
    nsq = rows // cs
    levels = _hgrn_levels(cs)

    @pl.when(c == 0)
    def _():
        row = lax.broadcasted_iota(jnp.int32, (rows, rows), 0)
        col = lax.broadcasted_iota(jnp.int32, (rows, rows), 1)
        mats = [((col <= row) & ((row & ~(cs - 1)) == (col & ~(cs - 1)))).astype(F32)]
        for i, s in enumerate(levels):
            mid = (row & ~(2 * s - 1)) + s
            upper = (row & s) != 0
            if s < _SUBLANES:
                up = ((col >= mid) & (col <= row)).astype(F32)
                lo = ((col > row) & (col < mid)).astype(F32)
                mats.append(jnp.where(upper, up, lo))
            same =(row & ~(2 * s - 1)) == (col & ~(2 * s - 1))
            same = same.astype(F32)
            m_ref[i] = jnp.concatenate([same, same], axis=1)
        for i, a in enumerate(mats):
            a = a.astype(BF16)
            a_ref[i * rows:(i + 1) * rows, :] = jnp.concatenate([a, a, a], axis=1)
        for sq in range(nsq):
            for h in range(HG_HEADS):
                st_ref[sq * HG_HEADS + h] = s0_ref[sq, h].T

    lg = lbl_ref[...]
    e = jnp.exp(lg - jnp.max(lg, axis=0, keepdims=True))
    lb = jnp.zeros((1, e.shape[1]), F32)
    for i in range(1, layer + 1):
        lb = lb + e[i:i + 1]
    lb = lb / jnp.sum(e, axis=0, keepdims=True)

    for sub in range(nsub):
        _hgrn_block(slice(sub * rows, (sub + 1) * rows), hq_ref, hf_ref, hi_ref, hg_ref, yb_ref,
                    a_ref, m_ref, st_ref, lb, on_ref[...], cs=cs)

    @pl.when(c == pl.num_programs(1) - 1)
    def _():
        for sq in range(nsq):
            for h in range(HG_HEADS):
                sout_ref[sq, h] = st_ref[sq * HG_HEADS + h].T


def _hgrn_block(rs, hq_ref, hf_ref, hi_ref, hg_ref, yb_ref, a_ref, m_ref, st_ref, lb, on, *, cs):
    rows = rs.stop - rs.start
    nsq = rows // cs
    levels = _hgrn_levels(cs)
    sg = _sigmoid(hf_ref[rs])
    f = lb + (1.0 - lb) * sg
    logf = jnp.maximum(jnp.log(jnp.maximum(f, 1e-26)), LOG_F_FLOOR)
    kk = (1.0 - lb) * (1.0 - sg)
    l1 = logf.astype(BF16)
    r1 = logf - l1.astype(F32)
    l2 = r1.astype(BF16)
    l3 = (r1 - l2.astype(F32)).astype(BF16)
    pieces = jnp.concatenate([l1, l2, l3], axis=0)

    lin = _dot(a_ref[...], pieces)
    bc = lin[:rows]
    ex, nmat = [], 1
    for s in levels:
        if s < _SUBLANES:
            ex.append(lin[nmat * rows:(nmat + 1) * rows])
            nmat += 1
        else:
            ex.append(jnp.concatenate(
                [-jnp.abs(bc[g0:g0 + 2 * s] - bc[g0 + s - 1:g0 + s]) for g0 in range(0, rows, 2 * s)],
                axis=0))

    dk = HG_DK
    for hp in range(HG_HEADS // 2):
        sl = slice(2 * hp * dk, 2 * (hp + 1) * dk)
        q = _silu(hq_ref[rs, sl])
        v = hi_ref[rs, sl]
        vb = v.astype(BF16)
        k = kk[:, sl]
        b = bc[:, sl]
        att = None
        for i, s in enumerate(levels):
            qs, ks = _level_operands(q, k, jnp.exp(ex[i][:, sl]), s)
            part = _dot_nt(qs.astype(BF16), _pair_block_diag(ks.astype(BF16)))
            if 2 * s < rows:
                part = part * m_ref[i]
            att = part if att is None else att + part
        qk = q * k
        diag = jnp.concatenate(
            [jnp.broadcast_to(jnp.sum(qk[:, :dk], axis=-1, keepdims=True), (rows, dk)),
             jnp.broadcast_to(jnp.sum(qk[:, dk:], axis=-1, keepdims=True), (rows, dk))], axis=1)
        o = _dot(att.astype(BF16), _pair_block_diag(vb)) + diag * v
        qd = (q * jnp.exp(b)).astype(BF16)
        inter = []
        for sq in range(nsq):
            r = slice(sq * cs, (sq + 1) * cs)
            ia = sq * HG_HEADS + 2 * hp
            sta, stb = st_ref[ia], st_ref[ia + 1]
            zero = jnp.zeros((dk, dk), BF16)
            st2 = jnp.concatenate(
                [jnp.concatenate([sta.astype(BF16), zero], axis=1),
                 jnp.concatenate([zero, stb.astype(BF16)], axis=1)], axis=0)
            inter.append(_dot_nt(qd[r], st2))
            blast = b[(sq + 1) * cs - 1:(sq + 1) * cs, :]
            kd = (k[r] * jnp.exp(blast - b[r])).astype(BF16)
            upd = _dot_tn(vb[r], kd)
            decay = jnp.exp(blast)
            st_ref[ia] = sta * decay[:, :dk] + upd[:dk, :dk]
            st_ref[ia + 1] = stb * decay[:, dk:] + upd[dk:, dk:]
        o = o + (inter[0] if nsq == 1 else jnp.concatenate(inter, axis=0))
        on2 = jnp.concatenate([_rms(o[:, :dk], on), _rms(o[:, dk:], on)], axis=1)
        yb_ref[rs, sl] = (on2 * _silu(hg_ref[rs, sl])).astype(yb_ref.dtype)


def _hgrn(p_all, lb_logits, o_norm, s0, *, nseq, t, cs, sb, nsub, layer, s0_layer):
    assert (sb == 1 or cs == t) and (nsub == 1 or sb == 1)
    n = nseq * t
    nc = t // (cs * nsub)
    rows = sb * cs
    nlev = len(_hgrn_levels(cs))
    nmat = sum(1 for s in _hgrn_levels(cs) if s < _SUBLANES)

    def slab(col):
        return pl.BlockSpec((nsub * rows, _SLAB), lambda b, i: (b * nc + i, col))

    st_block = (sb, HG_HEADS, HG_DK, HG_DV)
    return pl.pallas_call(
        functools.partial(_hgrn_kernel, layer=layer, cs=cs, nsub=nsub),
        grid=(nseq // sb, nc),
        in_specs=[
            slab(_C_HQ), slab(_C_HF), slab(_C_HI), slab(_C_HG),
            _resident(lb_logits.shape),
            _layer_resident(o_norm, layer),
            pl.BlockSpec((None,) + st_block, lambda b, i: (s0_layer, b, 0, 0, 0)),
        ],
        out_specs=[pl.BlockSpec((nsub * rows, _SLAB), lambda b, i: (b * nc + i, 0)),
                   pl.BlockSpec(st_block, lambda b, i: (b, 0, 0, 0))],
        out_shape=[jax.ShapeDtypeStruct((n, _SLAB), BF16),
                   jax.ShapeDtypeStruct((nseq, HG_HEADS, HG_DK, HG_DV), F32)],
        scratch_shapes=[
            pltpu.VMEM(((nmat + 1) * rows, 3 * rows), BF16),
            pltpu.VMEM((nlev, rows, 2 * rows), F32),
            pltpu.VMEM((sb * HG_HEADS, HG_DV, HG_DK), F32),
        ],
        compiler_params=_cparams(("parallel", "arbitrary")),
        name="hgrn_t%d" % cs,
    )(p_all, p_all, p_all, p_all, lb_logits, o_norm, s0)


_PAD = 8


def _lru_kernel(x_ref, g_ref, win_ref, cw_ref, cb_ref, wa_ref, ba_ref, wx_ref, bx_ref,
                lam_ref, cbuf_ref, h0_ref, yc_ref, nbuf_ref, hlast_ref,
                xe_ref, hs_ref, hc_ref):
    t = pl.program_id(1)
    rows, w = x_ref.shape
    nsq = xe_ref.shape[0]
    tt = rows // nsq
    hist = CONV_WIDTH - 1
    hn = _rms(x_ref[...], g_ref[...]).astype(BF16)

    @pl.when(t == 0)
    def _():
        for s in range(nsq):
            xe_ref[s, _PAD - hist:_PAD, :] = cbuf_ref[s]
            hc_ref[s] = h0_ref[s]

    @pl.when(t > 0)
    def _():
        for s in range(nsq):
            xe_ref[s, _PAD - hist:_PAD, :] = xe_ref[s, _PAD + tt - hist:_PAD + tt, :]

    ngroups = w // MXU_DIM
    sub =lax.broadcasted_iota(jnp.int32, (_SUBLANES, 1), 0)

    def project(g):
        return (_dot(hn, win_ref[:, g * MXU_DIM:(g + 1) * MXU_DIM]),
                _dot(hn, win_ref[:, w + g * MXU_DIM:w + (g + 1) * MXU_DIM]))

    nxt = project(0)
    for g in range(ngroups):
        cols = slice(g * MXU_DIM, (g + 1) * MXU_DIM)
        lx, lgv = nxt
        if g + 1 < ngroups:
            nxt = project(g + 1)
        xcs = []
        for s in range(nsq):
            xe_ref[s, _PAD:_PAD + tt, cols] = lx[s * tt:(s + 1) * tt]
            xc = xe_ref[s, _PAD - hist:_PAD - hist + tt, cols] * cw_ref[0:1, cols]
            for j in range(1, CONV_WIDTH):
                xc = xc + (xe_ref[s, _PAD - hist + j:_PAD - hist + j + tt, cols]
                           * cw_ref[j:j + 1, cols])
            xcs.append(xc)
        xc = (xcs[0] if nsq == 1 else jnp.concatenate(xcs, axis=0)) + cb_ref[:, cols]
        gelu = 0.5 * lgv * (
            1.0 + jnp.tanh(np.sqrt(2.0 / np.pi) * (lgv + 0.044715 * (lgv * lgv * lgv))))
        xg = xc.astype(BF16)
        ra = _dot(xg, wa_ref[g])
        rx = _dot(xg, wx_ref[g])
        r =_sigmoid(ra + ba_ref[:, cols])
        ig = _sigmoid(rx + bx_ref[:, cols])
        nl = -lam_ref[:, cols]
        softplus = jnp.maximum(nl, 0.0) + jnp.log1p(jnp.exp(-jnp.abs(nl)))
        log_a = -LRU_C * r * softplus
        a = jnp.exp(log_a)
        u = -jnp.tanh(log_a) * (a * a + 1.0)
        bv = jnp.where(u > 0.0, u * lax.rsqrt(u), 0.0) * (ig * xc)

        for s in range(nsq):
            h = jnp.broadcast_to(hc_ref[s, :, cols], (_SUBLANES, MXU_DIM))
            for r0 in range(s * tt, (s + 1) * tt, _SUBLANES):
                am = a[r0:r0 + _SUBLANES]
                bm = bv[r0:r0 + _SUBLANES]
                d = 1
                while d < _SUBLANES:
                    seen = sub >= d
                    a_prev = jnp.where(seen, pltpu.roll(am, d, 0), 1.0)
                    b_prev = jnp.where(seen, pltpu.roll(bm, d, 0), 0.0)
                    bm = am * b_prev + bm
                    am = am * a_prev
                    d *= 2
                hs = am * h + bm
                hs_ref[r0:r0 + _SUBLANES, cols] = hs
                h = jnp.broadcast_to(hs[_SUBLANES - 1:, :], hs.shape)
            hc_ref[s, :, cols] = h[:1, :]
        yc_ref[:, cols] = (hs_ref[:, cols] * gelu).astype(yc_ref.dtype)

    @pl.when(t == pl.num_programs(1) - 1)
    def _():
        for s in range(nsq):
            nbuf_ref[s] = xe_ref[s, _PAD + tt - hist:_PAD + tt, :]
            hlast_ref[s] = hc_ref[s]


def _lru(x, params, cbuf, h0, *, nseq, t, tt, sb, layer, state_layer):
    assert sb == 1 or tt == t
    n, w = x.shape
    nt = t // tt
    rows = sb * tt
    hist = CONV_WIDTH - 1
    return pl.pallas_call(
        _lru_kernel,
        grid=(nseq // sb, nt),
        in_specs=[
            pl.BlockSpec((rows, w), lambda b, i: (b * nt + i, 0)),
            *[_layer_resident(p, layer) for p in params],
            pl.BlockSpec((None, sb, hist, w), lambda b, i: (state_layer, b, 0, 0)),
            pl.BlockSpec((None, sb, 1, w), lambda b, i: (state_layer, b, 0, 0)),
        ],
        out_specs=[
            pl.BlockSpec((rows, w), lambda b, i: (b * nt + i, 0)),
            pl.BlockSpec((sb, hist, w), lambda b, i: (b, 0, 0)),
            pl.BlockSpec((sb, 1, w), lambda b, i: (b, 0, 0)),
        ],
        out_shape=[jax.ShapeDtypeStruct((n, w), BF16),
                   jax.ShapeDtypeStruct((nseq, hist, w), F32),
                   jax.ShapeDtypeStruct((nseq, 1, w), F32)],
        scratch_shapes=[
            pltpu.VMEM((sb, _PAD + tt, w), F32),
            pltpu.VMEM((rows, w), F32),
            pltpu.VMEM((sb, 1, w), F32),
        ],
        compiler_params=_cparams(("parallel", "arbitrary")),
        name="lru_t%d" % tt,
    )(x, *params, cbuf, h0)


def _merge_kernel(x_ref, ya_ref, yb_ref, yc_ref, g_ref, wg_ref,
                  wa_ref, wb_ref, wc_ref, wo_ref, o_ref):
    x = x_ref[...]
    d = x.shape[1]
    hn = _rms(x, g_ref[...]).astype(BF16)
    m = None
    for i, (y_ref, w_ref) in enumerate(((ya_ref, wa_ref), (yb_ref, wb_ref), (yc_ref, wc_ref))):
        gate = _sigmoid(_dot(hn, wg_ref[:, i * d:(i + 1) * d]))
        part = gate * _dot(y_ref[...], w_ref[...])
        m = part if m is None else m + part
    o_ref[...] = x + _dot(m.astype(BF16), wo_ref[...])


def _merge(x, ya, yb, yc, params, tm, layer):
    n, d = x.shape
    row = pl.BlockSpec((tm, d), lambda i: (i, 0))
    return pl.pallas_call(
        _merge_kernel,
        grid=(n // tm,),
        in_specs=[row, row, row, row, *[_layer_resident(p, layer) for p in params]],
        out_specs=row,
        out_shape=jax.ShapeDtypeStruct((n, d), F32),
        compiler_params=_cparams(("parallel",)),
        name="merge",
    )(x, ya, yb, yc, *params)


def _block_diag(w):
    nl, nb, bw, _ = w.shape
    per = MXU_DIM // bw
    wg = w.reshape(nl, nb // per, per, bw, bw)
    eye = jnp.eye(per, dtype=w.dtype)
    out = jnp.einsum('lgpij,pq->lgpiqj', wg, eye)
    return out.reshape(nl, nb // per, MXU_DIM, MXU_DIM).astype(BF16)


def _prepare_params(w):
    def vec(name):
        return w[name][:, None, :]

    def bf16(name):
        return w[name].astype(BF16)

    w_in = w['w_in']
    c_hgrn = _QKV_ROWS
    c_lru = c_hgrn + _N_HGRN_SLABS * _SLAB
    c_gate = c_lru + _N_LRU_SLABS * _SLAB
    assert c_gate + _N_GATE_SLABS * _SLAB == w_in.shape[2]
    qk_gain =jnp.concatenate([jnp.tile(w['q_norm'] * SCALE, (1, N_HEADS)),
                               jnp.tile(w['k_norm'], (1, N_KV_HEADS))], axis=1)
    return {
        'norm_ffn1': vec('norm_ffn1'), 'norm_mix': vec('norm_mix'), 'norm_ffn2': vec('norm_ffn2'),
        'ffn1': (bf16('w_ffn1_up'), bf16('w_ffn1_down')),
        'ffn2': (bf16('w_ffn2_up'), bf16('w_ffn2_down')),
        'w_qkv_t': jnp.swapaxes(w_in[:, :, :_QKV_ROWS], 1, 2).astype(BF16),
        'qk_gain_col': qk_gain[:, :, None],
        'attn_sinks': w['attn_sinks'],
        'merge': (vec('norm_mix'), w_in[:, :, c_gate:].astype(BF16),
                  bf16('w_attn_o'), bf16('w_hgrn_o'), bf16('w_lru_o'), bf16('w_out')),
        'hgrn_lb_logits': w['hgrn_lb_logits'], 'hgrn_o_norm': vec('hgrn_o_norm'),
        'w_hgrn_in': w_in[:, :, c_hgrn:c_lru].astype(BF16),
        'lru': (vec('norm_mix'), w_in[:, :, c_lru:c_gate].astype(BF16),
                w['conv_w'], vec('conv_b'), _block_diag(w['lru_w_a']), vec('lru_b_a'),
                _block_diag(w['lru_w_x']), vec('lru_b_x'), vec('lru_lambda')),
    }


def _tiles(n, prefer):
    for tm in prefer:
        if n % tm == 0:
            return tm
    return n


def _layer(x, pp, cache, *, nseq, t, layer, prompt):
    n, d = x.shape
    tm = _tiles(n, (512, 256, 128))
    x = _ffn(x, pp['norm_ffn1'], *pp['ffn1'], tm, layer)
    ck, cv, s0, cbuf, h0 = cache
    state_layer = 0 if prompt else layer
    qkv_t = _qkv_t(x, pp['norm_mix'], pp['w_qkv_t'], pp['qk_gain_col'],
                   _tiles(n, (1024, 512, 256, 128)), layer)
    sinks = pp['attn_sinks'][layer]
    nq = N_HEADS * HEAD_DIM
    hgrn_in = (x, pp['norm_mix'], pp['w_hgrn_in'])
    if prompt:
        ya, p_all = _attn_prompt(qkv_t, sinks, *hgrn_in, nseq=nseq, t=t,
                                 tq=_tiles(t, (512, 256, 128)), layer=layer)
        last = jnp.stack([qkv_t[nq:, (b + 1) * t - WINDOW:(b + 1) * t] for b in range(nseq)])
        cs = _tiles(t, (128, 64, 32, 16))
        tt = _tiles(t, (512, 256, 128, 64, 32, 16))
        sb = 1
    else:
        ya, p_all = _attn_sample(qkv_t, ck, cv, sinks, *hgrn_in, nseq=nseq, t=t, layer=layer)
        last = qkv_t[nq:, :].reshape(2 * _KV_W, nseq, t).transpose(1, 0, 2)
        cs = tt = t
        sb = _tiles(nseq, (4, 2, 1))
    last = last.reshape(nseq, 2, N_KV_HEADS, HEAD_DIM, -1).transpose(1, 0, 4, 2, 3)
    yc, nbuf, hlast = _lru(x, pp['lru'], cbuf, h0, nseq=nseq, t=t, tt=tt,
                           sb=1 if prompt else _tiles(nseq, (8, 4, 2, 1)), layer=layer,
                           state_layer=state_layer)
    nsub = _tiles(t // cs, (4, 2, 1)) if prompt else 1
    yb, s_new = _hgrn(p_all, pp['hgrn_lb_logits'], pp['hgrn_o_norm'], s0, nseq=nseq, t=t, cs=cs,
                      sb=sb, nsub=nsub, layer=layer, s0_layer=state_layer)
    x = _merge(x, ya, yb, yc, pp['merge'], tm, layer)
    x = _ffn(x, pp['norm_ffn2'], *pp['ffn2'], tm, layer)
    return x, (last[0], last[1], s_new, nbuf, hlast.reshape(nseq, d))


def _forward(x_prompt, x_sample, cache_attn_k, cache_attn_v, state_hgrn, state_conv, state_lru, w):
    bsz, seq, d = x_prompt.shape
    dbsz, dseq, _ = x_sample.shape
    depth = w['w_in'].shape[0]
    win_rows = cache_attn_k.shape[2]
    assert win_rows == WINDOW, "the attention kernels assume a full cached window"
    pp = _prepare_params(w)
    xp = x_prompt.reshape(bsz * seq, d)
    xs = x_sample.reshape(dbsz * dseq, d)
    zero_cache = (None, None,
                  jnp.zeros((1, bsz, HG_HEADS, HG_DK, HG_DV), F32),
                  jnp.zeros((1, bsz, CONV_WIDTH - 1, d), F32),
                  jnp.zeros((1, bsz, 1, d), F32))
    cache = (cache_attn_k.reshape(depth, dbsz * win_rows, _KV_W),
             cache_attn_v.reshape(depth, dbsz * win_rows, _KV_W),
             state_hgrn, state_conv, state_lru[:, :, None, :])
    st_p, st_s = [], []
    for l in range(depth):
        xp, sp = _layer(xp, pp, zero_cache, nseq=bsz, t=seq, layer=l, prompt=True)
        xs, ss = _layer(xs, pp, cache, nseq=dbsz, t=dseq, layer=l, prompt=False)
        st_p.append(sp)
        st_s.append(ss)

    def stack(sts, i):
        return jnp.stack([s[i] for s in sts], axis=0)

    return (xp.reshape(bsz, seq, d), xs.reshape(dbsz, dseq, d),
            stack(st_p, 0), stack(st_p, 1), stack(st_p, 2), stack(st_p, 3), stack(st_p, 4),
            stack(st_s, 0), stack(st_s, 1), stack(st_s, 2), stack(st_s, 3), stack(st_s, 4))


def kernel(x_prompt, x_sample, cache_attn_k, cache_attn_v, state_hgrn, state_conv, state_lru,
           norm_ffn1, w_ffn1_up, w_ffn1_down, norm_mix, w_in, q_norm, k_norm, attn_sinks, w_attn_o,
           hgrn_lb_logits, hgrn_o_norm, w_hgrn_o, conv_w, conv_b, lru_w_a, lru_b_a, lru_w_x, lru_b_x,
           lru_lambda, w_lru_o, w_out, norm_ffn2, w_ffn2_up, w_ffn2_down):
    w = dict(norm_ffn1=norm_ffn1, w_ffn1_up=w_ffn1_up, w_ffn1_down=w_ffn1_down, norm_mix=norm_mix,
             w_in=w_in, q_norm=q_norm, k_norm=k_norm, attn_sinks=attn_sinks, w_attn_o=w_attn_o,
             hgrn_lb_logits=hgrn_lb_logits, hgrn_o_norm=hgrn_o_norm, w_hgrn_o=w_hgrn_o,
             conv_w=conv_w, conv_b=conv_b, lru_w_a=lru_w_a, lru_b_a=lru_b_a, lru_w_x=lru_w_x,
             lru_b_x=lru_b_x, lru_lambda=lru_lambda, w_lru_o=w_lru_o, w_out=w_out,
             norm_ffn2=norm_ffn2, w_ffn2_up=w_ffn2_up, w_ffn2_down=w_ffn2_down)
    return _forward(x_prompt, x_sample, cache_attn_k, cache_attn_v, state_hgrn, state_conv,
                    state_lru, w)
```

```python
import functools

import numpy as np
import jax
import jax.numpy as jnp
from jax import lax
from jax.experimental import pallas as pl
from jax.experimental.pallas import tpu as pltpu

F32 = jnp.float32
BF16 = jnp.bfloat16

CHUNK = 64
N_HEADS = 16
N_KV_HEADS = 4
HEAD_DIM = 64
GROUP = N_HEADS // N_KV_HEADS
WINDOW = 128
SCALE = HEAD_DIM ** -0.5
HG_HEADS = 8
HG_DK = 128
HG_DV = 128
LOG_F_FLOOR = -60.0
CONV_WIDTH = 4
LRU_C = 8.0
LRU_BW = 64
EPS = 1e-6
NEG = -1e30

MXU_DIM = 256
VMEM_LIMIT_BYTES = 56 * 1024 * 1024

_SLAB = 1024
(_C_HQ, _C_HF, _C_HI, _C_HG) = range(4)
_N_HGRN_SLABS, _N_LRU_SLABS, _N_GATE_SLABS = 4, 2, 3
_KV_W = N_KV_HEADS * HEAD_DIM
_QKV_ROWS = N_HEADS * HEAD_DIM + 2 * _KV_W


def _cparams(sem):
    return pltpu.CompilerParams(dimension_semantics=sem, vmem_limit_bytes=VMEM_LIMIT_BYTES)


def _rms(x, g):
    return x * lax.rsqrt(jnp.mean(x * x, axis=-1, keepdims=True) + EPS) * g


def _sigmoid(x):
    return 1.0 / (1.0 + jnp.exp(-x))


def _silu(x):
    return x * _sigmoid(x)


def _dot(a, b):
    return jnp.dot(a, b, preferred_element_type=F32)


def _dot_nt(a, b):
    return lax.dot_general(a, b, (((1,), (1,)), ((), ())), preferred_element_type=F32)


def _dot_tn(a, b):
    return lax.dot_general(a, b, (((0,), (0,)), ((), ())), preferred_element_type=F32)


def _ff_chunks(dff, width):
    assert dff % MXU_DIM == 0
    edges = list(range(0, dff, width)) + [dff]
    return list(zip(edges[:-1], edges[1:]))


def _ffn_kernel(x_ref, g_ref, wu_ref, wd_ref, o_ref, *, chunks):
    x = x_ref[...]
    dff = wd_ref.shape[0]
    h = _rms(x, g_ref[...]).astype(BF16)
    acc = None
    for lo, hi in chunks:
        gate = _dot(h, wu_ref[:, lo:hi])
        val = _dot(h, wu_ref[:, dff + lo:dff + hi])
        part = _dot((_silu(gate) * val).astype(BF16), wd_ref[lo:hi, :])
        acc = part if acc is None else acc + part
    o_ref[...] = x + 0.5 * acc


def _resident(shape):
    return pl.BlockSpec(shape, lambda *_: (0,) * len(shape), pipeline_mode=pl.Buffered(1))


def _layer_resident(stacked, layer):
    shape = stacked.shape[1:]
    return pl.BlockSpec((None,) + shape, lambda *_: (layer,) + (0,) * len(shape),
                        pipeline_mode=pl.Buffered(1))


_FFN_CHUNK_ELEMS = 512 * 6 * MXU_DIM


def _ffn(x, g, w_up, w_down, layer):
    n, d = x.shape
    dff = w_down.shape[1]
    tm = _tiles(n, (1024, 512, 256, 128))
    return pl.pallas_call(
        functools.partial(_ffn_kernel, chunks=_ff_chunks(dff, _FFN_CHUNK_ELEMS // tm)),
        grid=(n // tm,),
        in_specs=[
            pl.BlockSpec((tm, d), lambda i: (i, 0)),
            _layer_resident(g, layer),
            _layer_resident(w_up, layer),
            _layer_resident(w_down, layer),
        ],
        out_specs=pl.BlockSpec((tm, d), lambda i: (i, 0)),
        out_shape=jax.ShapeDtypeStruct((n, d), F32),
        compiler_params=_cparams(("parallel",)),
        name="ffn",
    )(x, g, w_up, w_down)


_LANES = 128


def _softmax_pv(s, sk, vt):
    m = jnp.maximum(jnp.max(s, axis=0, keepdims=True), sk)
    e = jnp.exp(s - m)
    den = jnp.sum(e, axis=0, keepdims=True) + jnp.exp(sk - m)
    return _dot(vt, (e * (1.0 / den)).astype(BF16))


def _group_queries(q_ref, j, lanes):
    qcat = jnp.concatenate(
        [q_ref[(j * GROUP + g) * HEAD_DIM:(j * GROUP + g + 1) * HEAD_DIM, lanes]
         for g in range(GROUP)], axis=1).astype(BF16)
    zero = jnp.zeros_like(qcat)
    return jnp.concatenate([qcat if i == j else zero for i in range(N_KV_HEADS)], axis=0)


def _sink_row(sinks_ref, j, width):
    return jnp.concatenate(
        [jnp.full((1, width), sinks_ref[j * GROUP + g], F32) for g in range(GROUP)], axis=1)


def _attn_sample_kernel(qkv_ref, kp_ref, vp_ref, sinks_ref, x_ref, g_ref, whg_ref,
                        ya_ref, p_ref, yat_ref, *, t):
    nq = N_HEADS * HEAD_DIM
    hn = _rms(x_ref[...], g_ref[...]).astype(BF16)
    pcols = p_ref.shape[1] // N_KV_HEADS
    ncache = kp_ref.shape[0]
    keys = ncache + _LANES
    kfull = jnp.concatenate([kp_ref[...], qkv_ref[nq:nq + _KV_W, :].T], axis=0).astype(BF16)
    vt = jnp.concatenate([vp_ref[...].T, qkv_ref[nq + _KV_W:, :]], axis=1).astype(BF16)
    row = lax.broadcasted_iota(jnp.int32, (keys, 1), 0)
    key_seq = jnp.where(row < ncache, row // WINDOW, (row - ncache) // t)
    lane = lax.broadcasted_iota(jnp.int32, (1, GROUP * _LANES), 1)
    valid = key_seq == (lane % _LANES) // t
    for j in range(N_KV_HEADS):
        s = jnp.where(valid, _dot(kfull, _group_queries(qkv_ref, j, slice(None))), NEG)
        o = _softmax_pv(s, _sink_row(sinks_ref, j, _LANES), vt[j * HEAD_DIM:(j + 1) * HEAD_DIM])
        for g in range(GROUP):
            yat_ref[(j * GROUP + g) * HEAD_DIM:(j * GROUP + g + 1) * HEAD_DIM, :] = (
                o[:, g * _LANES:(g + 1) * _LANES])
        pc = slice(j * pcols, (j + 1) * pcols)
        p_ref[:, pc] = _dot(hn, whg_ref[:, pc])
    ya_ref[...] = yat_ref[...].T.astype(ya_ref.dtype)


def _attn_sample(qkv_t, cache_k, cache_v, sinks, x, g, whg, *, nseq, t, layer):
    n = nseq * t
    nq = N_HEADS * HEAD_DIM
    sb = _LANES // t
    d = x.shape[1]
    pcols = whg.shape[2]
    prev = pl.BlockSpec((None, sb * WINDOW, _KV_W), lambda b: (layer, b, 0))
    return pl.pallas_call(
        functools.partial(_attn_sample_kernel, t=t),
        grid=(nseq // sb,),
        in_specs=[
            pl.BlockSpec((_QKV_ROWS, _LANES), lambda b: (0, b)),
            prev, prev,
            pl.BlockSpec(memory_space=pltpu.SMEM),
            pl.BlockSpec((_LANES, d), lambda b: (b, 0)),
            _layer_resident(g, layer),
            _layer_resident(whg, layer),
        ],
        out_specs=[pl.BlockSpec((_LANES, nq), lambda b: (b, 0)),
                   pl.BlockSpec((_LANES, pcols), lambda b: (b, 0))],
        out_shape=[jax.ShapeDtypeStruct((n, nq), BF16),
                   jax.ShapeDtypeStruct((n, pcols), F32)],
        scratch_shapes=[pltpu.VMEM((nq, _LANES), F32)],
        compiler_params=_cparams(("parallel",)),
        name="attn_sample",
    )(qkv_t, cache_k, cache_v, sinks, x, g, whg)


def _qkv_t_kernel(x_ref, g_ref, w_ref, gc_ref, o_ref):
    h = _rms(x_ref[...], g_ref[...]).astype(BF16)
    acc = _dot_nt(w_ref[...], h)
    nqk = gc_ref.shape[0]
    for r in range(0, nqk, HEAD_DIM):
        blk = acc[r:r + HEAD_DIM]
        ms = jnp.sum(blk * blk, axis=0, keepdims=True) * (1.0 / HEAD_DIM)
        o_ref[r:r + HEAD_DIM, :] = blk * lax.rsqrt(ms + EPS) * gc_ref[r:r + HEAD_DIM, :]
    o_ref[nqk:, :] = acc[nqk:]


def _qkv_t(x, g, w_t, gain_col, tm, layer):
    n, d = x.shape
    rows = w_t.shape[1]
    return pl.pallas_call(
        _qkv_t_kernel,
        grid=(n // tm,),
        in_specs=[
            pl.BlockSpec((tm, d), lambda i: (i, 0)),
            _layer_resident(g, layer),
            _layer_resident(w_t, layer),
            _layer_resident(gain_col, layer),
        ],
        out_specs=pl.BlockSpec((rows, tm), lambda i: (0, i)),
        out_shape=jax.ShapeDtypeStruct((rows, n), F32),
        compiler_params=_cparams(("parallel",)),
        name="qkv_t",
    )(x, g, w_t, gain_col)


_PAIR = 2 * CHUNK
_KEYS = _PAIR + WINDOW


def _attn_prompt_kernel(q_ref, kc_ref, vc_ref, kp_ref, vp_ref, sinks_ref, x_ref, g_ref, whg_ref,
                        ya_ref, p_ref, yat_ref):
    tq = q_ref.shape[1]
    t = pl.program_id(1)
    hn = _rms(x_ref[...], g_ref[...]).astype(BF16)
    pcols = p_ref.shape[1] // ((tq // _PAIR) * N_KV_HEADS)
    lane = lax.broadcasted_iota(jnp.int32, (1, GROUP * _PAIR), 1)
    first_chunk = (lane % _PAIR) < CHUNK
    k_cur = kc_ref[...].T.astype(BF16)
    k_prev = kp_ref[...].T.astype(BF16)
    for p in range(tq // _PAIR):
        lo = p * _PAIR
        if p == 0:
            kb = jnp.concatenate([k_prev, k_cur[:_PAIR]], axis=0)
        else:
            kb = k_cur[lo - WINDOW:lo + _PAIR]
        first_key_chunk = t * (tq // CHUNK) + (lo - WINDOW) // CHUNK
        for j in range(N_KV_HEADS):
            rows = slice(j * HEAD_DIM, (j + 1) * HEAD_DIM)
            if p == 0:
                vb = jnp.concatenate([vp_ref[rows, :], vc_ref[rows, :_PAIR]], axis=1)
            else:
                vb = vc_ref[rows, lo - WINDOW:lo + _PAIR]
            s = _dot(kb, _group_queries(q_ref, j, slice(lo, lo + _PAIR)))
            s = jnp.concatenate([
                jnp.where(first_chunk & (first_key_chunk >= 0), s[:CHUNK], NEG),
                jnp.where(first_key_chunk + 1 >= 0, s[CHUNK:2 * CHUNK], NEG),
                s[2 * CHUNK:3 * CHUNK],
                jnp.where(first_chunk, NEG, s[3 * CHUNK:]),
            ], axis=0)
            o = _softmax_pv(s, _sink_row(sinks_ref, j, _PAIR), vb.astype(BF16))
            for g in range(GROUP):
                yat_ref[(j * GROUP + g) * HEAD_DIM:(j * GROUP + g + 1) * HEAD_DIM, lo:lo + _PAIR] = (
                    o[:, g * _PAIR:(g + 1) * _PAIR])
            pc = slice((p * N_KV_HEADS + j) * pcols, (p * N_KV_HEADS + j + 1) * pcols)
            p_ref[:, pc] = _dot(hn, whg_ref[:, pc])
    ya_ref[...] = yat_ref[...].T.astype(ya_ref.dtype)


def _attn_prompt(qkv_t, sinks, x, g, whg, *, nseq, t, tq, layer):
    n = nseq * t
    nt = t // tq
    nq = N_HEADS * HEAD_DIM
    kblk, vblk = nq // _KV_W, nq // _KV_W + 1
    d = x.shape[1]
    pcols = whg.shape[2]

    def prev_map(blk):
        return lambda b, i: (blk, jnp.maximum(b * (t // WINDOW) + i * (tq // WINDOW) - 1, 0))

    return pl.pallas_call(
        _attn_prompt_kernel,
        grid=(nseq, nt),
        in_specs=[
            pl.BlockSpec((nq, tq), lambda b, i: (0, b * nt + i)),
            pl.BlockSpec((_KV_W, tq), lambda b, i: (kblk, b * nt + i)),
            pl.BlockSpec((_KV_W, tq), lambda b, i: (vblk, b * nt + i)),
            pl.BlockSpec((_KV_W, WINDOW), prev_map(kblk)),
            pl.BlockSpec((_KV_W, WINDOW), prev_map(vblk)),
            pl.BlockSpec(memory_space=pltpu.SMEM),
            pl.BlockSpec((tq, d), lambda b, i: (b * nt + i, 0)),
            _layer_resident(g, layer),
            _layer_resident(whg, layer),
        ],
        out_specs=[pl.BlockSpec((tq, nq), lambda b, i: (b * nt + i, 0)),
                   pl.BlockSpec((tq, pcols), lambda b, i: (b * nt + i, 0))],
        out_shape=[jax.ShapeDtypeStruct((n, nq), BF16),
                   jax.ShapeDtypeStruct((n, pcols), F32)],
        scratch_shapes=[pltpu.VMEM((nq, tq), F32)],
        compiler_params=_cparams(("parallel", "arbitrary")),
        name="attn_prompt",
    )(qkv_t, qkv_t, qkv_t, qkv_t, qkv_t, sinks, x, g, whg)


def _hgrn_levels(c):
    return [c >> (i + 1) for i in range(int(np.log2(c)))]


_SUBLANES = 8


def _level_operands(q, k, g, s):
    rows = q.shape[0]
    if s < _SUBLANES:
        upper = (lax.broadcasted_iota(jnp.int32, (rows, 1), 0) & s) != 0
        return jnp.where(upper, q * g, 0.0), jnp.where(upper, 0.0, k * g)
    zero = jnp.zeros((s, q.shape[1]), F32)
    qparts, kparts = [], []
    for g0 in range(0, rows, 2 * s):
        lo, up = slice(g0, g0 + s), slice(g0 + s, g0 + 2 * s)
        kparts += [k[lo] * g[lo], zero]
        qparts += [zero, q[up] * g[up]]
    return jnp.concatenate(qparts, axis=0), jnp.concatenate(kparts, axis=0)


def _pair_block_diag(x):
    c = x.shape[1] // 2
    zero = jnp.zeros((x.shape[0], c), x.dtype)
    return jnp.concatenate([jnp.concatenate([x[:, :c], zero], axis=1),
                            jnp.concatenate([zero, x[:, c:]], axis=1)], axis=0)


def _hgrn_kernel(hq_ref, hf_ref, hi_ref, hg_ref, lbl_ref, on_ref, s0_ref, yb_ref, sout_ref,
                 a_ref, m_ref, st_ref, *, layer, cs, nsub):
    c = pl.program_id(1)
    rows = hq_ref.shape[0] // nsub
    nsq = rows // cs
    levels = _hgrn_levels(cs)

    @pl.when(c == 0)
    def _():
        row = lax.broadcasted_iota(jnp.int32, (rows, rows), 0)
        col = lax.broadcasted_iota(jnp.int32, (rows, rows), 1)
        mats = [((col <= row) & ((row & ~(cs - 1)) == (col & ~(cs - 1)))).astype(F32)]
        for i, s in enumerate(levels):
            mid = (row & ~(2 * s - 1)) + s
            upper = (row & s) != 0
            if s < _SUBLANES:
                up = ((col >= mid) & (col <= row)).astype(F32)
                lo = ((col > row) & (col < mid)).astype(F32)
                mats.append(jnp.where(upper, up, lo))
            same =(row & ~(2 * s - 1)) == (col & ~(2 * s - 1))
            same = same.astype(F32)
            m_ref[i] = jnp.concatenate([same, same], axis=1)
        for i, a in enumerate(mats):
            a = a.astype(BF16)
            a_ref[i * rows:(i + 1) * rows, :] = jnp.concatenate([a, a, a], axis=1)
        for sq in range(nsq):
            for h in range(HG_HEADS):
                st_ref[sq * HG_HEADS + h] = s0_ref[sq, h].T

    lg = lbl_ref[...]
    e = jnp.exp(lg - jnp.max(lg, axis=0, keepdims=True))
    lb = jnp.zeros((1, e.shape[1]), F32)
    for i in range(1, layer + 1):
        lb = lb + e[i:i + 1]
    lb = lb / jnp.sum(e, axis=0, keepdims=True)

    for sub in range(nsub):
        _hgrn_block(slice(sub * rows, (sub + 1) * rows), hq_ref, hf_ref, hi_ref, hg_ref, yb_ref,
                    a_ref, m_ref, st_ref, lb, on_ref[...], cs=cs)

    @pl.when(c == pl.num_programs(1) - 1)
    def _():
        for sq in range(nsq):
            for h in range(HG_HEADS):
                sout_ref[sq, h] = st_ref[sq * HG_HEADS + h].T


def _hgrn_block(rs, hq_ref, hf_ref, hi_ref, hg_ref, yb_ref, a_ref, m_ref, st_ref, lb, on, *, cs):
    rows = rs.stop - rs.start
    nsq = rows // cs
    levels = _hgrn_levels(cs)
    sg = _sigmoid(hf_ref[rs])
    f = lb + (1.0 - lb) * sg
    logf = jnp.maximum(jnp.log(jnp.maximum(f, 1e-26)), LOG_F_FLOOR)
    kk = (1.0 - lb) * (1.0 - sg)
    l1 = logf.astype(BF16)
    r1 = logf - l1.astype(F32)
    l2 = r1.astype(BF16)
    l3 = (r1 - l2.astype(F32)).astype(BF16)
    pieces = jnp.concatenate([l1, l2, l3], axis=0)

    lin = _dot(a_ref[...], pieces)
    bc = lin[:rows]
    ex, nmat = [], 1
    for s in levels:
        if s < _SUBLANES:
            ex.append(lin[nmat * rows:(nmat + 1) * rows])
            nmat += 1
        else:
            ex.append(jnp.concatenate(
                [-jnp.abs(bc[g0:g0 + 2 * s] - bc[g0 + s - 1:g0 + s]) for g0 in range(0, rows, 2 * s)],
                axis=0))

    dk = HG_DK
    for hp in range(HG_HEADS // 2):
        sl = slice(2 * hp * dk, 2 * (hp + 1) * dk)
        q = _silu(hq_ref[rs, sl])
        v = hi_ref[rs, sl]
        vb = v.astype(BF16)
        k = kk[:, sl]
        b = bc[:, sl]
        att = None
        for i, s in enumerate(levels):
            qs, ks = _level_operands(q, k, jnp.exp(ex[i][:, sl]), s)
            part = _dot_nt(qs.astype(BF16), _pair_block_diag(ks.astype(BF16)))
            if 2 * s < rows:
                part = part * m_ref[i]
            att = part if att is None else att + part
        qk = q * k
        diag = jnp.concatenate(
            [jnp.broadcast_to(jnp.sum(qk[:, :dk], axis=-1, keepdims=True), (rows, dk)),
             jnp.broadcast_to(jnp.sum(qk[:, dk:], axis=-1, keepdims=True), (rows, dk))], axis=1)
        o = _dot(att.astype(BF16), _pair_block_diag(vb)) + diag * v
        qd = (q * jnp.exp(b)).astype(BF16)
        inter = []
        for sq in range(nsq):
            r = slice(sq * cs, (sq + 1) * cs)
            ia = sq * HG_HEADS + 2 * hp
            sta, stb = st_ref[ia], st_ref[ia + 1]
            zero = jnp.zeros((dk, dk), BF16)
            st2 = jnp.concatenate(
                [jnp.concatenate([sta.astype(BF16), zero], axis=1),
                 jnp.concatenate([zero, stb.astype(BF16)], axis=1)], axis=0)
            inter.append(_dot_nt(qd[r], st2))
            blast = b[(sq + 1) * cs - 1:(sq + 1) * cs, :]
            kd = (k[r] * jnp.exp(blast - b[r])).astype(BF16)
            upd = _dot_tn(vb[r], kd)
            decay = jnp.exp(blast)
            st_ref[ia] = sta * decay[:, :dk] + upd[:dk, :dk]
            st_ref[ia + 1] = stb * decay[:, dk:] + upd[dk:, dk:]
        o = o + (inter[0] if nsq == 1 else jnp.concatenate(inter, axis=0))
        on2 = jnp.concatenate([_rms(o[:, :dk], on), _rms(o[:, dk:], on)], axis=1)
        yb_ref[rs, sl] = (on2 * _silu(hg_ref[rs, sl])).astype(yb_ref.dtype)


def _hgrn(p_all, lb_logits, o_norm, s0, *, nseq, t, cs, sb, nsub, layer, s0_layer):
    assert (sb == 1 or cs == t) and (nsub == 1 or sb == 1)
    n = nseq * t
    nc = t // (cs * nsub)
    rows = sb * cs
    nlev = len(_hgrn_levels(cs))
    nmat = sum(1 for s in _hgrn_levels(cs) if s < _SUBLANES)

    def slab(col):
        return pl.BlockSpec((nsub * rows, _SLAB), lambda b, i: (b * nc + i, col))

    st_block = (sb, HG_HEADS, HG_DK, HG_DV)
    return pl.pallas_call(
        functools.partial(_hgrn_kernel, layer=layer, cs=cs, nsub=nsub),
        grid=(nseq // sb, nc),
        in_specs=[
            slab(_C_HQ), slab(_C_HF), slab(_C_HI), slab(_C_HG),
            _resident(lb_logits.shape),
            _layer_resident(o_norm, layer),
            pl.BlockSpec((None,) + st_block, lambda b, i: (s0_layer, b, 0, 0, 0)),
        ],
        out_specs=[pl.BlockSpec((nsub * rows, _SLAB), lambda b, i: (b * nc + i, 0)),
                   pl.BlockSpec(st_block, lambda b, i: (b, 0, 0, 0))],
        out_shape=[jax.ShapeDtypeStruct((n, _SLAB), BF16),
                   jax.ShapeDtypeStruct((nseq, HG_HEADS, HG_DK, HG_DV), F32)],
        scratch_shapes=[
            pltpu.VMEM(((nmat + 1) * rows, 3 * rows), BF16),
            pltpu.VMEM((nlev, rows, 2 * rows), F32),
            pltpu.VMEM((sb * HG_HEADS, HG_DV, HG_DK), F32),
        ],
        compiler_params=_cparams(("parallel", "arbitrary")),
        name="hgrn_t%d" % cs,
    )(p_all, p_all, p_all, p_all, lb_logits, o_norm, s0)


_PAD = 8


def _lru_kernel(x_ref, g_ref, win_ref, cw_ref, cb_ref, wa_ref, ba_ref, wx_ref, bx_ref,
                lam_ref, cbuf_ref, h0_ref, yc_ref, nbuf_ref, hlast_ref,
                xe_ref, hs_ref, hc_ref):
    t = pl.program_id(1)
    rows, w = x_ref.shape
    nsq = xe_ref.shape[0]
    tt = rows // nsq
    hist = CONV_WIDTH - 1
    hn = _rms(x_ref[...], g_ref[...]).astype(BF16)

    @pl.when(t == 0)
    def _():
        for s in range(nsq):
            xe_ref[s, _PAD - hist:_PAD, :] = cbuf_ref[s]
            hc_ref[s] = h0_ref[s]

    @pl.when(t > 0)
    def _():
        for s in range(nsq):
            xe_ref[s, _PAD - hist:_PAD, :] = xe_ref[s, _PAD + tt - hist:_PAD + tt, :]

    ngroups = w // MXU_DIM
    sub =lax.broadcasted_iota(jnp.int32, (_SUBLANES, 1), 0)

    def project(g):
        return (_dot(hn, win_ref[:, g * MXU_DIM:(g + 1) * MXU_DIM]),
                _dot(hn, win_ref[:, w + g * MXU_DIM:w + (g + 1) * MXU_DIM]))

    nxt = project(0)
    for g in range(ngroups):
        cols = slice(g * MXU_DIM, (g + 1) * MXU_DIM)
        lx, lgv = nxt
        if g + 1 < ngroups:
            nxt = project(g + 1)
        xcs = []
        for s in range(nsq):
            xe_ref[s, _PAD:_PAD + tt, cols] = lx[s * tt:(s + 1) * tt]
            xc = xe_ref[s, _PAD - hist:_PAD - hist + tt, cols] * cw_ref[0:1, cols]
            for j in range(1, CONV_WIDTH):
                xc = xc + (xe_ref[s, _PAD - hist + j:_PAD - hist + j + tt, cols]
                           * cw_ref[j:j + 1, cols])
            xcs.append(xc)
        xc = (xcs[0] if nsq == 1 else jnp.concatenate(xcs, axis=0)) + cb_ref[:, cols]
        gelu = 0.5 * lgv * (
            1.0 + jnp.tanh(np.sqrt(2.0 / np.pi) * (lgv + 0.044715 * (lgv * lgv * lgv))))
        xg = xc.astype(BF16)
        ra = _dot(xg, wa_ref[g])
        rx = _dot(xg, wx_ref[g])
        r =_sigmoid(ra + ba_ref[:, cols])
        ig = _sigmoid(rx + bx_ref[:, cols])
        nl = -lam_ref[:, cols]
        softplus = jnp.maximum(nl, 0.0) + jnp.log1p(jnp.exp(-jnp.abs(nl)))
        log_a = -LRU_C * r * softplus
        a = jnp.exp(log_a)
        u = -jnp.tanh(log_a) * (a * a + 1.0)
        bv = jnp.where(u > 0.0, u * lax.rsqrt(u), 0.0) * (ig * xc)

        for s in range(nsq):
            h = jnp.broadcast_to(hc_ref[s, :, cols], (_SUBLANES, MXU_DIM))
            for r0 in range(s * tt, (s + 1) * tt, _SUBLANES):
                am = a[r0:r0 + _SUBLANES]
                bm = bv[r0:r0 + _SUBLANES]
                d = 1
                while d < _SUBLANES:
                    seen = sub >= d
                    a_prev = jnp.where(seen, pltpu.roll(am, d, 0), 1.0)
                    b_prev = jnp.where(seen, pltpu.roll(bm, d, 0), 0.0)
                    bm = am * b_prev + bm
                    am = am * a_prev
                    d *= 2
                hs = am * h + bm
                hs_ref[r0:r0 + _SUBLANES, cols] = hs
                h = jnp.broadcast_to(hs[_SUBLANES - 1:, :], hs.shape)
            hc_ref[s, :, cols] = h[:1, :]
        yc_ref[:, cols] = (hs_ref[:, cols] * gelu).astype(yc_ref.dtype)

    @pl.when(t == pl.num_programs(1) - 1)
    def _():
        for s in range(nsq):
            nbuf_ref[s] = xe_ref[s, _PAD + tt - hist:_PAD + tt, :]
            hlast_ref[s] = hc_ref[s]


def _lru(x, params, cbuf, h0, *, nseq, t, tt, sb, layer, state_layer):
    assert sb == 1 or tt == t
    n, w = x.shape
    nt = t // tt
    rows = sb * tt
    hist = CONV_WIDTH - 1
    return pl.pallas_call(
        _lru_kernel,
        grid=(nseq // sb, nt),
        in_specs=[
            pl.BlockSpec((rows, w), lambda b, i: (b * nt + i, 0)),
            *[_layer_resident(p, layer) for p in params],
            pl.BlockSpec((None, sb, hist, w), lambda b, i: (state_layer, b, 0, 0)),
            pl.BlockSpec((None, sb, 1, w), lambda b, i: (state_layer, b, 0, 0)),
        ],
        out_specs=[
            pl.BlockSpec((rows, w), lambda b, i: (b * nt + i, 0)),
            pl.BlockSpec((sb, hist, w), lambda b, i: (b, 0, 0)),
            pl.BlockSpec((sb, 1, w), lambda b, i: (b, 0, 0)),
        ],
        out_shape=[jax.ShapeDtypeStruct((n, w), BF16),
                   jax.ShapeDtypeStruct((nseq, hist, w), F32),
                   jax.ShapeDtypeStruct((nseq, 1, w), F32)],
        scratch_shapes=[
            pltpu.VMEM((sb, _PAD + tt, w), F32),
            pltpu.VMEM((rows, w), F32),
            pltpu.VMEM((sb, 1, w), F32),
        ],
        compiler_params=_cparams(("parallel", "arbitrary")),
        name="lru_t%d" % tt,
    )(x, *params, cbuf, h0)


def _merge_kernel(x_ref, ya_ref, yb_ref, yc_ref, g_ref, wg_ref,
                  wa_ref, wb_ref, wc_ref, wo_ref, o_ref):
    x = x_ref[...]
    d = x.shape[1]
    hn = _rms(x, g_ref[...]).astype(BF16)
    m = None
    for i, (y_ref, w_ref) in enumerate(((ya_ref, wa_ref), (yb_ref, wb_ref), (yc_ref, wc_ref))):
        gate = _sigmoid(_dot(hn, wg_ref[:, i * d:(i + 1) * d]))
        part = gate * _dot(y_ref[...], w_ref[...])
        m = part if m is None else m + part
    o_ref[...] = x + _dot(m.astype(BF16), wo_ref[...])


def _merge(x, ya, yb, yc, params, tm, layer):
    n, d = x.shape
    row = pl.BlockSpec((tm, d), lambda i: (i, 0))
    return pl.pallas_call(
        _merge_kernel,
        grid=(n // tm,),
        in_specs=[row, row, row, row, *[_layer_resident(p, layer) for p in params]],
        out_specs=row,
        out_shape=jax.ShapeDtypeStruct((n, d), F32),
        compiler_params=_cparams(("parallel",)),
        name="merge",
    )(x, ya, yb, yc, *params)


def _block_diag(w):
    nl, nb, bw, _ = w.shape
    per = MXU_DIM // bw
    wg = w.reshape(nl, nb // per, per, bw, bw)
    eye = jnp.eye(per, dtype=w.dtype)
    out = jnp.einsum('lgpij,pq->lgpiqj', wg, eye)
    return out.reshape(nl, nb // per, MXU_DIM, MXU_DIM).astype(BF16)


def _prepare_params(w):
    def vec(name):
        return w[name][:, None, :]

    def bf16(name):
        return w[name].astype(BF16)

    w_in = w['w_in']
    c_hgrn = _QKV_ROWS
    c_lru = c_hgrn + _N_HGRN_SLABS * _SLAB
    c_gate = c_lru + _N_LRU_SLABS * _SLAB
    assert c_gate + _N_GATE_SLABS * _SLAB == w_in.shape[2]
    qk_gain =jnp.concatenate([jnp.tile(w['q_norm'] * SCALE, (1, N_HEADS)),
                               jnp.tile(w['k_norm'], (1, N_KV_HEADS))], axis=1)
    return {
        'norm_ffn1': vec('norm_ffn1'), 'norm_mix': vec('norm_mix'), 'norm_ffn2': vec('norm_ffn2'),
        'ffn1': (bf16('w_ffn1_up'), bf16('w_ffn1_down')),
        'ffn2': (bf16('w_ffn2_up'), bf16('w_ffn2_down')),
        'w_qkv_t': jnp.swapaxes(w_in[:, :, :_QKV_ROWS], 1, 2).astype(BF16),
        'qk_gain_col': qk_gain[:, :, None],
        'attn_sinks': w['attn_sinks'],
        'merge': (vec('norm_mix'), w_in[:, :, c_gate:].astype(BF16),
                  bf16('w_attn_o'), bf16('w_hgrn_o'), bf16('w_lru_o'), bf16('w_out')),
        'hgrn_lb_logits': w['hgrn_lb_logits'], 'hgrn_o_norm': vec('hgrn_o_norm'),
        'w_hgrn_in': w_in[:, :, c_hgrn:c_lru].astype(BF16),
        'lru': (vec('norm_mix'), w_in[:, :, c_lru:c_gate].astype(BF16),
                w['conv_w'], vec('conv_b'), _block_diag(w['lru_w_a']), vec('lru_b_a'),
                _block_diag(w['lru_w_x']), vec('lru_b_x'), vec('lru_lambda')),
    }


def _tiles(n, prefer):
    for tm in prefer:
        if n % tm == 0:
            return tm
    return n


def _layer(x, pp, cache, *, nseq, t, layer, prompt):
    n, d = x.shape
    tm = _tiles(n, (512, 256, 128))
    x = _ffn(x, pp['norm_ffn1'], *pp['ffn1'], layer)
    ck, cv, s0, cbuf, h0 = cache
    state_layer = 0 if prompt else layer
    qkv_t = _qkv_t(x, pp['norm_mix'], pp['w_qkv_t'], pp['qk_gain_col'],
                   _tiles(n, (1024, 512, 256, 128)), layer)
    sinks = pp['attn_sinks'][layer]
    nq = N_HEADS * HEAD_DIM
    hgrn_in = (x, pp['norm_mix'], pp['w_hgrn_in'])
    if prompt:
        ya, p_all = _attn_prompt(qkv_t, sinks, *hgrn_in, nseq=nseq, t=t,
                                 tq=_tiles(t, (512, 256, 128)), layer=layer)
        last = jnp.stack([qkv_t[nq:, (b + 1) * t - WINDOW:(b + 1) * t] for b in range(nseq)])
        cs = _tiles(t, (128, 64, 32, 16))
        tt = _tiles(t, (1024, 512, 256, 128, 64, 32, 16))
        sb = 1
    else:
        ya, p_all = _attn_sample(qkv_t, ck, cv, sinks, *hgrn_in, nseq=nseq, t=t, layer=layer)
        last = qkv_t[nq:, :].reshape(2 * _KV_W, nseq, t).transpose(1, 0, 2)
        cs = tt = t
        sb = _tiles(nseq, (4, 2, 1))
    last = last.reshape(nseq, 2, N_KV_HEADS, HEAD_DIM, -1).transpose(1, 0, 4, 2, 3)
    yc, nbuf, hlast = _lru(x, pp['lru'], cbuf, h0, nseq=nseq, t=t, tt=tt,
                           sb=1 if prompt else _tiles(nseq, (8, 4, 2, 1)), layer=layer,
                           state_layer=state_layer)
    nsub = _tiles(t // cs, (4, 2, 1)) if prompt else 1
    yb, s_new = _hgrn(p_all, pp['hgrn_lb_logits'], pp['hgrn_o_norm'], s0, nseq=nseq, t=t, cs=cs,
                      sb=sb, nsub=nsub, layer=layer, s0_layer=state_layer)
    x = _merge(x, ya, yb, yc, pp['merge'], tm, layer)
    x = _ffn(x, pp['norm_ffn2'], *pp['ffn2'], layer)
    return x, (last[0], last[1], s_new, nbuf, hlast.reshape(nseq, d))


def _forward(x_prompt, x_sample, cache_attn_k, cache_attn_v, state_hgrn, state_conv, state_lru, w):
    bsz, seq, d = x_prompt.shape
    dbsz, dseq, _ = x_sample.shape
    depth = w['w_in'].shape[0]
    win_rows = cache_attn_k.shape[2]
    assert win_rows == WINDOW, "the attention kernels assume a full cached window"
    pp = _prepare_params(w)
    xp = x_prompt.reshape(bsz * seq, d)
    xs = x_sample.reshape(dbsz * dseq, d)
    zero_cache = (None, None,
                  jnp.zeros((1, bsz, HG_HEADS, HG_DK, HG_DV), F32),
                  jnp.zeros((1, bsz, CONV_WIDTH - 1, d), F32),
                  jnp.zeros((1, bsz, 1, d), F32))
    cache = (cache_attn_k.reshape(depth, dbsz * win_rows, _KV_W),
             cache_attn_v.reshape(depth, dbsz * win_rows, _KV_W),
             state_hgrn, state_conv, state_lru[:, :, None, :])
    st_p, st_s = [], []
    for l in range(depth):
        xp, sp = _layer(xp, pp, zero_cache, nseq=bsz, t=seq, layer=l, prompt=True)
        xs, ss = _layer(xs, pp, cache, nseq=dbsz, t=dseq, layer=l, prompt=False)
        st_p.append(sp)
        st_s.append(ss)

    def stack(sts, i):
        return jnp.stack([s[i] for s in sts], axis=0)

    return (xp.reshape(bsz, seq, d), xs.reshape(dbsz, dseq, d),
            stack(st_p, 0), stack(st_p, 1), stack(st_p, 2), stack(st_p, 3), stack(st_p, 4),
            stack(st_s, 0), stack(st_s, 1), stack(st_s, 2), stack(st_s, 3), stack(st_s, 4))


def kernel(x_prompt, x_sample, cache_attn_k, cache_attn_v, state_hgrn, state_conv, state_lru,
           norm_ffn1, w_ffn1_up, w_ffn1_down, norm_mix, w_in, q_norm, k_norm, attn_sinks, w_attn_o,
           hgrn_lb_logits, hgrn_o_norm, w_hgrn_o, conv_w, conv_b, lru_w_a, lru_b_a, lru_w_x, lru_b_x,
           lru_lambda, w_lru_o, w_out, norm_ffn2, w_ffn2_up, w_ffn2_down):
    w = dict(norm_ffn1=norm_ffn1, w_ffn1_up=w_ffn1_up, w_ffn1_down=w_ffn1_down, norm_mix=norm_mix,
             w_in=w_in, q_norm=q_norm, k_norm=k_norm, attn_sinks=attn_sinks, w_attn_o=w_attn_o,
             hgrn_lb_logits=hgrn_lb_logits, hgrn_o_norm=hgrn_o_norm, w_hgrn_o=w_hgrn_o,
             conv_w=conv_w, conv_b=conv_b, lru_w_a=lru_w_a, lru_b_a=lru_b_a, lru_w_x=lru_w_x,
             lru_b_x=lru_b_x, lru_lambda=lru_lambda, w_lru_o=w_lru_o, w_out=w_out,
             norm_ffn2=norm_ffn2, w_ffn2_up=w_ffn2_up, w_ffn2_down=w_ffn2_down)
    return _forward(x_prompt, x_sample, cache_attn_k, cache_attn_v, state_hgrn, state_conv,
                    state_lru, w)
```

```python
import functools

import numpy as np
import jax
import jax.numpy as jnp
from jax import lax
from jax.experimental import pallas as pl
from jax.experimental.pallas import tpu as pltpu

F32 = jnp.float32
BF16 = jnp.bfloat16

CHUNK = 64
N_HEADS = 16
N_KV_HEADS = 4
HEAD_DIM = 64
GROUP = N_HEADS // N_KV_HEADS
WINDOW = 128
SCALE = HEAD_DIM ** -0.5
HG_HEADS = 8
HG_DK = 128
HG_DV = 128
LOG_F_FLOOR = -60.0
CONV_WIDTH = 4
LRU_C = 8.0
LRU_BW = 64
EPS = 1e-6
NEG = -1e30
_LOG2E = float(np.log2(np.e))

MXU_DIM = 256
VMEM_LIMIT_BYTES = 56 * 1024 * 1024

_SLAB = 1024
(_C_HQ, _C_HF, _C_HI, _C_HG) = range(4)
_N_HGRN_SLABS, _N_LRU_SLABS, _N_GATE_SLABS = 4, 2, 3
_KV_W = N_KV_HEADS * HEAD_DIM
_QKV_ROWS = N_HEADS * HEAD_DIM + 2 * _KV_W


def _cparams(sem):
    return pltpu.CompilerParams(dimension_semantics=sem, vmem_limit_bytes=VMEM_LIMIT_BYTES)


def _rms(x, g):
    return x * lax.rsqrt(jnp.mean(x * x, axis=-1, keepdims=True) + EPS) * g


def _sigmoid(x):
    return 1.0 / (1.0 + jnp.exp(-x))


def _silu(x):
    return x * _sigmoid(x)


def _dot(a, b):
    return jnp.dot(a, b, preferred_element_type=F32)


def _dot_nt(a, b):
    return lax.dot_general(a, b, (((1,), (1,)), ((), ())), preferred_element_type=F32)


def _dot_tn(a, b):
    return lax.dot_general(a, b, (((0,), (0,)), ((), ())), preferred_element_type=F32)


def _ff_chunks(dff, width):
    assert dff % MXU_DIM == 0
    edges = list(range(0, dff, width)) + [dff]
    return list(zip(edges[:-1], edges[1:]))


def _ffn_kernel(x_ref, g_ref, wu_ref, wd_ref, o_ref, *, chunks):
    x = x_ref[...]
    dff = wd_ref.shape[0]
    h = _rms(x, g_ref[...]).astype(BF16)
    acc = None
    for lo, hi in chunks:
        gate = _dot(h, wu_ref[:, lo:hi])
        val = _dot(h, wu_ref[:, dff + lo:dff + hi])
        part = _dot((_silu(gate) * val).astype(BF16), wd_ref[lo:hi, :])
        acc = part if acc is None else acc + part
    o_ref[...] = x + 0.5 * acc


def _resident(shape):
    return pl.BlockSpec(shape, lambda *_: (0,) * len(shape), pipeline_mode=pl.Buffered(1))


def _layer_resident(stacked, layer):
    shape = stacked.shape[1:]
    return pl.BlockSpec((None,) + shape, lambda *_: (layer,) + (0,) * len(shape),
                        pipeline_mode=pl.Buffered(1))


_FFN_CHUNK_ELEMS = 512 * 6 * MXU_DIM


def _ffn(x, g, w_up, w_down, layer):
    n, d = x.shape
    dff = w_down.shape[1]
    tm = _tiles(n, (1024, 512, 256, 128))
    return pl.pallas_call(
        functools.partial(_ffn_kernel, chunks=_ff_chunks(dff, _FFN_CHUNK_ELEMS // tm)),
        grid=(n // tm,),
        in_specs=[
            pl.BlockSpec((tm, d), lambda i: (i, 0)),
            _layer_resident(g, layer),
            _layer_resident(w_up, layer),
            _layer_resident(w_down, layer),
        ],
        out_specs=pl.BlockSpec((tm, d), lambda i: (i, 0)),
        out_shape=jax.ShapeDtypeStruct((n, d), F32),
        compiler_params=_cparams(("parallel",)),
        name="ffn",
    )(x, g, w_up, w_down)


_LANES = 128


def _softmax_pv(s, sk, vt):
    m = jnp.maximum(jnp.max(s, axis=0, keepdims=True), sk)
    e = jnp.exp(s - m)
    den = jnp.sum(e, axis=0, keepdims=True) + jnp.exp(sk - m)
    return _dot(vt, (e * (1.0 / den)).astype(BF16))


def _group_queries(q_ref, j, lanes):
    qcat = jnp.concatenate(
        [q_ref[(j * GROUP + g) * HEAD_DIM:(j * GROUP + g + 1) * HEAD_DIM, lanes]
         for g in range(GROUP)], axis=1).astype(BF16)
    zero = jnp.zeros_like(qcat)
    return jnp.concatenate([qcat if i == j else zero for i in range(N_KV_HEADS)], axis=0)


def _sink_row(sinks_ref, j, width):
    return jnp.concatenate(
        [jnp.full((1, width), sinks_ref[j * GROUP + g], F32) for g in range(GROUP)], axis=1)


def _attn_sample_kernel(qkv_ref, kp_ref, vp_ref, sinks_ref, x_ref, g_ref, whg_ref,
                        ya_ref, p_ref, yat_ref, *, t):
    nq = N_HEADS * HEAD_DIM
    hn = _rms(x_ref[...], g_ref[...]).astype(BF16)
    pcols = p_ref.shape[1] // N_KV_HEADS
    ncache = kp_ref.shape[0]
    keys = ncache + _LANES
    kfull = jnp.concatenate([kp_ref[...], qkv_ref[nq:nq + _KV_W, :].T], axis=0).astype(BF16)
    vt = jnp.concatenate([vp_ref[...].T, qkv_ref[nq + _KV_W:, :]], axis=1).astype(BF16)
    row = lax.broadcasted_iota(jnp.int32, (keys, 1), 0)
    key_seq = jnp.where(row < ncache, row // WINDOW, (row - ncache) // t)
    lane = lax.broadcasted_iota(jnp.int32, (1, GROUP * _LANES), 1)
    valid = key_seq == (lane % _LANES) // t
    for j in range(N_KV_HEADS):
        s = jnp.where(valid, _dot(kfull, _group_queries(qkv_ref, j, slice(None))), NEG)
        o = _softmax_pv(s, _sink_row(sinks_ref, j, _LANES), vt[j * HEAD_DIM:(j + 1) * HEAD_DIM])
        for g in range(GROUP):
            yat_ref[(j * GROUP + g) * HEAD_DIM:(j * GROUP + g + 1) * HEAD_DIM, :] = (
                o[:, g * _LANES:(g + 1) * _LANES])
        pc = slice(j * pcols, (j + 1) * pcols)
        p_ref[:, pc] = _dot(hn, whg_ref[:, pc])
    ya_ref[...] = yat_ref[...].T.astype(ya_ref.dtype)


def _attn_sample(qkv_t, cache_k, cache_v, sinks, x, g, whg, *, nseq, t, layer):
    n = nseq * t
    nq = N_HEADS * HEAD_DIM
    sb = _LANES // t
    assert sb * t == _LANES and nseq % sb == 0, "new tokens must fill whole 128-lane tiles"
    d = x.shape[1]
    pcols = whg.shape[2]
    prev =pl.BlockSpec((None, sb * WINDOW, _KV_W), lambda b: (layer, b, 0))
    return pl.pallas_call(
        functools.partial(_attn_sample_kernel, t=t),
        grid=(nseq // sb,),
        in_specs=[
            pl.BlockSpec((_QKV_ROWS, _LANES), lambda b: (0, b)),
            prev, prev,
            pl.BlockSpec(memory_space=pltpu.SMEM),
            pl.BlockSpec((_LANES, d), lambda b: (b, 0)),
            _layer_resident(g, layer),
            _layer_resident(whg, layer),
        ],
        out_specs=[pl.BlockSpec((_LANES, nq), lambda b: (b, 0)),
                   pl.BlockSpec((_LANES, pcols), lambda b: (b, 0))],
        out_shape=[jax.ShapeDtypeStruct((n, nq), BF16),
                   jax.ShapeDtypeStruct((n, pcols), F32)],
        scratch_shapes=[pltpu.VMEM((nq, _LANES), F32)],
        compiler_params=_cparams(("parallel",)),
        name="attn_sample",
    )(qkv_t, cache_k, cache_v, sinks, x, g, whg)


def _qkv_t_kernel(x_ref, g_ref, w_ref, gc_ref, o_ref):
    h = _rms(x_ref[...], g_ref[...]).astype(BF16)
    acc = _dot_nt(w_ref[...], h)
    nqk = gc_ref.shape[0]
    for r in range(0, nqk, HEAD_DIM):
        blk = acc[r:r + HEAD_DIM]
        ms = jnp.sum(blk * blk, axis=0, keepdims=True) * (1.0 / HEAD_DIM)
        o_ref[r:r + HEAD_DIM, :] = blk * lax.rsqrt(ms + EPS) * gc_ref[r:r + HEAD_DIM, :]
    o_ref[nqk:, :] = acc[nqk:]


def _qkv_t(x, g, w_t, gain_col, tm, layer):
    n, d = x.shape
    rows = w_t.shape[1]
    return pl.pallas_call(
        _qkv_t_kernel,
        grid=(n // tm,),
        in_specs=[
            pl.BlockSpec((tm, d), lambda i: (i, 0)),
            _layer_resident(g, layer),
            _layer_resident(w_t, layer),
            _layer_resident(gain_col, layer),
        ],
        out_specs=pl.BlockSpec((rows, tm), lambda i: (0, i)),
        out_shape=jax.ShapeDtypeStruct((rows, n), F32),
        compiler_params=_cparams(("parallel",)),
        name="qkv_t",
    )(x, g, w_t, gain_col)


_PAIR = 2 * CHUNK
_KEYS = _PAIR + WINDOW


def _attn_prompt_kernel(q_ref, kc_ref, vc_ref, kp_ref, vp_ref, sinks_ref, x_ref, g_ref, whg_ref,
                        ya_ref, p_ref, yat_ref):
    tq = q_ref.shape[1]
    t = pl.program_id(1)
    hn = _rms(x_ref[...], g_ref[...]).astype(BF16)
    pcols = p_ref.shape[1] // ((tq // _PAIR) * N_KV_HEADS)
    lane = lax.broadcasted_iota(jnp.int32, (1, GROUP * _PAIR), 1)
    first_chunk = (lane % _PAIR) < CHUNK
    k_cur = kc_ref[...].T.astype(BF16)
    k_prev = kp_ref[...].T.astype(BF16)
    for p in range(tq // _PAIR):
        lo = p * _PAIR
        if p == 0:
            kb = jnp.concatenate([k_prev, k_cur[:_PAIR]], axis=0)
        else:
            kb = k_cur[lo - WINDOW:lo + _PAIR]
        first_key_chunk = t * (tq // CHUNK) + (lo - WINDOW) // CHUNK
        for j in range(N_KV_HEADS):
            rows = slice(j * HEAD_DIM, (j + 1) * HEAD_DIM)
            if p == 0:
                vb = jnp.concatenate([vp_ref[rows, :], vc_ref[rows, :_PAIR]], axis=1)
            else:
                vb = vc_ref[rows, lo - WINDOW:lo + _PAIR]
            s = _dot(kb, _group_queries(q_ref, j, slice(lo, lo + _PAIR)))
            s = jnp.concatenate([
                jnp.where(first_chunk & (first_key_chunk >= 0), s[:CHUNK], NEG),
                jnp.where(first_key_chunk + 1 >= 0, s[CHUNK:2 * CHUNK], NEG),
                s[2 * CHUNK:3 * CHUNK],
                jnp.where(first_chunk, NEG, s[3 * CHUNK:]),
            ], axis=0)
            o = _softmax_pv(s, _sink_row(sinks_ref, j, _PAIR), vb.astype(BF16))
            for g in range(GROUP):
                yat_ref[(j * GROUP + g) * HEAD_DIM:(j * GROUP + g + 1) * HEAD_DIM, lo:lo + _PAIR] = (
                    o[:, g * _PAIR:(g + 1) * _PAIR])
            pc = slice((p * N_KV_HEADS + j) * pcols, (p * N_KV_HEADS + j + 1) * pcols)
            p_ref[:, pc] = _dot(hn, whg_ref[:, pc])
    ya_ref[...] = yat_ref[...].T.astype(ya_ref.dtype)


def _attn_prompt(qkv_t, sinks, x, g, whg, *, nseq, t, tq, layer):
    n = nseq * t
    nt = t // tq
    nq = N_HEADS * HEAD_DIM
    kblk, vblk = nq // _KV_W, nq // _KV_W + 1
    d = x.shape[1]
    pcols = whg.shape[2]

    def prev_map(blk):
        return lambda b, i: (blk, jnp.maximum(b * (t // WINDOW) + i * (tq // WINDOW) - 1, 0))

    return pl.pallas_call(
        _attn_prompt_kernel,
        grid=(nseq, nt),
        in_specs=[
            pl.BlockSpec((nq, tq), lambda b, i: (0, b * nt + i)),
            pl.BlockSpec((_KV_W, tq), lambda b, i: (kblk, b * nt + i)),
            pl.BlockSpec((_KV_W, tq), lambda b, i: (vblk, b * nt + i)),
            pl.BlockSpec((_KV_W, WINDOW), prev_map(kblk)),
            pl.BlockSpec((_KV_W, WINDOW), prev_map(vblk)),
            pl.BlockSpec(memory_space=pltpu.SMEM),
            pl.BlockSpec((tq, d), lambda b, i: (b * nt + i, 0)),
            _layer_resident(g, layer),
            _layer_resident(whg, layer),
        ],
        out_specs=[pl.BlockSpec((tq, nq), lambda b, i: (b * nt + i, 0)),
                   pl.BlockSpec((tq, pcols), lambda b, i: (b * nt + i, 0))],
        out_shape=[jax.ShapeDtypeStruct((n, nq), BF16),
                   jax.ShapeDtypeStruct((n, pcols), F32)],
        scratch_shapes=[pltpu.VMEM((nq, tq), F32)],
        compiler_params=_cparams(("parallel", "arbitrary")),
        name="attn_prompt",
    )(qkv_t, qkv_t, qkv_t, qkv_t, qkv_t, sinks, x, g, whg)


def _hgrn_levels(c):
    return [c >> (i + 1) for i in range(int(np.log2(c)))]


_SUBLANES = 8


def _level_operands(q, k, g, s):
    rows = q.shape[0]
    if s < _SUBLANES:
        upper = (lax.broadcasted_iota(jnp.int32, (rows, 1), 0) & s) != 0
        return jnp.where(upper, q * g, 0.0), jnp.where(upper, 0.0, k * g)
    zero = jnp.zeros((s, q.shape[1]), F32)
    qparts, kparts = [], []
    for g0 in range(0, rows, 2 * s):
        lo, up = slice(g0, g0 + s), slice(g0 + s, g0 + 2 * s)
        kparts += [k[lo] * g[lo], zero]
        qparts += [zero, q[up] * g[up]]
    return jnp.concatenate(qparts, axis=0), jnp.concatenate(kparts, axis=0)


def _pair_block_diag(x):
    c = x.shape[1] // 2
    zero = jnp.zeros((x.shape[0], c), x.dtype)
    return jnp.concatenate([jnp.concatenate([x[:, :c], zero], axis=1),
                            jnp.concatenate([zero, x[:, c:]], axis=1)], axis=0)


def _hgrn_kernel(hq_ref, hf_ref, hi_ref, hg_ref, lbl_ref, on_ref, s0_ref, yb_ref, sout_ref,
                 a_ref, m_ref, st_ref, *, layer, cs, nsub):
    c = pl.program_id(1)
    rows = hq_ref.shape[0] // nsub
    nsq = rows // cs
    levels = _hgrn_levels(cs)

    @pl.when(c == 0)
    def _():
        row = lax.broadcasted_iota(jnp.int32, (rows, rows), 0)
        col = lax.broadcasted_iota(jnp.int32, (rows, rows), 1)
        mats = [((col <= row) & ((row & ~(cs - 1)) == (col & ~(cs - 1)))).astype(F32)]
        for i, s in enumerate(levels):
            mid = (row & ~(2 * s - 1)) + s
            upper = (row & s) != 0
            if s < _SUBLANES:
                up = ((col >= mid) & (col <= row)).astype(F32)
                lo = ((col > row) & (col < mid)).astype(F32)
                mats.append(jnp.where(upper, up, lo))
            same =(row & ~(2 * s - 1)) == (col & ~(2 * s - 1))
            same = same.astype(F32)
            m_ref[i] = jnp.concatenate([same, same], axis=1)
        for i, a in enumerate(mats):
            a = a.astype(BF16)
            a_ref[i * rows:(i + 1) * rows, :] = jnp.concatenate([a, a, a], axis=1)
        for sq in range(nsq):
            for h in range(HG_HEADS):
                st_ref[sq * HG_HEADS + h] = s0_ref[sq, h].T

    lg = lbl_ref[...]
    e = jnp.exp(lg - jnp.max(lg, axis=0, keepdims=True))
    lb = jnp.zeros((1, e.shape[1]), F32)
    for i in range(1, layer + 1):
        lb = lb + e[i:i + 1]
    lb = lb / jnp.sum(e, axis=0, keepdims=True)

    for sub in range(nsub):
        _hgrn_block(slice(sub * rows, (sub + 1) * rows), hq_ref, hf_ref, hi_ref, hg_ref, yb_ref,
                    a_ref, m_ref, st_ref, lb, on_ref[...], cs=cs)

    @pl.when(c == pl.num_programs(1) - 1)
    def _():
        for sq in range(nsq):
            for h in range(HG_HEADS):
                sout_ref[sq, h] = st_ref[sq * HG_HEADS + h].T


def _hgrn_block(rs, hq_ref, hf_ref, hi_ref, hg_ref, yb_ref, a_ref, m_ref, st_ref, lb, on, *, cs):
    rows = rs.stop - rs.start
    nsq = rows // cs
    levels = _hgrn_levels(cs)
    sg = _sigmoid(hf_ref[rs])
    f = lb + (1.0 - lb) * sg
    logf = jnp.maximum(jnp.log(jnp.maximum(f, 1e-26)), LOG_F_FLOOR) * _LOG2E
    kk =(1.0 - lb) * (1.0 - sg)
    l1 = logf.astype(BF16)
    r1 = logf - l1.astype(F32)
    l2 = r1.astype(BF16)
    l3 = (r1 - l2.astype(F32)).astype(BF16)
    pieces = jnp.concatenate([l1, l2, l3], axis=0)

    lin = _dot(a_ref[...], pieces)
    bc = lin[:rows]
    ex, nmat = [], 1
    for s in levels:
        if s < _SUBLANES:
            ex.append(lin[nmat * rows:(nmat + 1) * rows])
            nmat += 1
        else:
            ex.append(jnp.concatenate(
                [-jnp.abs(bc[g0:g0 + 2 * s] - bc[g0 + s - 1:g0 + s]) for g0 in range(0, rows, 2 * s)],
                axis=0))

    dk = HG_DK
    for hp in range(HG_HEADS // 2):
        sl = slice(2 * hp * dk, 2 * (hp + 1) * dk)
        q = _silu(hq_ref[rs, sl])
        v = hi_ref[rs, sl]
        vb = v.astype(BF16)
        k = kk[:, sl]
        b = bc[:, sl]
        att = None
        for i, s in enumerate(levels):
            qs, ks = _level_operands(q, k, jnp.exp2(ex[i][:, sl]), s)
            part = _dot_nt(qs.astype(BF16), _pair_block_diag(ks.astype(BF16)))
            if 2 * s < rows:
                part = part * m_ref[i]
            att = part if att is None else att + part
        qk = q * k
        diag = jnp.concatenate(
            [jnp.broadcast_to(jnp.sum(qk[:, :dk], axis=-1, keepdims=True), (rows, dk)),
             jnp.broadcast_to(jnp.sum(qk[:, dk:], axis=-1, keepdims=True), (rows, dk))], axis=1)
        o = _dot(att.astype(BF16), _pair_block_diag(vb)) + diag * v
        qd = (q * jnp.exp2(b)).astype(BF16)
        inter = []
        for sq in range(nsq):
            r = slice(sq * cs, (sq + 1) * cs)
            ia = sq * HG_HEADS + 2 * hp
            sta, stb = st_ref[ia], st_ref[ia + 1]
            zero = jnp.zeros((dk, dk), BF16)
            st2 = jnp.concatenate(
                [jnp.concatenate([sta.astype(BF16), zero], axis=1),
                 jnp.concatenate([zero, stb.astype(BF16)], axis=1)], axis=0)
            inter.append(_dot_nt(qd[r], st2))
            blast = b[(sq + 1) * cs - 1:(sq + 1) * cs, :]
            kd = (k[r] * jnp.exp2(blast - b[r])).astype(BF16)
            upd = _dot_tn(vb[r], kd)
            decay = jnp.exp2(blast)
            st_ref[ia] = sta * decay[:, :dk] + upd[:dk, :dk]
            st_ref[ia + 1] = stb * decay[:, dk:] + upd[dk:, dk:]
        o = o + (inter[0] if nsq == 1 else jnp.concatenate(inter, axis=0))
        on2 = jnp.concatenate([_rms(o[:, :dk], on), _rms(o[:, dk:], on)], axis=1)
        yb_ref[rs, sl] = (on2 * _silu(hg_ref[rs, sl])).astype(yb_ref.dtype)


def _hgrn(p_all, lb_logits, o_norm, s0, *, nseq, t, cs, sb, nsub, layer, s0_layer):
    assert (sb == 1 or cs == t) and (nsub == 1 or sb == 1)
    n = nseq * t
    nc = t // (cs * nsub)
    rows = sb * cs
    nlev = len(_hgrn_levels(cs))
    nmat = sum(1 for s in _hgrn_levels(cs) if s < _SUBLANES)

    def slab(col):
        return pl.BlockSpec((nsub * rows, _SLAB), lambda b, i: (b * nc + i, col))

    st_block = (sb, HG_HEADS, HG_DK, HG_DV)
    return pl.pallas_call(
        functools.partial(_hgrn_kernel, layer=layer, cs=cs, nsub=nsub),
        grid=(nseq // sb, nc),
        in_specs=[
            slab(_C_HQ), slab(_C_HF), slab(_C_HI), slab(_C_HG),
            _resident(lb_logits.shape),
            _layer_resident(o_norm, layer),
            pl.BlockSpec((None,) + st_block, lambda b, i: (s0_layer, b, 0, 0, 0)),
        ],
        out_specs=[pl.BlockSpec((nsub * rows, _SLAB), lambda b, i: (b * nc + i, 0)),
                   pl.BlockSpec(st_block, lambda b, i: (b, 0, 0, 0))],
        out_shape=[jax.ShapeDtypeStruct((n, _SLAB), BF16),
                   jax.ShapeDtypeStruct((nseq, HG_HEADS, HG_DK, HG_DV), F32)],
        scratch_shapes=[
            pltpu.VMEM(((nmat + 1) * rows, 3 * rows), BF16),
            pltpu.VMEM((nlev, rows, 2 * rows), F32),
            pltpu.VMEM((sb * HG_HEADS, HG_DV, HG_DK), F32),
        ],
        compiler_params=_cparams(("parallel", "arbitrary")),
        name="hgrn_t%d" % cs,
    )(p_all, p_all, p_all, p_all, lb_logits, o_norm, s0)


_PAD = 8


def _lru_kernel(x_ref, g_ref, win_ref, cw_ref, cb_ref, wa_ref, ba_ref, wx_ref, bx_ref,
                lam_ref, cbuf_ref, h0_ref, yc_ref, nbuf_ref, hlast_ref,
                xe_ref, hs_ref, hc_ref):
    t = pl.program_id(1)
    rows, w = x_ref.shape
    nsq = xe_ref.shape[0]
    tt = rows // nsq
    hist = CONV_WIDTH - 1
    hn = _rms(x_ref[...], g_ref[...]).astype(BF16)

    @pl.when(t == 0)
    def _():
        for s in range(nsq):
            xe_ref[s, _PAD - hist:_PAD, :] = cbuf_ref[s]
            hc_ref[s] = h0_ref[s]

    @pl.when(t > 0)
    def _():
        for s in range(nsq):
            xe_ref[s, _PAD - hist:_PAD, :] = xe_ref[s, _PAD + tt - hist:_PAD + tt, :]

    ngroups = w // MXU_DIM
    sub =lax.broadcasted_iota(jnp.int32, (_SUBLANES, 1), 0)

    def project(g):
        return (_dot(hn, win_ref[:, g * MXU_DIM:(g + 1) * MXU_DIM]),
                _dot(hn, win_ref[:, w + g * MXU_DIM:w + (g + 1) * MXU_DIM]))

    nxt = project(0)
    for g in range(ngroups):
        cols = slice(g * MXU_DIM, (g + 1) * MXU_DIM)
        lx, lgv = nxt
        if g + 1 < ngroups:
            nxt = project(g + 1)
        xcs = []
        for s in range(nsq):
            xe_ref[s, _PAD:_PAD + tt, cols] = lx[s * tt:(s + 1) * tt]
            xc = xe_ref[s, _PAD - hist:_PAD - hist + tt, cols] * cw_ref[0:1, cols]
            for j in range(1, CONV_WIDTH):
                xc = xc + (xe_ref[s, _PAD - hist + j:_PAD - hist + j + tt, cols]
                           * cw_ref[j:j + 1, cols])
            xcs.append(xc)
        xc = (xcs[0] if nsq == 1 else jnp.concatenate(xcs, axis=0)) + cb_ref[:, cols]
        gelu = 0.5 * lgv * (
            1.0 + jnp.tanh(np.sqrt(2.0 / np.pi) * (lgv + 0.044715 * (lgv * lgv * lgv))))
        xg = xc.astype(BF16)
        ra = _dot(xg, wa_ref[g])
        rx = _dot(xg, wx_ref[g])
        r =_sigmoid(ra + ba_ref[:, cols])
        ig = _sigmoid(rx + bx_ref[:, cols])
        nl = -lam_ref[:, cols]
        softplus = jnp.maximum(nl, 0.0) + jnp.log1p(jnp.exp(-jnp.abs(nl)))
        log_a = -LRU_C * r * softplus
        a = jnp.exp(log_a)
        u = -jnp.tanh(log_a) * (a * a + 1.0)
        bv = jnp.where(u > 0.0, u * lax.rsqrt(u), 0.0) * (ig * xc)

        for s in range(nsq):
            h = jnp.broadcast_to(hc_ref[s, :, cols], (_SUBLANES, MXU_DIM))
            for r0 in range(s * tt, (s + 1) * tt, _SUBLANES):
                am = a[r0:r0 + _SUBLANES]
                bm = bv[r0:r0 + _SUBLANES]
                d = 1
                while d < _SUBLANES:
                    seen = sub >= d
                    a_prev = jnp.where(seen, pltpu.roll(am, d, 0), 1.0)
                    b_prev = jnp.where(seen, pltpu.roll(bm, d, 0), 0.0)
                    bm = am * b_prev + bm
                    am = am * a_prev
                    d *= 2
                hs = am * h + bm
                hs_ref[r0:r0 + _SUBLANES, cols] = hs
                h = jnp.broadcast_to(hs[_SUBLANES - 1:, :], hs.shape)
            hc_ref[s, :, cols] = h[:1, :]
        yc_ref[:, cols] = (hs_ref[:, cols] * gelu).astype(yc_ref.dtype)

    @pl.when(t == pl.num_programs(1) - 1)
    def _():
        for s in range(nsq):
            nbuf_ref[s] = xe_ref[s, _PAD + tt - hist:_PAD + tt, :]
            hlast_ref[s] = hc_ref[s]


def _lru(x, params, cbuf, h0, *, nseq, t, tt, sb, layer, state_layer):
    assert sb == 1 or tt == t
    n, w = x.shape
    nt = t // tt
    rows = sb * tt
    hist = CONV_WIDTH - 1
    return pl.pallas_call(
        _lru_kernel,
        grid=(nseq // sb, nt),
        in_specs=[
            pl.BlockSpec((rows, w), lambda b, i: (b * nt + i, 0)),
            *[_layer_resident(p, layer) for p in params],
            pl.BlockSpec((None, sb, hist, w), lambda b, i: (state_layer, b, 0, 0)),
            pl.BlockSpec((None, sb, 1, w), lambda b, i: (state_layer, b, 0, 0)),
        ],
        out_specs=[
            pl.BlockSpec((rows, w), lambda b, i: (b * nt + i, 0)),
            pl.BlockSpec((sb, hist, w), lambda b, i: (b, 0, 0)),
            pl.BlockSpec((sb, 1, w), lambda b, i: (b, 0, 0)),
        ],
        out_shape=[jax.ShapeDtypeStruct((n, w), BF16),
                   jax.ShapeDtypeStruct((nseq, hist, w), F32),
                   jax.ShapeDtypeStruct((nseq, 1, w), F32)],
        scratch_shapes=[
            pltpu.VMEM((sb, _PAD + tt, w), F32),
            pltpu.VMEM((rows, w), F32),
            pltpu.VMEM((sb, 1, w), F32),
        ],
        compiler_params=_cparams(("parallel", "arbitrary")),
        name="lru_t%d" % tt,
    )(x, *params, cbuf, h0)


def _merge_kernel(x_ref, ya_ref, yb_ref, yc_ref, g_ref, wg_ref,
                  wa_ref, wb_ref, wc_ref, wo_ref, o_ref):
    x = x_ref[...]
    d = x.shape[1]
    hn = _rms(x, g_ref[...]).astype(BF16)
    m = None
    for i, (y_ref, w_ref) in enumerate(((ya_ref, wa_ref), (yb_ref, wb_ref), (yc_ref, wc_ref))):
        gate = _sigmoid(_dot(hn, wg_ref[:, i * d:(i + 1) * d]))
        part = gate * _dot(y_ref[...], w_ref[...])
        m = part if m is None else m + part
    o_ref[...] = x + _dot(m.astype(BF16), wo_ref[...])


def _merge(x, ya, yb, yc, params, tm, layer):
    n, d = x.shape
    row = pl.BlockSpec((tm, d), lambda i: (i, 0))
    return pl.pallas_call(
        _merge_kernel,
        grid=(n // tm,),
        in_specs=[row, row, row, row, *[_layer_resident(p, layer) for p in params]],
        out_specs=row,
        out_shape=jax.ShapeDtypeStruct((n, d), F32),
        compiler_params=_cparams(("parallel",)),
        name="merge",
    )(x, ya, yb, yc, *params)


def _block_diag(w):
    nl, nb, bw, _ = w.shape
    per = MXU_DIM // bw
    wg = w.reshape(nl, nb // per, per, bw, bw)
    eye = jnp.eye(per, dtype=w.dtype)
    out = jnp.einsum('lgpij,pq->lgpiqj', wg, eye)
    return out.reshape(nl, nb // per, MXU_DIM, MXU_DIM).astype(BF16)


def _prepare_params(w):
    def vec(name):
        return w[name][:, None, :]

    def bf16(name):
        return w[name].astype(BF16)

    w_in = w['w_in']
    c_hgrn = _QKV_ROWS
    c_lru = c_hgrn + _N_HGRN_SLABS * _SLAB
    c_gate = c_lru + _N_LRU_SLABS * _SLAB
    assert c_gate + _N_GATE_SLABS * _SLAB == w_in.shape[2]
    qk_gain =jnp.concatenate([jnp.tile(w['q_norm'] * SCALE, (1, N_HEADS)),
                               jnp.tile(w['k_norm'], (1, N_KV_HEADS))], axis=1)
    return {
        'norm_ffn1': vec('norm_ffn1'), 'norm_mix': vec('norm_mix'), 'norm_ffn2': vec('norm_ffn2'),
        'ffn1': (bf16('w_ffn1_up'), bf16('w_ffn1_down')),
        'ffn2': (bf16('w_ffn2_up'), bf16('w_ffn2_down')),
        'w_qkv_t': jnp.swapaxes(w_in[:, :, :_QKV_ROWS], 1, 2).astype(BF16),
        'qk_gain_col': qk_gain[:, :, None],
        'attn_sinks': w['attn_sinks'],
        'merge': (vec('norm_mix'), w_in[:, :, c_gate:].astype(BF16),
                  bf16('w_attn_o'), bf16('w_hgrn_o'), bf16('w_lru_o'), bf16('w_out')),
        'hgrn_lb_logits': w['hgrn_lb_logits'], 'hgrn_o_norm': vec('hgrn_o_norm'),
        'w_hgrn_in': w_in[:, :, c_hgrn:c_lru].astype(BF16),
        'lru': (vec('norm_mix'), w_in[:, :, c_lru:c_gate].astype(BF16),
                w['conv_w'], vec('conv_b'), _block_diag(w['lru_w_a']), vec('lru_b_a'),
                _block_diag(w['lru_w_x']), vec('lru_b_x'), vec('lru_lambda')),
    }


def _tiles(n, prefer):
    for tm in prefer:
        if n % tm == 0:
            return tm
    return n


def _layer(x, pp, cache, *, nseq, t, layer, prompt):
    n, d = x.shape
    tm = _tiles(n, (512, 256, 128))
    x = _ffn(x, pp['norm_ffn1'], *pp['ffn1'], layer)
    ck, cv, s0, cbuf, h0 = cache
    state_layer = 0 if prompt else layer
    qkv_t = _qkv_t(x, pp['norm_mix'], pp['w_qkv_t'], pp['qk_gain_col'],
                   _tiles(n, (1024, 512, 256, 128)), layer)
    sinks = pp['attn_sinks'][layer]
    nq = N_HEADS * HEAD_DIM
    hgrn_in = (x, pp['norm_mix'], pp['w_hgrn_in'])
    if prompt:
        ya, p_all = _attn_prompt(qkv_t, sinks, *hgrn_in, nseq=nseq, t=t,
                                 tq=_tiles(t, (512, 256, 128)), layer=layer)
        last = jnp.stack([qkv_t[nq:, (b + 1) * t - WINDOW:(b + 1) * t] for b in range(nseq)])
        cs = _tiles(t, (128, 64, 32, 16))
        tt = _tiles(t, (1024, 512, 256, 128, 64, 32, 16))
        sb = 1
    else:
        ya, p_all = _attn_sample(qkv_t, ck, cv, sinks, *hgrn_in, nseq=nseq, t=t, layer=layer)
        last = qkv_t[nq:, :].reshape(2 * _KV_W, nseq, t).transpose(1, 0, 2)
        cs = tt = t
        sb = _tiles(nseq, (4, 2, 1))
    last = last.reshape(nseq, 2, N_KV_HEADS, HEAD_DIM, -1).transpose(1, 0, 4, 2, 3)
    yc, nbuf, hlast = _lru(x, pp['lru'], cbuf, h0, nseq=nseq, t=t, tt=tt,
                           sb=1 if prompt else _tiles(nseq, (8, 4, 2, 1)), layer=layer,
                           state_layer=state_layer)
    nsub = _tiles(t // cs, (4, 2, 1)) if prompt else 1
    yb, s_new = _hgrn(p_all, pp['hgrn_lb_logits'], pp['hgrn_o_norm'], s0, nseq=nseq, t=t, cs=cs,
                      sb=sb, nsub=nsub, layer=layer, s0_layer=state_layer)
    x = _merge(x, ya, yb, yc, pp['merge'], tm, layer)
    x = _ffn(x, pp['norm_ffn2'], *pp['ffn2'], layer)
    return x, (last[0], last[1], s_new, nbuf, hlast.reshape(nseq, d))


def _forward(x_prompt, x_sample, cache_attn_k, cache_attn_v, state_hgrn, state_conv, state_lru, w):
    bsz, seq, d = x_prompt.shape
    dbsz, dseq, _ = x_sample.shape
    depth = w['w_in'].shape[0]
    win_rows = cache_attn_k.shape[2]
    assert win_rows == WINDOW, "the attention kernels assume a full cached window"
    pp = _prepare_params(w)
    xp = x_prompt.reshape(bsz * seq, d)
    xs = x_sample.reshape(dbsz * dseq, d)
    zero_cache = (None, None,
                  jnp.zeros((1, bsz, HG_HEADS, HG_DK, HG_DV), F32),
                  jnp.zeros((1, bsz, CONV_WIDTH - 1, d), F32),
                  jnp.zeros((1, bsz, 1, d), F32))
    cache = (cache_attn_k.reshape(depth, dbsz * win_rows, _KV_W),
             cache_attn_v.reshape(depth, dbsz * win_rows, _KV_W),
             state_hgrn, state_conv, state_lru[:, :, None, :])
    st_p, st_s = [], []
    for l in range(depth):
        xp, sp = _layer(xp, pp, zero_cache, nseq=bsz, t=seq, layer=l, prompt=True)
        xs, ss = _layer(xs, pp, cache, nseq=dbsz, t=dseq, layer=l, prompt=False)
        st_p.append(sp)
        st_s.append(ss)

    def stack(sts, i):
        return jnp.stack([s[i] for s in sts], axis=0)

    return (xp.reshape(bsz, seq, d), xs.reshape(dbsz, dseq, d),
            stack(st_p, 0), stack(st_p, 1), stack(st_p, 2), stack(st_p, 3), stack(st_p, 4),
            stack(st_s, 0), stack(st_s, 1), stack(st_s, 2), stack(st_s, 3), stack(st_s, 4))


def kernel(x_prompt, x_sample, cache_attn_k, cache_attn_v, state_hgrn, state_conv, state_lru,
           norm_ffn1, w_ffn1_up, w_ffn1_down, norm_mix, w_in, q_norm, k_norm, attn_sinks, w_attn_o,
           hgrn_lb_logits, hgrn_o_norm, w_hgrn_o, conv_w, conv_b, lru_w_a, lru_b_a, lru_w_x, lru_b_x,
           lru_lambda, w_lru_o, w_out, norm_ffn2, w_ffn2_up, w_ffn2_down):
    w = dict(norm_ffn1=norm_ffn1, w_ffn1_up=w_ffn1_up, w_ffn1_down=w_ffn1_down, norm_mix=norm_mix,
             w_in=w_in, q_norm=q_norm, k_norm=k_norm, attn_sinks=attn_sinks, w_attn_o=w_attn_o,
             hgrn_lb_logits=hgrn_lb_logits, hgrn_o_norm=hgrn_o_norm, w_hgrn_o=w_hgrn_o,
             conv_w=conv_w, conv_b=conv_b, lru_w_a=lru_w_a, lru_b_a=lru_b_a, lru_w_x=lru_w_x,
             lru_b_x=lru_b_x, lru_lambda=lru_lambda, w_lru_o=w_lru_o, w_out=w_out,
             norm_ffn2=norm_ffn2, w_ffn2_up=w_ffn2_up, w_ffn2_down=w_ffn2_down)
    return _forward(x_prompt, x_sample, cache_attn_k, cache_attn_v, state_hgrn, state_conv,
                    state_lru, w)
```

```python
import functools

import numpy as np
import jax
import jax.numpy as jnp
from jax import lax
from jax.experimental import pallas as pl
from jax.experimental.pallas import tpu as pltpu

F32 = jnp.float32
BF16 = jnp.bfloat16

CHUNK = 64
N_HEADS = 16
N_KV_HEADS = 4
HEAD_DIM = 64
GROUP = N_HEADS // N_KV_HEADS
WINDOW = 128
SCALE = HEAD_DIM ** -0.5
HG_HEADS = 8
HG_DK = 128
HG_DV = 128
LOG_F_FLOOR = -60.0
CONV_WIDTH = 4
LRU_C = 8.0
EPS = 1e-6
NEG = -1e30
_LOG2E = float(np.log2(np.e))

MXU_DIM = 256
VMEM_LIMIT_BYTES = 56 * 1024 * 1024

_SLAB = 1024
(_C_HQ, _C_HF, _C_HI, _C_HG) = range(4)
_N_HGRN_SLABS, _N_LRU_SLABS, _N_GATE_SLABS = 4, 2, 3
_KV_W = N_KV_HEADS * HEAD_DIM
_QKV_ROWS = N_HEADS * HEAD_DIM + 2 * _KV_W


def _cparams(sem):
    return pltpu.CompilerParams(dimension_semantics=sem, vmem_limit_bytes=VMEM_LIMIT_BYTES)


def _rms(x, g):
    return x * lax.rsqrt(jnp.mean(x * x, axis=-1, keepdims=True) + EPS) * g


def _sigmoid(x):
    return 1.0 / (1.0 + jnp.exp(-x))


def _silu(x):
    return x * _sigmoid(x)


def _dot(a, b):
    return jnp.dot(a, b, preferred_element_type=F32)


def _dot_nt(a, b):
    return lax.dot_general(a, b, (((1,), (1,)), ((), ())), preferred_element_type=F32)


def _dot_tn(a, b):
    return lax.dot_general(a, b, (((0,), (0,)), ((), ())), preferred_element_type=F32)


def _ff_chunks(dff, width):
    assert dff % MXU_DIM == 0
    edges = list(range(0, dff, width)) + [dff]
    return list(zip(edges[:-1], edges[1:]))


def _ffn_kernel(x_ref, g_ref, wu_ref, wd_ref, o_ref, *, chunks):
    x = x_ref[...]
    dff = wd_ref.shape[0]
    h = _rms(x, g_ref[...]).astype(BF16)
    acc = None
    for lo, hi in chunks:
        gate = _dot(h, wu_ref[:, lo:hi])
        val = _dot(h, wu_ref[:, dff + lo:dff + hi])
        part = _dot((_silu(gate) * val).astype(BF16), wd_ref[lo:hi, :])
        acc = part if acc is None else acc + part
    o_ref[...] = x + 0.5 * acc


def _resident(shape):
    return pl.BlockSpec(shape, lambda *_: (0,) * len(shape), pipeline_mode=pl.Buffered(1))


def _layer_resident(stacked, layer):
    shape = stacked.shape[1:]
    return pl.BlockSpec((None,) + shape, lambda *_: (layer,) + (0,) * len(shape),
                        pipeline_mode=pl.Buffered(1))


_FFN_CHUNK_ELEMS = 512 * 6 * MXU_DIM


def _ffn(x, g, w_up, w_down, layer):
    n, d = x.shape
    dff = w_down.shape[1]
    tm = _tiles(n, (1024, 512, 256, 128))
    return pl.pallas_call(
        functools.partial(_ffn_kernel, chunks=_ff_chunks(dff, _FFN_CHUNK_ELEMS // tm)),
        grid=(n // tm,),
        in_specs=[
            pl.BlockSpec((tm, d), lambda i: (i, 0)),
            _layer_resident(g, layer),
            _layer_resident(w_up, layer),
            _layer_resident(w_down, layer),
        ],
        out_specs=pl.BlockSpec((tm, d), lambda i: (i, 0)),
        out_shape=jax.ShapeDtypeStruct((n, d), F32),
        compiler_params=_cparams(("parallel",)),
        name="ffn",
    )(x, g, w_up, w_down)


_LANES = 128


def _softmax_pv(s, sk, vt):
    m = jnp.maximum(jnp.max(s, axis=0, keepdims=True), sk)
    e = jnp.exp(s - m)
    den = jnp.sum(e, axis=0, keepdims=True) + jnp.exp(sk - m)
    return _dot(vt, (e * (1.0 / den)).astype(BF16))


def _group_queries(q_ref, j, lanes):
    qcat = jnp.concatenate(
        [q_ref[(j * GROUP + g) * HEAD_DIM:(j * GROUP + g + 1) * HEAD_DIM, lanes]
         for g in range(GROUP)], axis=1).astype(BF16)
    zero = jnp.zeros_like(qcat)
    return jnp.concatenate([qcat if i == j else zero for i in range(N_KV_HEADS)], axis=0)


def _sink_row(sinks_ref, j, width):
    return jnp.concatenate(
        [jnp.full((1, width), sinks_ref[j * GROUP + g], F32) for g in range(GROUP)], axis=1)


def _attn_sample_kernel(qkv_ref, kp_ref, vp_ref, sinks_ref, x_ref, g_ref, whg_ref,
                        ya_ref, p_ref, yat_ref, *, t):
    nq = N_HEADS * HEAD_DIM
    hn = _rms(x_ref[...], g_ref[...]).astype(BF16)
    pcols = p_ref.shape[1] // N_KV_HEADS
    ncache = kp_ref.shape[0]
    keys = ncache + _LANES
    kfull = jnp.concatenate([kp_ref[...], qkv_ref[nq:nq + _KV_W, :].T], axis=0).astype(BF16)
    vt = jnp.concatenate([vp_ref[...].T, qkv_ref[nq + _KV_W:, :]], axis=1).astype(BF16)
    row = lax.broadcasted_iota(jnp.int32, (keys, 1), 0)
    key_seq = jnp.where(row < ncache, row // WINDOW, (row - ncache) // t)
    lane = lax.broadcasted_iota(jnp.int32, (1, GROUP * _LANES), 1)
    valid = key_seq == (lane % _LANES) // t
    for j in range(N_KV_HEADS):
        s = jnp.where(valid, _dot(kfull, _group_queries(qkv_ref, j, slice(None))), NEG)
        o = _softmax_pv(s, _sink_row(sinks_ref, j, _LANES), vt[j * HEAD_DIM:(j + 1) * HEAD_DIM])
        for g in range(GROUP):
            yat_ref[(j * GROUP + g) * HEAD_DIM:(j * GROUP + g + 1) * HEAD_DIM, :] = (
                o[:, g * _LANES:(g + 1) * _LANES])
        pc = slice(j * pcols, (j + 1) * pcols)
        p_ref[:, pc] = _dot(hn, whg_ref[:, pc])
    ya_ref[...] = yat_ref[...].T.astype(ya_ref.dtype)


def _attn_sample(qkv_t, cache_k, cache_v, sinks, x, g, whg, *, nseq, t, layer):
    n = nseq * t
    nq = N_HEADS * HEAD_DIM
    sb = _LANES // t
    assert sb * t == _LANES and nseq % sb == 0, "new tokens must fill whole 128-lane tiles"
    d = x.shape[1]
    pcols = whg.shape[2]
    prev =pl.BlockSpec((None, sb * WINDOW, _KV_W), lambda b: (layer, b, 0))
    return pl.pallas_call(
        functools.partial(_attn_sample_kernel, t=t),
        grid=(nseq // sb,),
        in_specs=[
            pl.BlockSpec((_QKV_ROWS, _LANES), lambda b: (0, b)),
            prev, prev,
            pl.BlockSpec(memory_space=pltpu.SMEM),
            pl.BlockSpec((_LANES, d), lambda b: (b, 0)),
            _layer_resident(g, layer),
            _layer_resident(whg, layer),
        ],
        out_specs=[pl.BlockSpec((_LANES, nq), lambda b: (b, 0)),
                   pl.BlockSpec((_LANES, pcols), lambda b: (b, 0))],
        out_shape=[jax.ShapeDtypeStruct((n, nq), BF16),
                   jax.ShapeDtypeStruct((n, pcols), F32)],
        scratch_shapes=[pltpu.VMEM((nq, _LANES), F32)],
        compiler_params=_cparams(("parallel",)),
        name="attn_sample",
    )(qkv_t, cache_k, cache_v, sinks, x, g, whg)


def _qkv_t_kernel(x_ref, g_ref, w_ref, gc_ref, o_ref):
    h = _rms(x_ref[...], g_ref[...]).astype(BF16)
    acc = _dot_nt(w_ref[...], h)
    nqk = gc_ref.shape[0]
    for r in range(0, nqk, HEAD_DIM):
        blk = acc[r:r + HEAD_DIM]
        ms = jnp.sum(blk * blk, axis=0, keepdims=True) * (1.0 / HEAD_DIM)
        o_ref[r:r + HEAD_DIM, :] = blk * lax.rsqrt(ms + EPS) * gc_ref[r:r + HEAD_DIM, :]
    o_ref[nqk:, :] = acc[nqk:]


def _qkv_t(x, g, w_t, gain_col, tm, layer):
    n, d = x.shape
    rows = w_t.shape[1]
    return pl.pallas_call(
        _qkv_t_kernel,
        grid=(n // tm,),
        in_specs=[
            pl.BlockSpec((tm, d), lambda i: (i, 0)),
            _layer_resident(g, layer),
            _layer_resident(w_t, layer),
            _layer_resident(gain_col, layer),
        ],
        out_specs=pl.BlockSpec((rows, tm), lambda i: (0, i)),
        out_shape=jax.ShapeDtypeStruct((rows, n), F32),
        compiler_params=_cparams(("parallel",)),
        name="qkv_t",
    )(x, g, w_t, gain_col)


_PAIR = 2 * CHUNK
_KEYS = _PAIR + WINDOW


def _attn_prompt_kernel(q_ref, kc_ref, vc_ref, kp_ref, vp_ref, sinks_ref, x_ref, g_ref, whg_ref,
                        ya_ref, p_ref, yat_ref):
    tq = q_ref.shape[1]
    t = pl.program_id(1)
    hn = _rms(x_ref[...], g_ref[...]).astype(BF16)
    pcols = p_ref.shape[1] // ((tq // _PAIR) * N_KV_HEADS)
    lane = lax.broadcasted_iota(jnp.int32, (1, GROUP * _PAIR), 1)
    first_chunk = (lane % _PAIR) < CHUNK
    k_cur = kc_ref[...].T.astype(BF16)
    k_prev = kp_ref[...].T.astype(BF16)
    for p in range(tq // _PAIR):
        lo = p * _PAIR
        if p == 0:
            kb = jnp.concatenate([k_prev, k_cur[:_PAIR]], axis=0)
        else:
            kb = k_cur[lo - WINDOW:lo + _PAIR]
        first_key_chunk = t * (tq // CHUNK) + (lo - WINDOW) // CHUNK
        for j in range(N_KV_HEADS):
            rows = slice(j * HEAD_DIM, (j + 1) * HEAD_DIM)
            if p == 0:
                vb = jnp.concatenate([vp_ref[rows, :], vc_ref[rows, :_PAIR]], axis=1)
            else:
                vb = vc_ref[rows, lo - WINDOW:lo + _PAIR]
            s = _dot(kb, _group_queries(q_ref, j, slice(lo, lo + _PAIR)))
            s = jnp.concatenate([
                jnp.where(first_chunk & (first_key_chunk >= 0), s[:CHUNK], NEG),
                jnp.where(first_key_chunk + 1 >= 0, s[CHUNK:2 * CHUNK], NEG),
                s[2 * CHUNK:3 * CHUNK],
                jnp.where(first_chunk, NEG, s[3 * CHUNK:]),
            ], axis=0)
            o = _softmax_pv(s, _sink_row(sinks_ref, j, _PAIR), vb.astype(BF16))
            for g in range(GROUP):
                yat_ref[(j * GROUP + g) * HEAD_DIM:(j * GROUP + g + 1) * HEAD_DIM, lo:lo + _PAIR] = (
                    o[:, g * _PAIR:(g + 1) * _PAIR])
            pc = slice((p * N_KV_HEADS + j) * pcols, (p * N_KV_HEADS + j + 1) * pcols)
            p_ref[:, pc] = _dot(hn, whg_ref[:, pc])
    ya_ref[...] = yat_ref[...].T.astype(ya_ref.dtype)


def _attn_prompt(qkv_t, sinks, x, g, whg, *, nseq, t, tq, layer):
    n = nseq * t
    nt = t // tq
    nq = N_HEADS * HEAD_DIM
    kblk, vblk = nq // _KV_W, nq // _KV_W + 1
    d = x.shape[1]
    pcols = whg.shape[2]

    def prev_map(blk):
        return lambda b, i: (blk, jnp.maximum(b * (t // WINDOW) + i * (tq // WINDOW) - 1, 0))

    return pl.pallas_call(
        _attn_prompt_kernel,
        grid=(nseq, nt),
        in_specs=[
            pl.BlockSpec((nq, tq), lambda b, i: (0, b * nt + i)),
            pl.BlockSpec((_KV_W, tq), lambda b, i: (kblk, b * nt + i)),
            pl.BlockSpec((_KV_W, tq), lambda b, i: (vblk, b * nt + i)),
            pl.BlockSpec((_KV_W, WINDOW), prev_map(kblk)),
            pl.BlockSpec((_KV_W, WINDOW), prev_map(vblk)),
            pl.BlockSpec(memory_space=pltpu.SMEM),
            pl.BlockSpec((tq, d), lambda b, i: (b * nt + i, 0)),
            _layer_resident(g, layer),
            _layer_resident(whg, layer),
        ],
        out_specs=[pl.BlockSpec((tq, nq), lambda b, i: (b * nt + i, 0)),
                   pl.BlockSpec((tq, pcols), lambda b, i: (b * nt + i, 0))],
        out_shape=[jax.ShapeDtypeStruct((n, nq), BF16),
                   jax.ShapeDtypeStruct((n, pcols), F32)],
        scratch_shapes=[pltpu.VMEM((nq, tq), F32)],
        compiler_params=_cparams(("parallel", "arbitrary")),
        name="attn_prompt",
    )(qkv_t, qkv_t, qkv_t, qkv_t, qkv_t, sinks, x, g, whg)


def _hgrn_levels(c):
    return [c >> (i + 1) for i in range(int(np.log2(c)))]


_SUBLANES = 8


def _level_operands(q, k, g, s):
    rows = q.shape[0]
    if s < _SUBLANES:
        upper = (lax.broadcasted_iota(jnp.int32, (rows, 1), 0) & s) != 0
        return jnp.where(upper, q * g, 0.0), jnp.where(upper, 0.0, k * g)
    zero = jnp.zeros((s, q.shape[1]), F32)
    qparts, kparts = [], []
    for g0 in range(0, rows, 2 * s):
        lo, up = slice(g0, g0 + s), slice(g0 + s, g0 + 2 * s)
        kparts += [k[lo] * g[lo], zero]
        qparts += [zero, q[up] * g[up]]
    return jnp.concatenate(qparts, axis=0), jnp.concatenate(kparts, axis=0)


def _pair_block_diag(x):
    c = x.shape[1] // 2
    zero = jnp.zeros((x.shape[0], c), x.dtype)
    return jnp.concatenate([jnp.concatenate([x[:, :c], zero], axis=1),
                            jnp.concatenate([zero, x[:, c:]], axis=1)], axis=0)


def _hgrn_kernel(hq_ref, hf_ref, hi_ref, hg_ref, lbl_ref, on_ref, s0_ref, yb_ref, sout_ref,
                 a_ref, m_ref, st_ref, *, layer, cs, nsub):
    c = pl.program_id(1)
    rows = hq_ref.shape[0] // nsub
    nsq = rows // cs
    levels = _hgrn_levels(cs)

    @pl.when(c == 0)
    def _():
        row = lax.broadcasted_iota(jnp.int32, (rows, rows), 0)
        col = lax.broadcasted_iota(jnp.int32, (rows, rows), 1)
        mats = [((col <= row) & ((row & ~(cs - 1)) == (col & ~(cs - 1)))).astype(F32)]
        for i, s in enumerate(levels):
            mid = (row & ~(2 * s - 1)) + s
            upper = (row & s) != 0
            if s < _SUBLANES:
                up = ((col >= mid) & (col <= row)).astype(F32)
                lo = ((col > row) & (col < mid)).astype(F32)
                mats.append(jnp.where(upper, up, lo))
            same =(row & ~(2 * s - 1)) == (col & ~(2 * s - 1))
            same = same.astype(F32)
            m_ref[i] = jnp.concatenate([same, same], axis=1)
        for i, a in enumerate(mats):
            a = a.astype(BF16)
            a_ref[i * rows:(i + 1) * rows, :] = jnp.concatenate([a, a, a], axis=1)
        for sq in range(nsq):
            for h in range(HG_HEADS):
                st_ref[sq * HG_HEADS + h] = s0_ref[sq, h].T

    lg = lbl_ref[...]
    e = jnp.exp(lg - jnp.max(lg, axis=0, keepdims=True))
    lb = jnp.zeros((1, e.shape[1]), F32)
    for i in range(1, layer + 1):
        lb = lb + e[i:i + 1]
    lb = lb / jnp.sum(e, axis=0, keepdims=True)

    for sub in range(nsub):
        _hgrn_block(slice(sub * rows, (sub + 1) * rows), hq_ref, hf_ref, hi_ref, hg_ref, yb_ref,
                    a_ref, m_ref, st_ref, lb, on_ref[...], cs=cs)

    @pl.when(c == pl.num_programs(1) - 1)
    def _():
        for sq in range(nsq):
            for h in range(HG_HEADS):
                sout_ref[sq, h] = st_ref[sq * HG_HEADS + h].T


def _hgrn_block(rs, hq_ref, hf_ref, hi_ref, hg_ref, yb_ref, a_ref, m_ref, st_ref, lb, on, *, cs):
    rows = rs.stop - rs.start
    nsq = rows // cs
    levels = _hgrn_levels(cs)
    sg = _sigmoid(hf_ref[rs])
    f = lb + (1.0 - lb) * sg
    logf = jnp.maximum(jnp.log(jnp.maximum(f, 1e-26)), LOG_F_FLOOR) * _LOG2E
    kk =(1.0 - lb) * (1.0 - sg)
    l1 = logf.astype(BF16)
    r1 = logf - l1.astype(F32)
    l2 = r1.astype(BF16)
    l3 = (r1 - l2.astype(F32)).astype(BF16)
    pieces = jnp.concatenate([l1, l2, l3], axis=0)

    lin = _dot(a_ref[...], pieces)
    bc = lin[:rows]
    ex, nmat = [], 1
    for s in levels:
        if s < _SUBLANES:
            ex.append(lin[nmat * rows:(nmat + 1) * rows])
            nmat += 1
        else:
            ex.append(jnp.concatenate(
                [-jnp.abs(bc[g0:g0 + 2 * s] - bc[g0 + s - 1:g0 + s]) for g0 in range(0, rows, 2 * s)],
                axis=0))

    dk = HG_DK
    for hp in range(HG_HEADS // 2):
        sl = slice(2 * hp * dk, 2 * (hp + 1) * dk)
        q = _silu(hq_ref[rs, sl])
        v = hi_ref[rs, sl]
        vb = v.astype(BF16)
        k = kk[:, sl]
        b = bc[:, sl]
        att = None
        for i, s in enumerate(levels):
            qs, ks = _level_operands(q, k, jnp.exp2(ex[i][:, sl]), s)
            part = _dot_nt(qs.astype(BF16), _pair_block_diag(ks.astype(BF16)))
            if 2 * s < rows:
                part = part * m_ref[i]
            att = part if att is None else att + part
        qk = q * k
        diag = jnp.concatenate(
            [jnp.broadcast_to(jnp.sum(qk[:, :dk], axis=-1, keepdims=True), (rows, dk)),
             jnp.broadcast_to(jnp.sum(qk[:, dk:], axis=-1, keepdims=True), (rows, dk))], axis=1)
        o = _dot(att.astype(BF16), _pair_block_diag(vb)) + diag * v
        qd = (q * jnp.exp2(b)).astype(BF16)
        inter = []
        for sq in range(nsq):
            r = slice(sq * cs, (sq + 1) * cs)
            ia = sq * HG_HEADS + 2 * hp
            sta, stb = st_ref[ia], st_ref[ia + 1]
            zero = jnp.zeros((dk, dk), BF16)
            st2 = jnp.concatenate(
                [jnp.concatenate([sta.astype(BF16), zero], axis=1),
                 jnp.concatenate([zero, stb.astype(BF16)], axis=1)], axis=0)
            inter.append(_dot_nt(qd[r], st2))
            blast = b[(sq + 1) * cs - 1:(sq + 1) * cs, :]
            kd = (k[r] * jnp.exp2(blast - b[r])).astype(BF16)
            upd = _dot_tn(vb[r], kd)
            decay = jnp.exp2(blast)
            st_ref[ia] = sta * decay[:, :dk] + upd[:dk, :dk]
            st_ref[ia + 1] = stb * decay[:, dk:] + upd[dk:, dk:]
        o = o + (inter[0] if nsq == 1 else jnp.concatenate(inter, axis=0))
        on2 = jnp.concatenate([_rms(o[:, :dk], on), _rms(o[:, dk:], on)], axis=1)
        yb_ref[rs, sl] = (on2 * _silu(hg_ref[rs, sl])).astype(yb_ref.dtype)


def _hgrn(p_all, lb_logits, o_norm, s0, *, nseq, t, cs, sb, nsub, layer, s0_layer):
    assert (sb == 1 or cs == t) and (nsub == 1 or sb == 1)
    n = nseq * t
    nc = t // (cs * nsub)
    rows = sb * cs
    nlev = len(_hgrn_levels(cs))
    nmat = sum(1 for s in _hgrn_levels(cs) if s < _SUBLANES)

    def slab(col):
        return pl.BlockSpec((nsub * rows, _SLAB), lambda b, i: (b * nc + i, col))

    st_block = (sb, HG_HEADS, HG_DK, HG_DV)
    return pl.pallas_call(
        functools.partial(_hgrn_kernel, layer=layer, cs=cs, nsub=nsub),
        grid=(nseq // sb, nc),
        in_specs=[
            slab(_C_HQ), slab(_C_HF), slab(_C_HI), slab(_C_HG),
            _resident(lb_logits.shape),
            _layer_resident(o_norm, layer),
            pl.BlockSpec((None,) + st_block, lambda b, i: (s0_layer, b, 0, 0, 0)),
        ],
        out_specs=[pl.BlockSpec((nsub * rows, _SLAB), lambda b, i: (b * nc + i, 0)),
                   pl.BlockSpec(st_block, lambda b, i: (b, 0, 0, 0))],
        out_shape=[jax.ShapeDtypeStruct((n, _SLAB), BF16),
                   jax.ShapeDtypeStruct((nseq, HG_HEADS, HG_DK, HG_DV), F32)],
        scratch_shapes=[
            pltpu.VMEM(((nmat + 1) * rows, 3 * rows), BF16),
            pltpu.VMEM((nlev, rows, 2 * rows), F32),
            pltpu.VMEM((sb * HG_HEADS, HG_DV, HG_DK), F32),
        ],
        compiler_params=_cparams(("parallel", "arbitrary")),
        name="hgrn_t%d" % cs,
    )(p_all, p_all, p_all, p_all, lb_logits, o_norm, s0)


_PAD = 8


def _lru_kernel(x_ref, g_ref, win_ref, cw_ref, cb_ref, wa_ref, ba_ref, wx_ref, bx_ref,
                lam_ref, cbuf_ref, h0_ref, yc_ref, nbuf_ref, hlast_ref,
                xe_ref, hs_ref, hc_ref):
    t = pl.program_id(1)
    rows, w = x_ref.shape
    nsq = xe_ref.shape[0]
    tt = rows // nsq
    hist = CONV_WIDTH - 1
    hn = _rms(x_ref[...], g_ref[...]).astype(BF16)

    @pl.when(t == 0)
    def _():
        for s in range(nsq):
            xe_ref[s, _PAD - hist:_PAD, :] = cbuf_ref[s]
            hc_ref[s] = h0_ref[s]

    @pl.when(t > 0)
    def _():
        for s in range(nsq):
            xe_ref[s, _PAD - hist:_PAD, :] = xe_ref[s, _PAD + tt - hist:_PAD + tt, :]

    ngroups = w // MXU_DIM
    sub =lax.broadcasted_iota(jnp.int32, (_SUBLANES, 1), 0)

    def project(g):
        return (_dot(hn, win_ref[:, g * MXU_DIM:(g + 1) * MXU_DIM]),
                _dot(hn, win_ref[:, w + g * MXU_DIM:w + (g + 1) * MXU_DIM]))

    nxt = project(0)
    for g in range(ngroups):
        cols = slice(g * MXU_DIM, (g + 1) * MXU_DIM)
        lx, lgv = nxt
        if g + 1 < ngroups:
            nxt = project(g + 1)
        xcs = []
        for s in range(nsq):
            xe_ref[s, _PAD:_PAD + tt, cols] = lx[s * tt:(s + 1) * tt]
            xc = xe_ref[s, _PAD - hist:_PAD - hist + tt, cols] * cw_ref[0:1, cols]
            for j in range(1, CONV_WIDTH):
                xc = xc + (xe_ref[s, _PAD - hist + j:_PAD - hist + j + tt, cols]
                           * cw_ref[j:j + 1, cols])
            xcs.append(xc)
        xc = (xcs[0] if nsq == 1 else jnp.concatenate(xcs, axis=0)) + cb_ref[:, cols]
        gelu = 0.5 * lgv * (
            1.0 + jnp.tanh(np.sqrt(2.0 / np.pi) * (lgv + 0.044715 * (lgv * lgv * lgv))))
        xg = xc.astype(BF16)
        ra = _dot(xg, wa_ref[g])
        rx = _dot(xg, wx_ref[g])
        r =_sigmoid(ra + ba_ref[:, cols])
        ig = _sigmoid(rx + bx_ref[:, cols])
        nl = -lam_ref[:, cols]
        softplus = jnp.maximum(nl, 0.0) + jnp.log1p(jnp.exp(-jnp.abs(nl)))
        log_a = -LRU_C * r * softplus
        a = jnp.exp(log_a)
        u = -jnp.tanh(log_a) * (a * a + 1.0)
        bv = jnp.where(u > 0.0, u * lax.rsqrt(u), 0.0) * (ig * xc)

        for s in range(nsq):
            h = jnp.broadcast_to(hc_ref[s, :, cols], (_SUBLANES, MXU_DIM))
            for r0 in range(s * tt, (s + 1) * tt, _SUBLANES):
                am = a[r0:r0 + _SUBLANES]
                bm = bv[r0:r0 + _SUBLANES]
                d = 1
                while d < _SUBLANES:
                    seen = sub >= d
                    a_prev = jnp.where(seen, pltpu.roll(am, d, 0), 1.0)
                    b_prev = jnp.where(seen, pltpu.roll(bm, d, 0), 0.0)
                    bm = am * b_prev + bm
                    am = am * a_prev
                    d *= 2
                hs = am * h + bm
                hs_ref[r0:r0 + _SUBLANES, cols] = hs
                h = jnp.broadcast_to(hs[_SUBLANES - 1:, :], hs.shape)
            hc_ref[s, :, cols] = h[:1, :]
        yc_ref[:, cols] = (hs_ref[:, cols] * gelu).astype(yc_ref.dtype)

    @pl.when(t == pl.num_programs(1) - 1)
    def _():
        for s in range(nsq):
            nbuf_ref[s] = xe_ref[s, _PAD + tt - hist:_PAD + tt, :]
            hlast_ref[s] = hc_ref[s]


def _lru(x, params, cbuf, h0, *, nseq, t, tt, sb, layer, state_layer):
    assert sb == 1 or tt == t
    n, w = x.shape
    nt = t // tt
    rows = sb * tt
    hist = CONV_WIDTH - 1
    return pl.pallas_call(
        _lru_kernel,
        grid=(nseq // sb, nt),
        in_specs=[
            pl.BlockSpec((rows, w), lambda b, i: (b * nt + i, 0)),
            *[_layer_resident(p, layer) for p in params],
            pl.BlockSpec((None, sb, hist, w), lambda b, i: (state_layer, b, 0, 0)),
            pl.BlockSpec((None, sb, 1, w), lambda b, i: (state_layer, b, 0, 0)),
        ],
        out_specs=[
            pl.BlockSpec((rows, w), lambda b, i: (b * nt + i, 0)),
            pl.BlockSpec((sb, hist, w), lambda b, i: (b, 0, 0)),
            pl.BlockSpec((sb, 1, w), lambda b, i: (b, 0, 0)),
        ],
        out_shape=[jax.ShapeDtypeStruct((n, w), BF16),
                   jax.ShapeDtypeStruct((nseq, hist, w), F32),
                   jax.ShapeDtypeStruct((nseq, 1, w), F32)],
        scratch_shapes=[
            pltpu.VMEM((sb, _PAD + tt, w), F32),
            pltpu.VMEM((rows, w), F32),
            pltpu.VMEM((sb, 1, w), F32),
        ],
        compiler_params=_cparams(("parallel", "arbitrary")),
        name="lru_t%d" % tt,
    )(x, *params, cbuf, h0)


def _merge_kernel(x_ref, ya_ref, yb_ref, yc_ref, g_ref, wg_ref,
                  wa_ref, wb_ref, wc_ref, wo_ref, o_ref):
    x = x_ref[...]
    d = x.shape[1]
    hn = _rms(x, g_ref[...]).astype(BF16)
    m = None
    for i, (y_ref, w_ref) in enumerate(((ya_ref, wa_ref), (yb_ref, wb_ref), (yc_ref, wc_ref))):
        gate = _sigmoid(_dot(hn, wg_ref[:, i * d:(i + 1) * d]))
        part = gate * _dot(y_ref[...], w_ref[...])
        m = part if m is None else m + part
    o_ref[...] = x + _dot(m.astype(BF16), wo_ref[...])


def _merge(x, ya, yb, yc, params, tm, layer):
    n, d = x.shape
    row = pl.BlockSpec((tm, d), lambda i: (i, 0))
    return pl.pallas_call(
        _merge_kernel,
        grid=(n // tm,),
        in_specs=[row, row, row, row, *[_layer_resident(p, layer) for p in params]],
        out_specs=row,
        out_shape=jax.ShapeDtypeStruct((n, d), F32),
        compiler_params=_cparams(("parallel",)),
        name="merge",
    )(x, ya, yb, yc, *params)


def _block_diag(w):
    nl, nb, bw, _ = w.shape
    per = MXU_DIM // bw
    wg = w.reshape(nl, nb // per, per, bw, bw)
    eye = jnp.eye(per, dtype=w.dtype)
    out = jnp.einsum('lgpij,pq->lgpiqj', wg, eye)
    return out.reshape(nl, nb // per, MXU_DIM, MXU_DIM).astype(BF16)


def _prepare_params(w):
    def vec(name):
        return w[name][:, None, :]

    def bf16(name):
        return w[name].astype(BF16)

    w_in = w['w_in']
    c_hgrn = _QKV_ROWS
    c_lru = c_hgrn + _N_HGRN_SLABS * _SLAB
    c_gate = c_lru + _N_LRU_SLABS * _SLAB
    assert c_gate + _N_GATE_SLABS * _SLAB == w_in.shape[2]
    qk_gain =jnp.concatenate([jnp.tile(w['q_norm'] * SCALE, (1, N_HEADS)),
                               jnp.tile(w['k_norm'], (1, N_KV_HEADS))], axis=1)
    return {
        'norm_ffn1': vec('norm_ffn1'), 'norm_mix': vec('norm_mix'), 'norm_ffn2': vec('norm_ffn2'),
        'ffn1': (bf16('w_ffn1_up'), bf16('w_ffn1_down')),
        'ffn2': (bf16('w_ffn2_up'), bf16('w_ffn2_down')),
        'w_qkv_t': jnp.swapaxes(w_in[:, :, :_QKV_ROWS], 1, 2).astype(BF16),
        'qk_gain_col': qk_gain[:, :, None],
        'attn_sinks': w['attn_sinks'],
        'merge': (vec('norm_mix'), w_in[:, :, c_gate:].astype(BF16),
                  bf16('w_attn_o'), bf16('w_hgrn_o'), bf16('w_lru_o'), bf16('w_out')),
        'hgrn_lb_logits': w['hgrn_lb_logits'], 'hgrn_o_norm': vec('hgrn_o_norm'),
        'w_hgrn_in': w_in[:, :, c_hgrn:c_lru].astype(BF16),
        'lru': (vec('norm_mix'), w_in[:, :, c_lru:c_gate].astype(BF16),
                w['conv_w'], vec('conv_b'), _block_diag(w['lru_w_a']), vec('lru_b_a'),
                _block_diag(w['lru_w_x']), vec('lru_b_x'), vec('lru_lambda')),
    }


def _tiles(n, prefer):
    for tm in prefer:
        if n % tm == 0:
            return tm
    return n


def _layer(x, pp, cache, *, nseq, t, layer, prompt):
    n, d = x.shape
    tm = _tiles(n, (512, 256, 128))
    x = _ffn(x, pp['norm_ffn1'], *pp['ffn1'], layer)
    ck, cv, s0, cbuf, h0 = cache
    state_layer = 0 if prompt else layer
    qkv_t = _qkv_t(x, pp['norm_mix'], pp['w_qkv_t'], pp['qk_gain_col'],
                   _tiles(n, (1024, 512, 256, 128)), layer)
    sinks = pp['attn_sinks'][layer]
    nq = N_HEADS * HEAD_DIM
    hgrn_in = (x, pp['norm_mix'], pp['w_hgrn_in'])
    if prompt:
        ya, p_all = _attn_prompt(qkv_t, sinks, *hgrn_in, nseq=nseq, t=t,
                                 tq=_tiles(t, (512, 256, 128)), layer=layer)
        last = jnp.stack([qkv_t[nq:, (b + 1) * t - WINDOW:(b + 1) * t] for b in range(nseq)])
        cs = _tiles(t, (128, 64, 32, 16))
        tt = _tiles(t, (1024, 512, 256, 128, 64, 32, 16))
        sb = 1
    else:
        ya, p_all = _attn_sample(qkv_t, ck, cv, sinks, *hgrn_in, nseq=nseq, t=t, layer=layer)
        last = qkv_t[nq:, :].reshape(2 * _KV_W, nseq, t).transpose(1, 0, 2)
        cs = tt = t
        sb = _tiles(nseq, (4, 2, 1))
    last = last.reshape(nseq, 2, N_KV_HEADS, HEAD_DIM, -1).transpose(1, 0, 4, 2, 3)
    yc, nbuf, hlast = _lru(x, pp['lru'], cbuf, h0, nseq=nseq, t=t, tt=tt,
                           sb=1 if prompt else _tiles(nseq, (8, 4, 2, 1)), layer=layer,
                           state_layer=state_layer)
    nsub = _tiles(t // cs, (4, 2, 1)) if prompt else 1
    yb, s_new = _hgrn(p_all, pp['hgrn_lb_logits'], pp['hgrn_o_norm'], s0, nseq=nseq, t=t, cs=cs,
                      sb=sb, nsub=nsub, layer=layer, s0_layer=state_layer)
    x = _merge(x, ya, yb, yc, pp['merge'], tm, layer)
    x = _ffn(x, pp['norm_ffn2'], *pp['ffn2'], layer)
    return x, (last[0], last[1], s_new, nbuf, hlast.reshape(nseq, d))


def _forward(x_prompt, x_sample, cache_attn_k, cache_attn_v, state_hgrn, state_conv, state_lru, w):
    bsz, seq, d = x_prompt.shape
    dbsz, dseq, _ = x_sample.shape
    depth = w['w_in'].shape[0]
    win_rows = cache_attn_k.shape[2]
    assert win_rows == WINDOW, "the attention kernels assume a full cached window"
    pp = _prepare_params(w)
    xp = x_prompt.reshape(bsz * seq, d)
    xs = x_sample.reshape(dbsz * dseq, d)
    zero_cache = (None, None,
                  jnp.zeros((1, bsz, HG_HEADS, HG_DK, HG_DV), F32),
                  jnp.zeros((1, bsz, CONV_WIDTH - 1, d), F32),
                  jnp.zeros((1, bsz, 1, d), F32))
    cache = (cache_attn_k.reshape(depth, dbsz * win_rows, _KV_W),
             cache_attn_v.reshape(depth, dbsz * win_rows, _KV_W),
             state_hgrn, state_conv, state_lru[:, :, None, :])
    st_p, st_s = [], []
    for l in range(depth):
        xp, sp = _layer(xp, pp, zero_cache, nseq=bsz, t=seq, layer=l, prompt=True)
        xs, ss = _layer(xs, pp, cache, nseq=dbsz, t=dseq, layer=l, prompt=False)
        st_p.append(sp)
        st_s.append(ss)

    def stack(sts, i):
        return jnp.stack([s[i] for s in sts], axis=0)

    return (xp.reshape(bsz, seq, d), xs.reshape(dbsz, dseq, d),
            stack(st_p, 0), stack(st_p, 1), stack(st_p, 2), stack(st_p, 3), stack(st_p, 4),
            stack(st_s, 0), stack(st_s, 1), stack(st_s, 2), stack(st_s, 3), stack(st_s, 4))


def kernel(x_prompt, x_sample, cache_attn_k, cache_attn_v, state_hgrn, state_conv, state_lru,
           norm_ffn1, w_ffn1_up, w_ffn1_down, norm_mix, w_in, q_norm, k_norm, attn_sinks, w_attn_o,
           hgrn_lb_logits, hgrn_o_norm, w_hgrn_o, conv_w, conv_b, lru_w_a, lru_b_a, lru_w_x, lru_b_x,
           lru_lambda, w_lru_o, w_out, norm_ffn2, w_ffn2_up, w_ffn2_down):
    w = dict(norm_ffn1=norm_ffn1, w_ffn1_up=w_ffn1_up, w_ffn1_down=w_ffn1_down, norm_mix=norm_mix,
             w_in=w_in, q_norm=q_norm, k_norm=k_norm, attn_sinks=attn_sinks, w_attn_o=w_attn_o,
             hgrn_lb_logits=hgrn_lb_logits, hgrn_o_norm=hgrn_o_norm, w_hgrn_o=w_hgrn_o,
             conv_w=conv_w, conv_b=conv_b, lru_w_a=lru_w_a, lru_b_a=lru_b_a, lru_w_x=lru_w_x,
             lru_b_x=lru_b_x, lru_lambda=lru_lambda, w_lru_o=w_lru_o, w_out=w_out,
             norm_ffn2=norm_ffn2, w_ffn2_up=w_ffn2_up, w_ffn2_down=w_ffn2_down)
    return _forward(x_prompt, x_sample, cache_attn_k, cache_attn_v, state_hgrn, state_conv,
                    state_lru, w)
```
